```python
import math
import jax
import jax.numpy as jnp
from jax import lax
import numpy as np

D_MODEL = 1024
BATCH = 4
SEQ = 4096
DEPTH = 4
DEC_BATCH = 16
DEC_SEQ = 64
PAST_LEN = 4096

CHUNK = 64
N_MIXERS = 3
N_HEADS = 16
N_KV_HEADS = 4
HEAD_DIM = 64
GQA_GROUP = N_HEADS // N_KV_HEADS
WINDOW = 128
WIN_CHUNKS = WINDOW // CHUNK
NUM_BUCKETS = 32
MAX_DISTANCE = 128
CONV_WIDTH = 31
MIX_BLOCK = 128
CMLP_GROUPS = 4
CMLP_DIM = 2 * D_MODEL
D_FF = 2816
FFN_CONV_WIDTH = 3
N_ATTN_LAYERS = (DEPTH + 2) // 3
N_CONV_LAYERS = (DEPTH + 1) // 3
N_CMLP_LAYERS = DEPTH // 3
RMS_EPS = 1e-6
LN_EPS = 1e-5

kernel_name = 'hybrid_chunk_causal_streaming_encoder_step'


def _rmsnorm(x, g):
    xf = x.astype(jnp.float32)
    y = xf * lax.rsqrt(jnp.mean(xf * xf, axis=-1, keepdims=True) + RMS_EPS)
    return (y * g.astype(jnp.float32)).astype(x.dtype)


def _layernorm(x, g, b):
    xf = x.astype(jnp.float32)
    mu = jnp.mean(xf, axis=-1, keepdims=True)
    var = jnp.mean(jnp.square(xf - mu), axis=-1, keepdims=True)
    y = (xf - mu) * lax.rsqrt(var + LN_EPS) * g.astype(jnp.float32) + b.astype(jnp.float32)
    return y.astype(x.dtype)


def _t5_bucket(rel):
    half = NUM_BUCKETS // 2
    max_exact = half // 2
    n = jnp.abs(rel)
    log_ratio = jnp.log(jnp.maximum(n, 1).astype(jnp.float32) / max_exact) / math.log(MAX_DISTANCE / max_exact)
    large = jnp.minimum(max_exact + (log_ratio * (half - max_exact)).astype(jnp.int32), half - 1)
    return jnp.where(rel > 0, half, 0) + jnp.where(n < max_exact, n, large)


def _rel_bias(table, q_pos, k_pos):
    b = table[_t5_bucket(k_pos[None, :] - q_pos[:, None])]
    b = jnp.transpose(b, (2, 0, 1)).astype(jnp.float32)
    return b.reshape(N_KV_HEADS, GQA_GROUP, q_pos.shape[0], k_pos.shape[0])


def _sink_attend(q, k, v, bias, valid, sinks):
    s = jnp.einsum('...qkgd,...jkd->...kgqj', q, k, preferred_element_type=jnp.float32)
    s = jnp.where(valid, s * (HEAD_DIM ** -0.5) + bias, -jnp.inf)
    sink = sinks.astype(jnp.float32).reshape(N_KV_HEADS, GQA_GROUP, 1, 1)
    m = jnp.maximum(jnp.max(s, axis=-1, keepdims=True), sink)
    p = jnp.exp(s - m)
    p = p / (jnp.sum(p, axis=-1, keepdims=True) + jnp.exp(sink - m))
    return jnp.einsum('...kgqj,...jkd->...qkgd', p.astype(v.dtype), v)


def _qkv(h, w_qkv, b_qkv):
    qkv = h @ w_qkv + b_qkv
    return jnp.split(qkv, [N_HEADS * HEAD_DIM, (N_HEADS + N_KV_HEADS) * HEAD_DIM], axis=-1)


def _attn_prompt(h, w_qkv, b_qkv, w_o, b_o, sinks, rel_table):
    B, S, _ = h.shape
    nc = S // CHUNK
    q, k, v = _qkv(h, w_qkv, b_qkv)
    q = q.reshape(B, nc, CHUNK, N_KV_HEADS, GQA_GROUP, HEAD_DIM)
    k = k.reshape(B, S, N_KV_HEADS, HEAD_DIM)
    v = v.reshape(B, S, N_KV_HEADS, HEAD_DIM)
    pad = ((0, 0), (WINDOW, 0), (0, 0), (0, 0))
    kc = jnp.pad(k, pad).reshape(B, nc + WIN_CHUNKS, CHUNK, N_KV_HEADS, HEAD_DIM)
    vc = jnp.pad(v, pad).reshape(B, nc + WIN_CHUNKS, CHUNK, N_KV_HEADS, HEAD_DIM)
    kb = jnp.concatenate([kc[:, i:i + nc] for i in range(WIN_CHUNKS + 1)], axis=2)
    vb = jnp.concatenate([vc[:, i:i + nc] for i in range(WIN_CHUNKS + 1)], axis=2)
    q_pos = jnp.arange(CHUNK, dtype=jnp.int32)
    k_pos = jnp.arange(WINDOW + CHUNK, dtype=jnp.int32) - WINDOW
    bias = _rel_bias(rel_table, q_pos, k_pos)
    valid = (jnp.arange(nc, dtype=jnp.int32)[:, None] * CHUNK + k_pos[None, :]) >= 0
    valid = valid[:, None, None, None, :]
    o = _sink_attend(q, kb, vb, bias, valid, sinks)
    y = o.reshape(B, S, N_HEADS * HEAD_DIM) @ w_o + b_o
    return y, k[:, -WINDOW:], v[:, -WINDOW:]


def _attn_sample(h, cache_k, cache_v, w_qkv, b_qkv, w_o, b_o, sinks, rel_table):
    B, T, _ = h.shape
    W = cache_k.shape[1]
    q, k, v = _qkv(h, w_qkv, b_qkv)
    q = q.reshape(B, T, N_KV_HEADS, GQA_GROUP, HEAD_DIM)
    k = k.reshape(B, T, N_KV_HEADS, HEAD_DIM)
    v = v.reshape(B, T, N_KV_HEADS, HEAD_DIM)
    kk = jnp.concatenate([cache_k, k], axis=1)
    vv = jnp.concatenate([cache_v, v], axis=1)
    q_pos = jnp.arange(T, dtype=jnp.int32)
    k_pos = jnp.arange(W + T, dtype=jnp.int32) - W
    bias = _rel_bias(rel_table, q_pos, k_pos)
    o = _sink_attend(q, kk, vv, bias, True, sinks)
    y = o.reshape(B, T, N_HEADS * HEAD_DIM) @ w_o + b_o
    return y, k, v


def _dwconv(xp, w, b):
    y = lax.conv_general_dilated(xp, w[:, None, :], (1,), 'VALID',
                                 dimension_numbers=('NWC', 'WIO', 'NWC'),
                                 feature_group_count=xp.shape[-1])
    return y + b


def _conv_module(h, state, w_pw1, b_pw1, w_dw, b_dw, ln_g, ln_b, w_pw2, b_pw2):
    a, g = jnp.split(h @ w_pw1 + b_pw1, 2, axis=-1)
    glu = a * jax.nn.sigmoid(g)
    xp = jnp.concatenate([state, glu], axis=1)
    c = jax.nn.silu(_layernorm(_dwconv(xp, w_dw, b_dw), ln_g, ln_b))
    return c @ w_pw2 + b_pw2, xp[:, -(CONV_WIDTH - 1):]


def _chunk_mlp(h, w_in, b_in, ln_g, ln_b, w_s, b_s, w_out, b_out):
    B, T, _ = h.shape
    L = min(T, MIX_BLOCK)
    n = T // L
    u, v = jnp.split(jax.nn.gelu(h @ w_in + b_in), 2, axis=-1)
    v = _layernorm(v, ln_g, ln_b)
    vb = v.reshape(B, n, L, CMLP_GROUPS, CMLP_DIM // CMLP_GROUPS)
    pos = jnp.arange(L)
    mask = (pos[None, :] // CHUNK) <= (pos[:, None] // CHUNK)
    ws = jnp.where(mask, w_s[:, :L, :L], 0.0)
    gate = jnp.einsum('gij,bnjgc->bnigc', ws, vb) + b_s[:, :L].T[None, None, :, :, None]
    y = u * gate.reshape(B, T, CMLP_DIM)
    return y @ w_out + b_out, v


def _conv_ffn(h, state, w_up, w_dw, b_dw, w_down):
    up = h @ w_up
    xp = jnp.concatenate([state, up], axis=1)
    g, u = jnp.split(_dwconv(xp, w_dw, b_dw), 2, axis=-1)
    return (jax.nn.gelu(g) * u) @ w_down, xp[:, -(FFN_CONV_WIDTH - 1):]


def _trunk(x, caches, p):
    prompt = caches is None
    B = x.shape[0]
    new_k, new_v, new_conv, new_cv, new_ffn = [], [], [], [], []
    for i in range(DEPTH):
        kind, j = i % N_MIXERS, i // N_MIXERS
        g = p['norm_gain'][i]
        h = _rmsnorm(x, g[0])
        if kind == 0:
            args = (p['attn_w_qkv'][j], p['attn_b_qkv'][j], p['attn_w_o'][j], p['attn_b_o'][j],
                    p['attn_sinks'][j], p['rel_bias_table'])
            if prompt:
                y, k_rows, v_rows = _attn_prompt(h, *args)
            else:
                y, k_rows, v_rows = _attn_sample(h, caches[0][j], caches[1][j], *args)
            new_k.append(k_rows)
            new_v.append(v_rows)
        elif kind == 1:
            st = jnp.zeros((B, CONV_WIDTH - 1, D_MODEL), x.dtype) if prompt else caches[2][j]
            y, s = _conv_module(h, st, p['conv_w_pw1'][j], p['conv_b_pw1'][j], p['conv_w_dw'][j],
                                p['conv_b_dw'][j], p['conv_ln_g'][j], p['conv_ln_b'][j],
                                p['conv_w_pw2'][j], p['conv_b_pw2'][j])
            new_conv.append(s)
        else:
            y, v_rows = _chunk_mlp(h, p['cmlp_w_in'][j], p['cmlp_b_in'][j], p['cmlp_ln_g'][j],
                                   p['cmlp_ln_b'][j], p['cmlp_w_s'][j], p['cmlp_b_s'][j],
                                   p['cmlp_w_out'][j], p['cmlp_b_out'][j])
            if not prompt:
                new_cv.append(v_rows)
        x = x + _rmsnorm(y, g[1])
        h = _rmsnorm(x, g[2])
        st = jnp.zeros((B, FFN_CONV_WIDTH - 1, 2 * D_FF), x.dtype) if prompt else caches[3][i]
        y, s = _conv_ffn(h, st, p['ffn_w_up'][i], p['ffn_w_dw'][i], p['ffn_b_dw'][i], p['ffn_w_down'][i])
        new_ffn.append(s)
        x = x + _rmsnorm(y, g[3])
    cv = jnp.stack(new_cv) if new_cv else None
    return x, jnp.stack(new_k), jnp.stack(new_v), jnp.stack(new_conv), cv, jnp.stack(new_ffn)


def setup_inputs(seed: int = 0) -> dict:
    key = jax.random.key(seed)
    ks = iter(jax.random.split(key, 40))

    def nrm(shape, scale):
        return jax.random.normal(next(ks), shape, jnp.float32) * scale

    win = min(WINDOW, PAST_LEN)
    qkv_dim = (N_HEADS + 2 * N_KV_HEADS) * HEAD_DIM
    na, nc, nm = N_ATTN_LAYERS, N_CONV_LAYERS, N_CMLP_LAYERS
    return {
        'x_prompt': nrm((BATCH, SEQ, D_MODEL), 1.0),
        'x_sample': nrm((DEC_BATCH, DEC_SEQ, D_MODEL), 1.0),
        'cache_attn_k': nrm((na, DEC_BATCH, win, N_KV_HEADS, HEAD_DIM), 1.0),
        'cache_attn_v': nrm((na, DEC_BATCH, win, N_KV_HEADS, HEAD_DIM), 1.0),
        'state_conv': nrm((nc, DEC_BATCH, CONV_WIDTH - 1, D_MODEL), 0.5),
        'state_ffn_conv': nrm((DEPTH, DEC_BATCH, FFN_CONV_WIDTH - 1, 2 * D_FF), 1.0),
        'rel_bias_table': nrm((NUM_BUCKETS, N_HEADS), 0.5),
        'norm_gain': 1.0 + nrm((DEPTH, 4, D_MODEL), 0.05),
        'attn_w_qkv': nrm((na, D_MODEL, qkv_dim), D_MODEL ** -0.5),
        'attn_b_qkv': nrm((na, qkv_dim), 0.02),
        'attn_w_o': nrm((na, N_HEADS * HEAD_DIM, D_MODEL), (N_HEADS * HEAD_DIM) ** -0.5),
        'attn_b_o': nrm((na, D_MODEL), 0.02),
        'attn_sinks': nrm((na, N_HEADS), 1.0),
        'conv_w_pw1': nrm((nc, D_MODEL, 2 * D_MODEL), D_MODEL ** -0.5),
        'conv_b_pw1': nrm((nc, 2 * D_MODEL), 0.02),
        'conv_w_dw': nrm((nc, CONV_WIDTH, D_MODEL), CONV_WIDTH ** -0.5),
        'conv_b_dw': nrm((nc, D_MODEL), 0.02),
        'conv_ln_g': 1.0 + nrm((nc, D_MODEL), 0.05),
        'conv_ln_b': nrm((nc, D_MODEL), 0.02),
        'conv_w_pw2': nrm((nc, D_MODEL, D_MODEL), D_MODEL ** -0.5),
        'conv_b_pw2': nrm((nc, D_MODEL), 0.02),
        'cmlp_w_in': nrm((nm, D_MODEL, 2 * CMLP_DIM), D_MODEL ** -0.5),
        'cmlp_b_in': nrm((nm, 2 * CMLP_DIM), 0.02),
        'cmlp_ln_g': 1.0 + nrm((nm, CMLP_DIM), 0.05),
        'cmlp_ln_b': nrm((nm, CMLP_DIM), 0.02),
        'cmlp_w_s': nrm((nm, CMLP_GROUPS, MIX_BLOCK, MIX_BLOCK), MIX_BLOCK ** -0.5),
        'cmlp_b_s': 1.0 + nrm((nm, CMLP_GROUPS, MIX_BLOCK), 0.1),
        'cmlp_w_out': nrm((nm, CMLP_DIM, D_MODEL), CMLP_DIM ** -0.5),
        'cmlp_b_out': nrm((nm, D_MODEL), 0.02),
        'ffn_w_up': nrm((DEPTH, D_MODEL, 2 * D_FF), D_MODEL ** -0.5),
        'ffn_w_dw': nrm((DEPTH, FFN_CONV_WIDTH, 2 * D_FF), FFN_CONV_WIDTH ** -0.5),
        'ffn_b_dw': nrm((DEPTH, 2 * D_FF), 0.02),
        'ffn_w_down': nrm((DEPTH, D_FF, D_MODEL), D_FF ** -0.5),
    }


def reference(x_prompt, x_sample, cache_attn_k, cache_attn_v, state_conv, state_ffn_conv,
              rel_bias_table, norm_gain, attn_w_qkv, attn_b_qkv, attn_w_o, attn_b_o, attn_sinks,
              conv_w_pw1, conv_b_pw1, conv_w_dw, conv_b_dw, conv_ln_g, conv_ln_b, conv_w_pw2,
              conv_b_pw2, cmlp_w_in, cmlp_b_in, cmlp_ln_g, cmlp_ln_b, cmlp_w_s, cmlp_b_s,
              cmlp_w_out, cmlp_b_out, ffn_w_up, ffn_w_dw, ffn_b_dw, ffn_w_down):
    p = {
        'rel_bias_table': rel_bias_table, 'norm_gain': norm_gain,
        'attn_w_qkv': attn_w_qkv, 'attn_b_qkv': attn_b_qkv, 'attn_w_o': attn_w_o,
        'attn_b_o': attn_b_o, 'attn_sinks': attn_sinks,
        'conv_w_pw1': conv_w_pw1, 'conv_b_pw1': conv_b_pw1, 'conv_w_dw': conv_w_dw,
        'conv_b_dw': conv_b_dw, 'conv_ln_g': conv_ln_g, 'conv_ln_b': conv_ln_b,
        'conv_w_pw2': conv_w_pw2, 'conv_b_pw2': conv_b_pw2,
        'cmlp_w_in': cmlp_w_in, 'cmlp_b_in': cmlp_b_in, 'cmlp_ln_g': cmlp_ln_g,
        'cmlp_ln_b': cmlp_ln_b, 'cmlp_w_s': cmlp_w_s, 'cmlp_b_s': cmlp_b_s,
        'cmlp_w_out': cmlp_w_out, 'cmlp_b_out': cmlp_b_out,
        'ffn_w_up': ffn_w_up, 'ffn_w_dw': ffn_w_dw, 'ffn_b_dw': ffn_b_dw, 'ffn_w_down': ffn_w_down,
    }
    y_prompt, p_attn_k, p_attn_v, p_conv, _, p_ffn_conv = _trunk(x_prompt, None, p)
    y_sample, s_attn_k, s_attn_v, s_conv, s_cmlp_v, s_ffn_conv = _trunk(
        x_sample, (cache_attn_k, cache_attn_v, state_conv, state_ffn_conv), p)
    return (y_prompt, y_sample, p_attn_k, p_attn_v, p_conv, p_ffn_conv,
            s_attn_k, s_attn_v, s_conv, s_cmlp_v, s_ffn_conv)
```

```python
import functools
import math

import jax
import jax.numpy as jnp
from jax import lax
from jax.experimental import pallas as pl
from jax.experimental.pallas import tpu as pltpu

D_MODEL = 1024
DEPTH = 4
CHUNK = 64
N_MIXERS = 3
N_HEADS = 16
N_KV_HEADS = 4
HEAD_DIM = 64
GQA_GROUP = N_HEADS // N_KV_HEADS
WINDOW = 128
KV_DIM = N_KV_HEADS * HEAD_DIM
Q_DIM = N_HEADS * HEAD_DIM
NUM_BUCKETS = 32
MAX_DISTANCE = 128
CONV_WIDTH = 31
CONV_HIST = CONV_WIDTH - 1
MIX_BLOCK = 128
CMLP_GROUPS = 4
CMLP_DIM = 2 * D_MODEL
CMLP_GROUP_DIM = CMLP_DIM // CMLP_GROUPS
D_FF = 2816
FFN_CONV_WIDTH = 3
FFN_HIST = FFN_CONV_WIDTH - 1
RMS_EPS = 1e-6
LN_EPS = 1e-5

F32 = jnp.float32
BF16 = jnp.bfloat16

V7X_VMEM_BYTES = 64 * 1024 * 1024
VMEM_LIMIT_BYTES = V7X_VMEM_BYTES - 8 * 1024 * 1024
SUBLANES = 8
FFN_CHUNK = 256
FFN_PAD = SUBLANES
CONV_PAD = 32
CONV_ROW_TILE = 32


def _rms(x, g):
    ms = jnp.mean(x * x, axis=-1, keepdims=True)
    return x * lax.rsqrt(ms + RMS_EPS) * g


def _layernorm(x, g, b):
    mu = jnp.mean(x, axis=-1, keepdims=True)
    xc = x - mu
    var = jnp.mean(xc * xc, axis=-1, keepdims=True)
    return xc * lax.rsqrt(var + LN_EPS) * g + b


def _dot(a, b):
    return jnp.dot(a, b, preferred_element_type=F32)


def _const_spec(shape):
    zeros = (0,) * len(shape)
    return pl.BlockSpec(shape, lambda b, t: zeros, pipeline_mode=pl.Buffered(1))


def _seq_spec(bb, rows, width):
    return pl.BlockSpec((bb, rows, width), lambda b, t: (b, t, 0))


def _state_spec(bb, rows, width):
    return pl.BlockSpec((bb, rows, width), lambda b, t: (b, 0, 0))


def _params():
    return pltpu.CompilerParams(
        dimension_semantics=("arbitrary", "arbitrary"),
        vmem_limit_bytes=VMEM_LIMIT_BYTES,
    )


def _bias_body(bucket_ref, table_ref, o_ref):
    bk = bucket_ref[...]
    for h in range(N_HEADS):
        acc = jnp.zeros(bk.shape, F32)
        for b in range(NUM_BUCKETS):
            acc = jnp.where(bk == b, table_ref[b, h], acc)
        j, i = divmod(h, GQA_GROUP)
        o_ref[j, i * CHUNK:(i + 1) * CHUNK, :] = acc


def _t5_bucket(rel):
    half = NUM_BUCKETS // 2
    max_exact = half // 2
    n = jnp.abs(rel)
    log_ratio = jnp.log(jnp.maximum(n, 1).astype(F32) / max_exact) / math.log(MAX_DISTANCE / max_exact)
    large = jnp.minimum(max_exact + (log_ratio * (half - max_exact)).astype(jnp.int32), half - 1)
    return jnp.where(rel > 0, half, 0) + jnp.where(n < max_exact, n, large)


def _rel_bias(table):
    q_pos = jnp.arange(CHUNK, dtype=jnp.int32)
    k_pos = jnp.arange(WINDOW + CHUNK, dtype=jnp.int32) - WINDOW
    bucket = _t5_bucket(k_pos[None, :] - q_pos[:, None]).astype(jnp.int32)
    return pl.pallas_call(
        _bias_body,
        out_shape=jax.ShapeDtypeStruct((N_KV_HEADS, GQA_GROUP * CHUNK, WINDOW + CHUNK), F32),
        in_specs=[pl.BlockSpec(memory_space=pltpu.VMEM), pl.BlockSpec(memory_space=pltpu.SMEM)],
        out_specs=pl.BlockSpec(memory_space=pltpu.VMEM),
        name="rel_bias",
    )(bucket, table)


def _attn_body(x_ref, kc_ref, vc_ref, gin_ref, gout_ref, wqkv_ref, bqkv_ref, wo_ref, bo_ref,
               sinks_ref, bias_ref, o_ref, kout_ref, vout_ref, qbuf, kbuf, vbuf, obuf,
               *, bb, tm, nt, mask_start):
    t = pl.program_id(1)
    rows = bb * tm
    kv_rows = min(WINDOW, tm)
    span = WINDOW + CHUNK

    @pl.when(t == 0)
    def _():
        kbuf[:, 0:WINDOW, :] = kc_ref[...].astype(BF16)
        vbuf[:, 0:WINDOW, :] = vc_ref[...].astype(BF16)

    x = x_ref[...].reshape(rows, D_MODEL)
    h = _rms(x, gin_ref[...]).astype(BF16)
    qkv = _dot(h, wqkv_ref[...]) + bqkv_ref[...]
    qbuf[...] = (qkv[:, :Q_DIM] * (HEAD_DIM ** -0.5)).astype(BF16)
    k = qkv[:, Q_DIM:Q_DIM + KV_DIM].reshape(bb, tm, KV_DIM)
    v = qkv[:, Q_DIM + KV_DIM:].reshape(bb, tm, KV_DIM)
    kbuf[:, WINDOW:WINDOW + tm, :] = k.astype(BF16)
    vbuf[:, WINDOW:WINDOW + tm, :] = v.astype(BF16)

    @pl.when(t == nt - 1)
    def _():
        kout_ref[...] = k[:, tm - kv_rows:, :]
        vout_ref[...] = v[:, tm - kv_rows:, :]

    row_head = lax.broadcasted_iota(jnp.int32, (GQA_GROUP * CHUNK, 1), 0) // CHUNK
    col = lax.broadcasted_iota(jnp.int32, (GQA_GROUP * CHUNK, span), 1)
    first_valid = jnp.where(t == 0, WINDOW, 0)

    def chunk(s, c):
        r0 = c * CHUNK
        if not isinstance(r0, int):
            r0 = pl.multiple_of(r0, CHUNK)
        qs = qbuf[pl.ds(s * tm + r0, CHUNK), :]
        for j in range(N_KV_HEADS):
            lanes = slice(j * HEAD_DIM, (j + 1) * HEAD_DIM)
            kj = kbuf[s, pl.ds(r0, span), lanes]
            vj = vbuf[s, pl.ds(r0, span), lanes]
            qst = jnp.concatenate(
                [qs[:, (GQA_GROUP * j + i) * HEAD_DIM:(GQA_GROUP * j + i + 1) * HEAD_DIM]
                 for i in range(GQA_GROUP)], axis=0)
            sc = lax.dot_general(qst, kj, (((1,), (1,)), ((), ())), preferred_element_type=F32)
            sc = sc + bias_ref[j]
            if mask_start:
                sc = jnp.where(col + r0 >= first_valid, sc, -jnp.inf)
            sink = jnp.zeros((GQA_GROUP * CHUNK, 1), F32)
            for i in range(GQA_GROUP):
                sink = jnp.where(row_head == i, sinks_ref[GQA_GROUP * j + i], sink)
            m = jnp.maximum(jnp.max(sc, axis=-1, keepdims=True), sink)
            p = jnp.exp(sc - m)
            denom = jnp.sum(p, axis=-1, keepdims=True) + jnp.exp(sink - m)
            o = _dot(p.astype(BF16), vj) / denom
            oj = jnp.concatenate([o[i * CHUNK:(i + 1) * CHUNK] for i in range(GQA_GROUP)], axis=1)
            obuf[pl.ds(s * tm + r0, CHUNK), j * GQA_GROUP * HEAD_DIM:(j + 1) * GQA_GROUP * HEAD_DIM] = (
                oj.astype(BF16))

    n_chunks = tm // CHUNK
    for s in range(bb):
        if n_chunks == 1:
            chunk(s, 0)
        else:
            def loop_body(c, carry, s=s):
                chunk(s, c)
                return carry
            lax.fori_loop(0, n_chunks, loop_body, 0)

    y = _dot(obuf[...], wo_ref[...]) + bo_ref[...]
    o_ref[...] = (x + _rms(y, gout_ref[...])).reshape(bb, tm, D_MODEL)

    if nt > 1:
        kbuf[:, 0:WINDOW, :] = kbuf[:, tm:tm + WINDOW, :]
        vbuf[:, 0:WINDOW, :] = vbuf[:, tm:tm + WINDOW, :]


def _attn_layer(x, kc, vc, gin, gout, wqkv, bqkv, wo, bo, sinks, bias, *, bb, tm, mask_start):
    B, T, _ = x.shape
    nt = T // tm
    kv_rows = min(WINDOW, tm)
    body = functools.partial(_attn_body, bb=bb, tm=tm, nt=nt, mask_start=mask_start)
    return pl.pallas_call(
        body,
        grid=(B // bb, nt),
        out_shape=(jax.ShapeDtypeStruct(x.shape, F32),
                   jax.ShapeDtypeStruct((B, kv_rows, KV_DIM), F32),
                   jax.ShapeDtypeStruct((B, kv_rows, KV_DIM), F32)),
        in_specs=[
            _seq_spec(bb, tm, D_MODEL),
            _state_spec(bb, WINDOW, KV_DIM),
            _state_spec(bb, WINDOW, KV_DIM),
            _const_spec((1, D_MODEL)),
            _const_spec((1, D_MODEL)),
            _const_spec(wqkv.shape),
            _const_spec(bqkv.shape),
            _const_spec(wo.shape),
            _const_spec(bo.shape),
            pl.BlockSpec(memory_space=pltpu.SMEM),
            _const_spec(bias.shape),
        ],
        out_specs=(_seq_spec(bb, tm, D_MODEL),
                   _state_spec(bb, kv_rows, KV_DIM),
                   _state_spec(bb, kv_rows, KV_DIM)),
        scratch_shapes=[
            pltpu.VMEM((bb * tm, Q_DIM), BF16),
            pltpu.VMEM((bb, WINDOW + tm, KV_DIM), BF16),
            pltpu.VMEM((bb, WINDOW + tm, KV_DIM), BF16),
            pltpu.VMEM((bb * tm, Q_DIM), BF16),
        ],
        compiler_params=_params(),
        name="attn_mixer",
    )(x, kc, vc, gin, gout, wqkv, bqkv, wo, bo, sinks, bias)


def _conv_body(x_ref, st_ref, gin_ref, gout_ref, w1_ref, b1_ref, wdw_ref, bdw_ref, lng_ref, lnb_ref,
               w2_ref, b2_ref, o_ref, nst_ref, gbuf, cbuf, *, bb, tm, nt):
    t = pl.program_id(1)
    rows = bb * tm
    hist0 = CONV_PAD - CONV_HIST

    @pl.when(t == 0)
    def _():
        gbuf[:, hist0:CONV_PAD, :] = st_ref[...]

    x = x_ref[...].reshape(rows, D_MODEL)
    h = _rms(x, gin_ref[...]).astype(BF16)
    ag = _dot(h, w1_ref[...]) + b1_ref[...]
    glu = ag[:, :D_MODEL] * jax.nn.sigmoid(ag[:, D_MODEL:])
    gbuf[:, CONV_PAD:CONV_PAD + tm, :] = glu.reshape(bb, tm, D_MODEL)

    for s in range(bb):
        for r0 in range(0, tm, CONV_ROW_TILE):
            acc = jnp.zeros((CONV_ROW_TILE, D_MODEL), F32)
            for kk in range(CONV_WIDTH):
                lo = r0 + hist0 + kk
                acc = acc + gbuf[s, lo:lo + CONV_ROW_TILE, :] * wdw_ref[kk:kk + 1, :]
            y = _layernorm(acc + bdw_ref[...], lng_ref[...], lnb_ref[...])
            cbuf[s * tm + r0:s * tm + r0 + CONV_ROW_TILE, :] = (y * jax.nn.sigmoid(y)).astype(BF16)

    y = _dot(cbuf[...], w2_ref[...]) + b2_ref[...]
    o_ref[...] = (x + _rms(y, gout_ref[...])).reshape(bb, tm, D_MODEL)

    last = gbuf[:, CONV_PAD + tm - CONV_HIST:CONV_PAD + tm, :]

    @pl.when(t == nt - 1)
    def _():
        nst_ref[...] = last

    if nt > 1:
        gbuf[:, hist0:CONV_PAD, :] = last


def _conv_layer(x, st, gin, gout, w1, b1, wdw, bdw, lng, lnb, w2, b2, *, bb, tm):
    B, T, _ = x.shape
    nt = T // tm
    assert tm >= CONV_HIST and tm % CONV_ROW_TILE == 0
    body = functools.partial(_conv_body, bb=bb, tm=tm, nt=nt)
    return pl.pallas_call(
        body,
        grid=(B // bb, nt),
        out_shape=(jax.ShapeDtypeStruct(x.shape, F32),
                   jax.ShapeDtypeStruct((B, CONV_HIST, D_MODEL), F32)),
        in_specs=[
            _seq_spec(bb, tm, D_MODEL),
            _state_spec(bb, CONV_HIST, D_MODEL),
            _const_spec((1, D_MODEL)),
            _const_spec((1, D_MODEL)),
            _const_spec(w1.shape),
            _const_spec(b1.shape),
            _const_spec(wdw.shape),
            _const_spec(bdw.shape),
            _const_spec(lng.shape),
            _const_spec(lnb.shape),
            _const_spec(w2.shape),
            _const_spec(b2.shape),
        ],
        out_specs=(_seq_spec(bb, tm, D_MODEL), _state_spec(bb, CONV_HIST, D_MODEL)),
        scratch_shapes=[
            pltpu.VMEM((bb, CONV_PAD + tm, D_MODEL), F32),
            pltpu.VMEM((bb * tm, D_MODEL), BF16),
        ],
        compiler_params=_params(),
        name="conv_mixer",
    )(x, st, gin, gout, w1, b1, wdw, bdw, lng, lnb, w2, b2)


def _cmlp_body(x_ref, gin_ref, gout_ref, win_ref, bin_ref, lng_ref, lnb_ref, ws_ref, bs_ref,
               wout_ref, bout_ref, *rest, bb, tm, blk, emit_v):
    if emit_v:
        o_ref, v_ref, vbuf, ybuf = rest
    else:
        o_ref, vbuf, ybuf = rest
    rows = bb * tm
    x = x_ref[...].reshape(rows, D_MODEL)
    h = _rms(x, gin_ref[...]).astype(BF16)
    hv = jax.nn.gelu(_dot(h, win_ref[:, CMLP_DIM:]) + bin_ref[:, CMLP_DIM:])
    vln = _layernorm(hv, lng_ref[...], lnb_ref[...])
    if emit_v:
        v_ref[...] = vln.reshape(bb, tm, CMLP_DIM)
    vbuf[...] = vln.astype(BF16)

    pos_r = lax.broadcasted_iota(jnp.int32, (blk, blk), 0) // CHUNK
    pos_c = lax.broadcasted_iota(jnp.int32, (blk, blk), 1) // CHUNK
    for g in range(CMLP_GROUPS):
        lanes = slice(g * CMLP_GROUP_DIM, (g + 1) * CMLP_GROUP_DIM)
        ug = jax.nn.gelu(_dot(h, win_ref[:, lanes]) + bin_ref[:, lanes])
        wsm = jnp.where(pos_c <= pos_r, ws_ref[g], 0.0).astype(BF16)
        for n in range(rows // blk):
            rs = slice(n * blk, (n + 1) * blk)
            gate = _dot(wsm, vbuf[rs, lanes]) + bs_ref[g]
            ybuf[rs, lanes] = (ug[rs] * gate).astype(BF16)

    y = _dot(ybuf[...], wout_ref[...]) + bout_ref[...]
    o_ref[...] = (x + _rms(y, gout_ref[...])).reshape(bb, tm, D_MODEL)


def _cmlp_layer(x, gin, gout, win, bin_, lng, lnb, ws, bs, wout, bout, *, bb, tm, emit_v):
    B, T, _ = x.shape
    blk = min(T, MIX_BLOCK)
    assert tm % blk == 0
    ws = ws[:, :blk, :blk]
    bs = bs[:, :blk, None]
    body = functools.partial(_cmlp_body, bb=bb, tm=tm, blk=blk, emit_v=emit_v)
    out_shape = [jax.ShapeDtypeStruct(x.shape, F32)]
    out_specs = [_seq_spec(bb, tm, D_MODEL)]
    if emit_v:
        out_shape.append(jax.ShapeDtypeStruct((B, T, CMLP_DIM), F32))
        out_specs.append(_seq_spec(bb, tm, CMLP_DIM))
    return pl.pallas_call(
        body,
        grid=(B // bb, T // tm),
        out_shape=tuple(out_shape),
        in_specs=[
            _seq_spec(bb, tm, D_MODEL),
            _const_spec((1, D_MODEL)),
            _const_spec((1, D_MODEL)),
            _const_spec(win.shape),
            _const_spec(bin_.shape),
            _const_spec(lng.shape),
            _const_spec(lnb.shape),
            _const_spec(ws.shape),
            _const_spec(bs.shape),
            _const_spec(wout.shape),
            _const_spec(bout.shape),
        ],
        out_specs=tuple(out_specs),
        scratch_shapes=[
            pltpu.VMEM((bb * tm, CMLP_DIM), BF16),
            pltpu.VMEM((bb * tm, CMLP_DIM), BF16),
        ],
        compiler_params=_params(),
        name="cmlp_mixer",
    )(x, gin, gout, win, bin_, lng, lnb, ws, bs, wout, bout)


def _ffn_body(x_ref, st_ref, gin_ref, gout_ref, wup_ref, wdw_ref, bdw_ref, wdn_ref,
              o_ref, nst_ref, upbuf, actbuf, *, bb, tm, nt):
    t = pl.program_id(1)
    rows = bb * tm
    hist0 = FFN_PAD - FFN_HIST

    @pl.when(t == 0)
    def _():
        upbuf[:, hist0:FFN_PAD, :] = st_ref[...]

    x = x_ref[...].reshape(rows, D_MODEL)
    h = _rms(x, gin_ref[...]).astype(BF16)

    def conv(lo):
        lanes = slice(lo, lo + FFN_CHUNK)
        acc = bdw_ref[:, lanes]
        for kk in range(FFN_CONV_WIDTH):
            acc = acc + upbuf[:, hist0 + kk:hist0 + kk + tm, lanes] * wdw_ref[kk:kk + 1, lanes]
        return acc

    for c in range(D_FF // FFN_CHUNK):
        for lo in (c * FFN_CHUNK, D_FF + c * FFN_CHUNK):
            up = _dot(h, wup_ref[:, lo:lo + FFN_CHUNK])
            upbuf[:, FFN_PAD:FFN_PAD + tm, lo:lo + FFN_CHUNK] = up.reshape(bb, tm, FFN_CHUNK)
        act = jax.nn.gelu(conv(c * FFN_CHUNK)) * conv(D_FF + c * FFN_CHUNK)
        actbuf[:, c * FFN_CHUNK:(c + 1) * FFN_CHUNK] = act.reshape(rows, FFN_CHUNK).astype(BF16)

    y = _dot(actbuf[...], wdn_ref[...])
    o_ref[...] = (x + _rms(y, gout_ref[...])).reshape(bb, tm, D_MODEL)

    last = upbuf[:, FFN_PAD + tm - FFN_HIST:FFN_PAD + tm, :]

    @pl.when(t == nt - 1)
    def _():
        nst_ref[...] = last

    if nt > 1:
        upbuf[:, hist0:FFN_PAD, :] = last


def _ffn_layer(x, st, gin, gout, wup, wdw, bdw, wdn, *, bb, tm):
    B, T, _ = x.shape
    nt = T // tm
    assert tm >= FFN_HIST and D_FF % FFN_CHUNK == 0
    body = functools.partial(_ffn_body, bb=bb, tm=tm, nt=nt)
    return pl.pallas_call(
        body,
        grid=(B // bb, nt),
        out_shape=(jax.ShapeDtypeStruct(x.shape, F32),
                   jax.ShapeDtypeStruct((B, FFN_HIST, 2 * D_FF), F32)),
        in_specs=[
            _seq_spec(bb, tm, D_MODEL),
            _state_spec(bb, FFN_HIST, 2 * D_FF),
            _const_spec((1, D_MODEL)),
            _const_spec((1, D_MODEL)),
            _const_spec(wup.shape),
            _const_spec(wdw.shape),
            _const_spec(bdw.shape),
            _const_spec(wdn.shape),
        ],
        out_specs=(_seq_spec(bb, tm, D_MODEL), _state_spec(bb, FFN_HIST, 2 * D_FF)),
        scratch_shapes=[
            pltpu.VMEM((bb, FFN_PAD + tm, 2 * D_FF), F32),
            pltpu.VMEM((bb * tm, D_FF), BF16),
        ],
        compiler_params=_params(),
        name="conv_ffn",
    )(x, st, gin, gout, wup, wdw, bdw, wdn)


def _trunk(x, caches, p, bias, *, bb, tm):
    prompt = caches is None
    B = x.shape[0]
    new_k, new_v, new_conv, new_cv, new_ffn = [], [], [], [], []
    for i in range(DEPTH):
        kind, j = i % N_MIXERS, i // N_MIXERS
        g = p['norm_gain'][i]
        gains = [g[n][None, :] for n in range(4)]
        if kind == 0:
            if prompt:
                kc = jnp.zeros((B, WINDOW, KV_DIM), F32)
                vc = kc
            else:
                kc = caches[0][j].reshape(B, WINDOW, KV_DIM)
                vc = caches[1][j].reshape(B, WINDOW, KV_DIM)
            x, k_rows, v_rows = _attn_layer(
                x, kc, vc, gains[0], gains[1], p['attn_w_qkv'][j], p['attn_b_qkv'][j][None, :],
                p['attn_w_o'][j], p['attn_b_o'][j][None, :], p['attn_sinks'][j], bias,
                bb=bb, tm=tm, mask_start=prompt)
            new_k.append(k_rows.reshape(B, -1, N_KV_HEADS, HEAD_DIM))
            new_v.append(v_rows.reshape(B, -1, N_KV_HEADS, HEAD_DIM))
        elif kind == 1:
            st = jnp.zeros((B, CONV_HIST, D_MODEL), F32) if prompt else caches[2][j]
            x, s = _conv_layer(
                x, st, gains[0], gains[1], p['conv_w_pw1'][j], p['conv_b_pw1'][j][None, :],
                p['conv_w_dw'][j], p['conv_b_dw'][j][None, :], p['conv_ln_g'][j][None, :],
                p['conv_ln_b'][j][None, :], p['conv_w_pw2'][j], p['conv_b_pw2'][j][None, :],
                bb=bb, tm=tm)
            new_conv.append(s)
        else:
            res = _cmlp_layer(
                x, gains[0], gains[1], p['cmlp_w_in'][j], p['cmlp_b_in'][j][None, :],
                p['cmlp_ln_g'][j][None, :], p['cmlp_ln_b'][j][None, :], p['cmlp_w_s'][j],
                p['cmlp_b_s'][j], p['cmlp_w_out'][j], p['cmlp_b_out'][j][None, :],
                bb=bb, tm=tm, emit_v=not prompt)
            x = res[0]
            if not prompt:
                new_cv.append(res[1])
        st = jnp.zeros((B, FFN_HIST, 2 * D_FF), F32) if prompt else caches[3][i]
        x, s = _ffn_layer(x, st, gains[2], gains[3], p['ffn_w_up'][i], p['ffn_w_dw'][i],
                          p['ffn_b_dw'][i][None, :], p['ffn_w_down'][i], bb=bb, tm=tm)
        new_ffn.append(s)
    cv = jnp.stack(new_cv) if new_cv else None
    return x, jnp.stack(new_k), jnp.stack(new_v), jnp.stack(new_conv), cv, jnp.stack(new_ffn)


def kernel(x_prompt, x_sample, cache_attn_k, cache_attn_v, state_conv, state_ffn_conv, rel_bias_table, norm_gain, attn_w_qkv, attn_b_qkv, attn_w_o, attn_b_o, attn_sinks, conv_w_pw1, conv_b_pw1, conv_w_dw, conv_b_dw, conv_ln_g, conv_ln_b, conv_w_pw2, conv_b_pw2, cmlp_w_in, cmlp_b_in, cmlp_ln_g, cmlp_ln_b, cmlp_w_s, cmlp_b_s, cmlp_w_out, cmlp_b_out, ffn_w_up, ffn_w_dw, ffn_b_dw, ffn_w_down):
    p = {
        'norm_gain': norm_gain,
        'attn_w_qkv': attn_w_qkv.astype(BF16), 'attn_b_qkv': attn_b_qkv,
        'attn_w_o': attn_w_o.astype(BF16), 'attn_b_o': attn_b_o, 'attn_sinks': attn_sinks,
        'conv_w_pw1': conv_w_pw1.astype(BF16), 'conv_b_pw1': conv_b_pw1, 'conv_w_dw': conv_w_dw,
        'conv_b_dw': conv_b_dw, 'conv_ln_g': conv_ln_g, 'conv_ln_b': conv_ln_b,
        'conv_w_pw2': conv_w_pw2.astype(BF16), 'conv_b_pw2': conv_b_pw2,
        'cmlp_w_in': cmlp_w_in.astype(BF16), 'cmlp_b_in': cmlp_b_in, 'cmlp_ln_g': cmlp_ln_g,
        'cmlp_ln_b': cmlp_ln_b, 'cmlp_w_s': cmlp_w_s, 'cmlp_b_s': cmlp_b_s,
        'cmlp_w_out': cmlp_w_out.astype(BF16), 'cmlp_b_out': cmlp_b_out,
        'ffn_w_up': ffn_w_up.astype(BF16), 'ffn_w_dw': ffn_w_dw, 'ffn_b_dw': ffn_b_dw,
        'ffn_w_down': ffn_w_down.astype(BF16),
    }
    bias = _rel_bias(rel_bias_table)
    y_prompt, p_attn_k, p_attn_v, p_conv, _, p_ffn_conv = _trunk(
        x_prompt, None, p, bias, bb=1, tm=512)
    y_sample, s_attn_k, s_attn_v, s_conv, s_cmlp_v, s_ffn_conv = _trunk(
        x_sample, (cache_attn_k, cache_attn_v, state_conv, state_ffn_conv), p, bias, bb=8, tm=64)
    return (y_prompt, y_sample, p_attn_k, p_attn_v, p_conv, p_ffn_conv,
            s_attn_k, s_attn_v, s_conv, s_cmlp_v, s_ffn_conv)
```

```python
import functools
import math

import jax
import jax.numpy as jnp
from jax import lax
from jax.experimental import pallas as pl
from jax.experimental.pallas import tpu as pltpu

D_MODEL = 1024
DEPTH = 4
CHUNK = 64
N_MIXERS = 3
N_HEADS = 16
N_KV_HEADS = 4
HEAD_DIM = 64
GQA_GROUP = N_HEADS // N_KV_HEADS
WINDOW = 128
KV_DIM = N_KV_HEADS * HEAD_DIM
Q_DIM = N_HEADS * HEAD_DIM
NUM_BUCKETS = 32
MAX_DISTANCE = 128
CONV_WIDTH = 31
CONV_HIST = CONV_WIDTH - 1
MIX_BLOCK = 128
CMLP_GROUPS = 4
CMLP_DIM = 2 * D_MODEL
CMLP_GROUP_DIM = CMLP_DIM // CMLP_GROUPS
D_FF = 2816
FFN_CONV_WIDTH = 3
FFN_HIST = FFN_CONV_WIDTH - 1
RMS_EPS = 1e-6
LN_EPS = 1e-5

F32 = jnp.float32
BF16 = jnp.bfloat16

V7X_VMEM_BYTES = 64 * 1024 * 1024
VMEM_LIMIT_BYTES = V7X_VMEM_BYTES - 8 * 1024 * 1024
SUBLANES = 8
FFN_CHUNK = 256
FFN_DOWN_GROUPS = (4, 4, 3)
FFN_LOOKAHEAD = 2
FFN_PAD = SUBLANES
CONV_PAD = 32
CONV_ROW_TILE = 64
CONV_LANE_TILE = 128
CONV_ROW_BLOCK = 256


def _rms(x, g):
    ms = jnp.mean(x * x, axis=-1, keepdims=True)
    return x * lax.rsqrt(ms + RMS_EPS) * g


def _layernorm(x, g, b):
    mu = jnp.mean(x, axis=-1, keepdims=True)
    xc = x - mu
    var = jnp.mean(xc * xc, axis=-1, keepdims=True)
    return xc * lax.rsqrt(var + LN_EPS) * g + b


def _dot(a, b):
    return jnp.dot(a, b, preferred_element_type=F32)


def _const_spec(shape):
    zeros = (0,) * len(shape)
    return pl.BlockSpec(shape, lambda b, t: zeros, pipeline_mode=pl.Buffered(1))


def _seq_spec(bb, rows, width):
    return pl.BlockSpec((bb, rows, width), lambda b, t: (b, t, 0))


def _state_spec(bb, rows, width):
    return pl.BlockSpec((bb, rows, width), lambda b, t: (b, 0, 0))


def _params():
    return pltpu.CompilerParams(
        dimension_semantics=("arbitrary", "arbitrary"),
        vmem_limit_bytes=VMEM_LIMIT_BYTES,
    )


def _bias_body(bucket_ref, table_ref, o_ref):
    bk = bucket_ref[...]
    for h in range(N_HEADS):
        acc = jnp.zeros(bk.shape, F32)
        for b in range(NUM_BUCKETS):
            acc = jnp.where(bk == b, table_ref[b, h], acc)
        j, i = divmod(h, GQA_GROUP)
        o_ref[j, i * CHUNK:(i + 1) * CHUNK, :] = acc


def _t5_bucket(rel):
    half = NUM_BUCKETS // 2
    max_exact = half // 2
    n = jnp.abs(rel)
    log_ratio = jnp.log(jnp.maximum(n, 1).astype(F32) / max_exact) / math.log(MAX_DISTANCE / max_exact)
    large = jnp.minimum(max_exact + (log_ratio * (half - max_exact)).astype(jnp.int32), half - 1)
    return jnp.where(rel > 0, half, 0) + jnp.where(n < max_exact, n, large)


def _rel_bias(table):
    q_pos = jnp.arange(CHUNK, dtype=jnp.int32)
    k_pos = jnp.arange(WINDOW + CHUNK, dtype=jnp.int32) - WINDOW
    bucket = _t5_bucket(k_pos[None, :] - q_pos[:, None]).astype(jnp.int32)
    return pl.pallas_call(
        _bias_body,
        out_shape=jax.ShapeDtypeStruct((N_KV_HEADS, GQA_GROUP * CHUNK, WINDOW + CHUNK), F32),
        in_specs=[pl.BlockSpec(memory_space=pltpu.VMEM), pl.BlockSpec(memory_space=pltpu.SMEM)],
        out_specs=pl.BlockSpec(memory_space=pltpu.VMEM),
        name="rel_bias",
    )(bucket, table)


def _attn_body(x_ref, kc_ref, vc_ref, gin_ref, gout_ref, wqkv_ref, bqkv_ref, wo_ref, bo_ref,
               sinks_ref, bias_ref, o_ref, kout_ref, vout_ref, qbuf, kbuf, vbuf, obuf,
               *, bb, tm, nt, mask_start):
    t = pl.program_id(1)
    rows = bb * tm
    kv_rows = min(WINDOW, tm)
    span = WINDOW + CHUNK

    @pl.when(t == 0)
    def _():
        kbuf[:, 0:WINDOW, :] = kc_ref[...].astype(BF16)
        vbuf[:, 0:WINDOW, :] = vc_ref[...].astype(BF16)

    x = x_ref[...].reshape(rows, D_MODEL)
    h = _rms(x, gin_ref[...]).astype(BF16)
    qkv = _dot(h, wqkv_ref[...]) + bqkv_ref[...]
    qbuf[...] = (qkv[:, :Q_DIM] * (HEAD_DIM ** -0.5)).astype(BF16)
    k = qkv[:, Q_DIM:Q_DIM + KV_DIM].reshape(bb, tm, KV_DIM)
    v = qkv[:, Q_DIM + KV_DIM:].reshape(bb, tm, KV_DIM)
    kbuf[:, WINDOW:WINDOW + tm, :] = k.astype(BF16)
    vbuf[:, WINDOW:WINDOW + tm, :] = v.astype(BF16)

    @pl.when(t == nt - 1)
    def _():
        kout_ref[...] = k[:, tm - kv_rows:, :]
        vout_ref[...] = v[:, tm - kv_rows:, :]

    row_head = lax.broadcasted_iota(jnp.int32, (GQA_GROUP * CHUNK, 1), 0) // CHUNK
    col = lax.broadcasted_iota(jnp.int32, (GQA_GROUP * CHUNK, span), 1)
    first_valid = jnp.where(t == 0, WINDOW, 0)

    def chunk(s, c):
        r0 = c * CHUNK
        if not isinstance(r0, int):
            r0 = pl.multiple_of(r0, CHUNK)
        qs = qbuf[pl.ds(s * tm + r0, CHUNK), :]
        for j in range(N_KV_HEADS):
            lanes = slice(j * HEAD_DIM, (j + 1) * HEAD_DIM)
            kj = kbuf[s, pl.ds(r0, span), lanes]
            vj = vbuf[s, pl.ds(r0, span), lanes]
            qst = jnp.concatenate(
                [qs[:, (GQA_GROUP * j + i) * HEAD_DIM:(GQA_GROUP * j + i + 1) * HEAD_DIM]
                 for i in range(GQA_GROUP)], axis=0)
            sc = lax.dot_general(qst, kj, (((1,), (1,)), ((), ())), preferred_element_type=F32)
            sc = sc + bias_ref[j]
            if mask_start:
                sc = jnp.where(col + r0 >= first_valid, sc, -jnp.inf)
            sink = jnp.zeros((GQA_GROUP * CHUNK, 1), F32)
            for i in range(GQA_GROUP):
                sink = jnp.where(row_head == i, sinks_ref[GQA_GROUP * j + i], sink)
            m = jnp.maximum(jnp.max(sc, axis=-1, keepdims=True), sink)
            p = jnp.exp(sc - m)
            denom = jnp.sum(p, axis=-1, keepdims=True) + jnp.exp(sink - m)
            o = _dot(p.astype(BF16), vj) / denom
            oj = jnp.concatenate([o[i * CHUNK:(i + 1) * CHUNK] for i in range(GQA_GROUP)], axis=1)
            obuf[pl.ds(s * tm + r0, CHUNK), j * GQA_GROUP * HEAD_DIM:(j + 1) * GQA_GROUP * HEAD_DIM] = (
                oj.astype(BF16))

    n_chunks = tm // CHUNK
    for s in range(bb):
        if n_chunks == 1:
            chunk(s, 0)
        else:
            def loop_body(c, carry, s=s):
                chunk(s, c)
                return carry
            lax.fori_loop(0, n_chunks, loop_body, 0)

    y = _dot(obuf[...], wo_ref[...]) + bo_ref[...]
    o_ref[...] = (x + _rms(y, gout_ref[...])).reshape(bb, tm, D_MODEL)

    if nt > 1:
        kbuf[:, 0:WINDOW, :] = kbuf[:, tm:tm + WINDOW, :]
        vbuf[:, 0:WINDOW, :] = vbuf[:, tm:tm + WINDOW, :]


def _attn_layer(x, kc, vc, gin, gout, wqkv, bqkv, wo, bo, sinks, bias, *, bb, tm, mask_start):
    B, T, _ = x.shape
    nt = T // tm
    kv_rows = min(WINDOW, tm)
    body = functools.partial(_attn_body, bb=bb, tm=tm, nt=nt, mask_start=mask_start)
    return pl.pallas_call(
        body,
        grid=(B // bb, nt),
        out_shape=(jax.ShapeDtypeStruct(x.shape, F32),
                   jax.ShapeDtypeStruct((B, kv_rows, KV_DIM), F32),
                   jax.ShapeDtypeStruct((B, kv_rows, KV_DIM), F32)),
        in_specs=[
            _seq_spec(bb, tm, D_MODEL),
            _state_spec(bb, WINDOW, KV_DIM),
            _state_spec(bb, WINDOW, KV_DIM),
            _const_spec((1, D_MODEL)),
            _const_spec((1, D_MODEL)),
            _const_spec(wqkv.shape),
            _const_spec(bqkv.shape),
            _const_spec(wo.shape),
            _const_spec(bo.shape),
            pl.BlockSpec(memory_space=pltpu.SMEM),
            _const_spec(bias.shape),
        ],
        out_specs=(_seq_spec(bb, tm, D_MODEL),
                   _state_spec(bb, kv_rows, KV_DIM),
                   _state_spec(bb, kv_rows, KV_DIM)),
        scratch_shapes=[
            pltpu.VMEM((bb * tm, Q_DIM), BF16),
            pltpu.VMEM((bb, WINDOW + tm, KV_DIM), BF16),
            pltpu.VMEM((bb, WINDOW + tm, KV_DIM), BF16),
            pltpu.VMEM((bb * tm, Q_DIM), BF16),
        ],
        compiler_params=_params(),
        name="attn_mixer",
    )(x, kc, vc, gin, gout, wqkv, bqkv, wo, bo, sinks, bias)


def _conv_body(x_ref, st_ref, gin_ref, gout_ref, w1_ref, b1_ref, wdw_ref, bdw_ref, lng_ref, lnb_ref,
               w2_ref, b2_ref, o_ref, nst_ref, gbuf, ybuf, cbuf, *, bb, tm, nt):
    t = pl.program_id(1)
    rows = bb * tm
    hist0 = CONV_PAD - CONV_HIST

    @pl.when(t == 0)
    def _():
        gbuf[:, hist0:CONV_PAD, :] = st_ref[...]

    row = lax.broadcasted_iota(jnp.int32, (SUBLANES, CONV_LANE_TILE), 0)
    n_tiles = CONV_ROW_TILE // SUBLANES
    max_a = (hist0 + CONV_WIDTH - 1) // SUBLANES
    tiles_per_block = CONV_ROW_BLOCK // CONV_ROW_TILE

    def row_tiles(rb):
        out = []
        for i in range(tiles_per_block):
            flat = rb * CONV_ROW_BLOCK + i * CONV_ROW_TILE
            out.append((flat // tm, flat % tm, flat))
        return out

    def block_rows(ref, rb):
        flat = rb * CONV_ROW_BLOCK
        if tm >= CONV_ROW_BLOCK:
            return ref.at[flat // tm, flat % tm:flat % tm + CONV_ROW_BLOCK, :]
        return ref.at[flat // tm:(flat + CONV_ROW_BLOCK) // tm, :, :]

    def pointwise_in(rb):
        x = block_rows(x_ref, rb)[...].reshape(CONV_ROW_BLOCK, D_MODEL)
        h = _rms(x, gin_ref[...]).astype(BF16)
        ag = _dot(h, w1_ref[...]) + b1_ref[...]
        glu = ag[:, :D_MODEL] * jax.nn.sigmoid(ag[:, D_MODEL:])
        for i, (s, r0, _) in enumerate(row_tiles(rb)):
            gbuf[s, CONV_PAD + r0:CONV_PAD + r0 + CONV_ROW_TILE, :] = (
                glu[i * CONV_ROW_TILE:(i + 1) * CONV_ROW_TILE])

    def depthwise(rb):
        for l0 in range(0, D_MODEL, CONV_LANE_TILE):
            lanes = slice(l0, l0 + CONV_LANE_TILE)
            carried, carried_for = {}, None
            for s, r0, flat in row_tiles(rb):
                if carried_for != (s, r0):
                    carried = {}
                tiles = [gbuf[s, r0 + SUBLANES * j:r0 + SUBLANES * (j + 1), lanes]
                         for j in range(n_tiles + max_a)]
                out = [None] * n_tiles
                for b in range(SUBLANES):
                    taps = [a for a in range(max_a + 1)
                            if 0 <= SUBLANES * a + b - hist0 < CONV_WIDTH]
                    wts = [wdw_ref[SUBLANES * a + b - hist0:SUBLANES * a + b - hist0 + 1, lanes]
                           for a in taps]
                    z = []
                    for m in range(n_tiles + (1 if b else 0)):
                        if m == 0 and b in carried:
                            z.append(carried[b])
                            continue
                        acc = None
                        for a, wt in zip(taps, wts):
                            term = tiles[m + a] * wt
                            acc = term if acc is None else acc + term
                        z.append(acc)
                    if b:
                        carried[b] = z[n_tiles]
                    for m in range(n_tiles):
                        if b == 0:
                            part = z[m]
                        else:
                            part = pltpu.roll(jnp.where(row >= b, z[m], z[m + 1]), SUBLANES - b, 0)
                        out[m] = part if out[m] is None else out[m] + part
                carried_for = (s, r0 + CONV_ROW_TILE)
                for m in range(n_tiles):
                    lo = flat + SUBLANES * m
                    ybuf[lo:lo + SUBLANES, lanes] = out[m]
        for s, r0, flat in row_tiles(rb):
            rs = slice(flat, flat + CONV_ROW_TILE)
            y = _layernorm(ybuf[rs, :] + bdw_ref[...], lng_ref[...], lnb_ref[...])
            cbuf[rs, :] = (y * jax.nn.sigmoid(y)).astype(BF16)

    def pointwise_out(rb):
        flat = rb * CONV_ROW_BLOCK
        y = _dot(cbuf[flat:flat + CONV_ROW_BLOCK, :], w2_ref[...]) + b2_ref[...]
        x = block_rows(x_ref, rb)[...]
        block_rows(o_ref, rb)[...] = x + _rms(y, gout_ref[...]).reshape(x.shape)

    n_blocks = rows // CONV_ROW_BLOCK
    pointwise_in(0)
    for rb in range(1, n_blocks):
        depthwise(rb - 1)
        pointwise_in(rb)
        pointwise_out(rb - 1)
    depthwise(n_blocks - 1)
    pointwise_out(n_blocks - 1)

    last = gbuf[:, CONV_PAD + tm - CONV_HIST:CONV_PAD + tm, :]

    @pl.when(t == nt - 1)
    def _():
        nst_ref[...] = last

    if nt > 1:
        gbuf[:, hist0:CONV_PAD, :] = last


def _conv_layer(x, st, gin, gout, w1, b1, wdw, bdw, lng, lnb, w2, b2, *, bb, tm):
    B, T, _ = x.shape
    nt = T // tm
    assert tm >= CONV_HIST and tm % CONV_ROW_TILE == 0
    body = functools.partial(_conv_body, bb=bb, tm=tm, nt=nt)
    return pl.pallas_call(
        body,
        grid=(B // bb, nt),
        out_shape=(jax.ShapeDtypeStruct(x.shape, F32),
                   jax.ShapeDtypeStruct((B, CONV_HIST, D_MODEL), F32)),
        in_specs=[
            _seq_spec(bb, tm, D_MODEL),
            _state_spec(bb, CONV_HIST, D_MODEL),
            _const_spec((1, D_MODEL)),
            _const_spec((1, D_MODEL)),
            _const_spec(w1.shape),
            _const_spec(b1.shape),
            _const_spec(wdw.shape),
            _const_spec(bdw.shape),
            _const_spec(lng.shape),
            _const_spec(lnb.shape),
            _const_spec(w2.shape),
            _const_spec(b2.shape),
        ],
        out_specs=(_seq_spec(bb, tm, D_MODEL), _state_spec(bb, CONV_HIST, D_MODEL)),
        scratch_shapes=[
            pltpu.VMEM((bb, CONV_PAD + tm, D_MODEL), F32),
            pltpu.VMEM((bb * tm, D_MODEL), F32),
            pltpu.VMEM((bb * tm, D_MODEL), BF16),
        ],
        compiler_params=_params(),
        name="conv_mixer",
    )(x, st, gin, gout, w1, b1, wdw, bdw, lng, lnb, w2, b2)


def _cmlp_body(x_ref, gin_ref, gout_ref, win_ref, bin_ref, lng_ref, lnb_ref, ws_ref, bs_ref,
               wout_ref, bout_ref, *rest, bb, tm, blk, emit_v):
    if emit_v:
        o_ref, v_ref, vbuf, ybuf = rest
    else:
        o_ref, vbuf, ybuf = rest
    rows = bb * tm
    x = x_ref[...].reshape(rows, D_MODEL)
    h = _rms(x, gin_ref[...]).astype(BF16)
    hv = jax.nn.gelu(_dot(h, win_ref[:, CMLP_DIM:]) + bin_ref[:, CMLP_DIM:])
    vln = _layernorm(hv, lng_ref[...], lnb_ref[...])
    if emit_v:
        v_ref[...] = vln.reshape(bb, tm, CMLP_DIM)
    vbuf[...] = vln.astype(BF16)

    pos_r = lax.broadcasted_iota(jnp.int32, (blk, blk), 0) // CHUNK
    pos_c = lax.broadcasted_iota(jnp.int32, (blk, blk), 1) // CHUNK
    for g in range(CMLP_GROUPS):
        lanes = slice(g * CMLP_GROUP_DIM, (g + 1) * CMLP_GROUP_DIM)
        ug = jax.nn.gelu(_dot(h, win_ref[:, lanes]) + bin_ref[:, lanes])
        wsm = jnp.where(pos_c <= pos_r, ws_ref[g], 0.0).astype(BF16)
        for n in range(rows // blk):
            rs = slice(n * blk, (n + 1) * blk)
            gate = _dot(wsm, vbuf[rs, lanes]) + bs_ref[g]
            ybuf[rs, lanes] = (ug[rs] * gate).astype(BF16)

    y = _dot(ybuf[...], wout_ref[...]) + bout_ref[...]
    o_ref[...] = (x + _rms(y, gout_ref[...])).reshape(bb, tm, D_MODEL)


def _cmlp_layer(x, gin, gout, win, bin_, lng, lnb, ws, bs, wout, bout, *, bb, tm, emit_v):
    B, T, _ = x.shape
    blk = min(T, MIX_BLOCK)
    assert tm % blk == 0
    ws = ws[:, :blk, :blk]
    bs = bs[:, :blk, None]
    body = functools.partial(_cmlp_body, bb=bb, tm=tm, blk=blk, emit_v=emit_v)
    out_shape = [jax.ShapeDtypeStruct(x.shape, F32)]
    out_specs = [_seq_spec(bb, tm, D_MODEL)]
    if emit_v:
        out_shape.append(jax.ShapeDtypeStruct((B, T, CMLP_DIM), F32))
        out_specs.append(_seq_spec(bb, tm, CMLP_DIM))
    return pl.pallas_call(
        body,
        grid=(B // bb, T // tm),
        out_shape=tuple(out_shape),
        in_specs=[
            _seq_spec(bb, tm, D_MODEL),
            _const_spec((1, D_MODEL)),
            _const_spec((1, D_MODEL)),
            _const_spec(win.shape),
            _const_spec(bin_.shape),
            _const_spec(lng.shape),
            _const_spec(lnb.shape),
            _const_spec(ws.shape),
            _const_spec(bs.shape),
            _const_spec(wout.shape),
            _const_spec(bout.shape),
        ],
        out_specs=tuple(out_specs),
        scratch_shapes=[
            pltpu.VMEM((bb * tm, CMLP_DIM), BF16),
            pltpu.VMEM((bb * tm, CMLP_DIM), BF16),
        ],
        compiler_params=_params(),
        name="cmlp_mixer",
    )(x, gin, gout, win, bin_, lng, lnb, ws, bs, wout, bout)


def _ffn_body(x_ref, st_ref, gin_ref, gout_ref, wup_ref, wdw_ref, bdw_ref, wdn_ref,
              o_ref, nst_ref, *scratch, bb, tm, nt):
    n_chunks = D_FF // FFN_CHUNK
    upbufs = scratch[:n_chunks]
    actbufs = scratch[n_chunks:]
    t = pl.program_id(1)
    rows = bb * tm
    hist0 = FFN_PAD - FFN_HIST

    def halves(c):
        return ((c * FFN_CHUNK, slice(0, FFN_CHUNK)),
                (D_FF + c * FFN_CHUNK, slice(FFN_CHUNK, 2 * FFN_CHUNK)))

    @pl.when(t == 0)
    def _():
        for c in range(n_chunks):
            for lo, lanes in halves(c):
                upbufs[c][:, hist0:FFN_PAD, lanes] = st_ref[:, :, lo:lo + FFN_CHUNK]

    x = x_ref[...].reshape(rows, D_MODEL)
    h = _rms(x, gin_ref[...]).astype(BF16)

    def up_proj(c):
        for lo, lanes in halves(c):
            up = _dot(h, wup_ref[:, lo:lo + FFN_CHUNK])
            upbufs[c][:, FFN_PAD:FFN_PAD + tm, lanes] = up.reshape(bb, tm, FFN_CHUNK)

    def conv(c, lo, lanes):
        acc = bdw_ref[:, lo:lo + FFN_CHUNK]
        for kk in range(FFN_CONV_WIDTH):
            acc = acc + (upbufs[c][:, hist0 + kk:hist0 + kk + tm, lanes]
                         * wdw_ref[kk:kk + 1, lo:lo + FFN_CHUNK])
        return acc

    def gate(c, actbuf, col):
        (glo, glanes), (ulo, ulanes) = halves(c)
        act = jax.nn.gelu(conv(c, glo, glanes)) * conv(c, ulo, ulanes)
        actbuf[:, col:col + FFN_CHUNK] = act.reshape(rows, FFN_CHUNK).astype(BF16)

    pending, start = [], 0
    for gi, size in enumerate(FFN_DOWN_GROUPS):
        pending.append((gi, start, start + size))
        start += size
    y = None
    for c in range(n_chunks):
        up_proj(c)
        while pending and (pending[0][2] - 1 + FFN_LOOKAHEAD <= c or c == n_chunks - 1):
            gi, c0, c1 = pending.pop(0)
            for cc in range(c0, c1):
                gate(cc, actbufs[gi], (cc - c0) * FFN_CHUNK)
            part = _dot(actbufs[gi][...], wdn_ref[c0 * FFN_CHUNK:c1 * FFN_CHUNK, :])
            y = part if y is None else y + part
    o_ref[...] = (x + _rms(y, gout_ref[...])).reshape(bb, tm, D_MODEL)

    for c in range(n_chunks):
        for lo, lanes in halves(c):
            last = upbufs[c][:, FFN_PAD + tm - FFN_HIST:FFN_PAD + tm, lanes]

            @pl.when(t == nt - 1)
            def _(last=last, lo=lo):
                nst_ref[:, :, lo:lo + FFN_CHUNK] = last

            if nt > 1:
                upbufs[c][:, hist0:FFN_PAD, lanes] = last


def _ffn_layer(x, st, gin, gout, wup, wdw, bdw, wdn, *, bb, tm):
    B, T, _ = x.shape
    nt = T // tm
    assert tm >= FFN_HIST and sum(FFN_DOWN_GROUPS) * FFN_CHUNK == D_FF
    body = functools.partial(_ffn_body, bb=bb, tm=tm, nt=nt)
    return pl.pallas_call(
        body,
        grid=(B // bb, nt),
        out_shape=(jax.ShapeDtypeStruct(x.shape, F32),
                   jax.ShapeDtypeStruct((B, FFN_HIST, 2 * D_FF), F32)),
        in_specs=[
            _seq_spec(bb, tm, D_MODEL),
            _state_spec(bb, FFN_HIST, 2 * D_FF),
            _const_spec((1, D_MODEL)),
            _const_spec((1, D_MODEL)),
            _const_spec(wup.shape),
            _const_spec(wdw.shape),
            _const_spec(bdw.shape),
            _const_spec(wdn.shape),
        ],
        out_specs=(_seq_spec(bb, tm, D_MODEL), _state_spec(bb, FFN_HIST, 2 * D_FF)),
        scratch_shapes=(
            [pltpu.VMEM((bb, FFN_PAD + tm, 2 * FFN_CHUNK), F32)] * (D_FF // FFN_CHUNK)
            + [pltpu.VMEM((bb * tm, size * FFN_CHUNK), BF16) for size in FFN_DOWN_GROUPS]),
        compiler_params=_params(),
        name="conv_ffn",
    )(x, st, gin, gout, wup, wdw, bdw, wdn)


def _trunk(x, caches, p, bias, *, bb, tm):
    prompt = caches is None
    B = x.shape[0]
    new_k, new_v, new_conv, new_cv, new_ffn = [], [], [], [], []
    for i in range(DEPTH):
        kind, j = i % N_MIXERS, i // N_MIXERS
        g = p['norm_gain'][i]
        gains = [g[n][None, :] for n in range(4)]
        if kind == 0:
            if prompt:
                kc = jnp.zeros((B, WINDOW, KV_DIM), F32)
                vc = kc
            else:
                kc = caches[0][j].reshape(B, WINDOW, KV_DIM)
                vc = caches[1][j].reshape(B, WINDOW, KV_DIM)
            x, k_rows, v_rows = _attn_layer(
                x, kc, vc, gains[0], gains[1], p['attn_w_qkv'][j], p['attn_b_qkv'][j][None, :],
                p['attn_w_o'][j], p['attn_b_o'][j][None, :], p['attn_sinks'][j], bias,
                bb=bb, tm=tm, mask_start=prompt)
            new_k.append(k_rows.reshape(B, -1, N_KV_HEADS, HEAD_DIM))
            new_v.append(v_rows.reshape(B, -1, N_KV_HEADS, HEAD_DIM))
        elif kind == 1:
            st = jnp.zeros((B, CONV_HIST, D_MODEL), F32) if prompt else caches[2][j]
            x, s = _conv_layer(
                x, st, gains[0], gains[1], p['conv_w_pw1'][j], p['conv_b_pw1'][j][None, :],
                p['conv_w_dw'][j], p['conv_b_dw'][j][None, :], p['conv_ln_g'][j][None, :],
                p['conv_ln_b'][j][None, :], p['conv_w_pw2'][j], p['conv_b_pw2'][j][None, :],
                bb=bb, tm=tm)
            new_conv.append(s)
        else:
            res = _cmlp_layer(
                x, gains[0], gains[1], p['cmlp_w_in'][j], p['cmlp_b_in'][j][None, :],
                p['cmlp_ln_g'][j][None, :], p['cmlp_ln_b'][j][None, :], p['cmlp_w_s'][j],
                p['cmlp_b_s'][j], p['cmlp_w_out'][j], p['cmlp_b_out'][j][None, :],
                bb=bb, tm=tm, emit_v=not prompt)
            x = res[0]
            if not prompt:
                new_cv.append(res[1])
        st = jnp.zeros((B, FFN_HIST, 2 * D_FF), F32) if prompt else caches[3][i]
        x, s = _ffn_layer(x, st, gains[2], gains[3], p['ffn_w_up'][i], p['ffn_w_dw'][i],
                          p['ffn_b_dw'][i][None, :], p['ffn_w_down'][i], bb=bb, tm=tm)
        new_ffn.append(s)
    cv = jnp.stack(new_cv) if new_cv else None
    return x, jnp.stack(new_k), jnp.stack(new_v), jnp.stack(new_conv), cv, jnp.stack(new_ffn)


def kernel(x_prompt, x_sample, cache_attn_k, cache_attn_v, state_conv, state_ffn_conv, rel_bias_table, norm_gain, attn_w_qkv, attn_b_qkv, attn_w_o, attn_b_o, attn_sinks, conv_w_pw1, conv_b_pw1, conv_w_dw, conv_b_dw, conv_ln_g, conv_ln_b, conv_w_pw2, conv_b_pw2, cmlp_w_in, cmlp_b_in, cmlp_ln_g, cmlp_ln_b, cmlp_w_s, cmlp_b_s, cmlp_w_out, cmlp_b_out, ffn_w_up, ffn_w_dw, ffn_b_dw, ffn_w_down):
    p = {
        'norm_gain': norm_gain,
        'attn_w_qkv': attn_w_qkv.astype(BF16), 'attn_b_qkv': attn_b_qkv,
        'attn_w_o': attn_w_o.astype(BF16), 'attn_b_o': attn_b_o, 'attn_sinks': attn_sinks,
        'conv_w_pw1': conv_w_pw1.astype(BF16), 'conv_b_pw1': conv_b_pw1, 'conv_w_dw': conv_w_dw,
        'conv_b_dw': conv_b_dw, 'conv_ln_g': conv_ln_g, 'conv_ln_b': conv_ln_b,
        'conv_w_pw2': conv_w_pw2.astype(BF16), 'conv_b_pw2': conv_b_pw2,
        'cmlp_w_in': cmlp_w_in.astype(BF16), 'cmlp_b_in': cmlp_b_in, 'cmlp_ln_g': cmlp_ln_g,
        'cmlp_ln_b': cmlp_ln_b, 'cmlp_w_s': cmlp_w_s, 'cmlp_b_s': cmlp_b_s,
        'cmlp_w_out': cmlp_w_out.astype(BF16), 'cmlp_b_out': cmlp_b_out,
        'ffn_w_up': ffn_w_up.astype(BF16), 'ffn_w_dw': ffn_w_dw, 'ffn_b_dw': ffn_b_dw,
        'ffn_w_down': ffn_w_down.astype(BF16),
    }
    bias = _rel_bias(rel_bias_table)
    y_prompt, p_attn_k, p_attn_v, p_conv, _, p_ffn_conv = _trunk(
        x_prompt, None, p, bias, bb=1, tm=512)
    y_sample, s_attn_k, s_attn_v, s_conv, s_cmlp_v, s_ffn_conv = _trunk(
        x_sample, (cache_attn_k, cache_attn_v, state_conv, state_ffn_conv), p, bias, bb=8, tm=64)
    return (y_prompt, y_sample, p_attn_k, p_attn_v, p_conv, p_ffn_conv,
            s_attn_k, s_attn_v, s_conv, s_cmlp_v, s_ffn_conv)
```

```python
import functools
import math

import jax
import jax.numpy as jnp
from jax import lax
from jax.experimental import pallas as pl
from jax.experimental.pallas import tpu as pltpu

D_MODEL = 1024
DEPTH = 4
CHUNK = 64
N_MIXERS = 3
N_HEADS = 16
N_KV_HEADS = 4
HEAD_DIM = 64
GQA_GROUP = N_HEADS // N_KV_HEADS
WINDOW = 128
KV_DIM = N_KV_HEADS * HEAD_DIM
Q_DIM = N_HEADS * HEAD_DIM
NUM_BUCKETS = 32
MAX_DISTANCE = 128
CONV_WIDTH = 31
CONV_HIST = CONV_WIDTH - 1
MIX_BLOCK = 128
CMLP_GROUPS = 4
CMLP_DIM = 2 * D_MODEL
CMLP_GROUP_DIM = CMLP_DIM // CMLP_GROUPS
D_FF = 2816
FFN_CONV_WIDTH = 3
FFN_HIST = FFN_CONV_WIDTH - 1
RMS_EPS = 1e-6
LN_EPS = 1e-5

F32 = jnp.float32
BF16 = jnp.bfloat16

V7X_VMEM_BYTES = 64 * 1024 * 1024
VMEM_LIMIT_BYTES = V7X_VMEM_BYTES - 8 * 1024 * 1024
SUBLANES = 8
FFN_CHUNK = 256
FFN_DOWN_GROUPS = (4, 4, 3)
FFN_LOOKAHEAD = 2
FFN_PAD = SUBLANES
CONV_PAD = 32
CONV_ROW_TILE = 64
CONV_LANE_TILE = 128
CONV_ROW_BLOCK = 256


def _rms(x, g):
    ms = jnp.mean(x * x, axis=-1, keepdims=True)
    return x * lax.rsqrt(ms + RMS_EPS) * g


def _layernorm(x, g, b):
    mu = jnp.mean(x, axis=-1, keepdims=True)
    xc = x - mu
    var = jnp.mean(xc * xc, axis=-1, keepdims=True)
    return xc * lax.rsqrt(var + LN_EPS) * g + b


def _dot(a, b):
    return jnp.dot(a, b, preferred_element_type=F32)


def _const_spec(shape):
    zeros = (0,) * len(shape)
    return pl.BlockSpec(shape, lambda b, t: zeros, pipeline_mode=pl.Buffered(1))


def _seq_spec(bb, rows, width):
    return pl.BlockSpec((bb, rows, width), lambda b, t: (b, t, 0))


def _state_spec(bb, rows, width):
    return pl.BlockSpec((bb, rows, width), lambda b, t: (b, 0, 0))


def _params():
    return pltpu.CompilerParams(
        dimension_semantics=("arbitrary", "arbitrary"),
        vmem_limit_bytes=VMEM_LIMIT_BYTES,
    )


HEAD_PAIR = 2 * HEAD_DIM
ATTN_LOOKAHEAD = 3
PAIR_STEP = GQA_GROUP // 2


def _pair_heads(j, v):
    return GQA_GROUP * j + v, GQA_GROUP * j + PAIR_STEP + v


def _bias_body(bucket_ref, table_ref, o_ref):
    bk = bucket_ref[...]
    low = lax.broadcasted_iota(jnp.int32, (1, HEAD_PAIR), 1) < CHUNK
    for j in range(N_KV_HEADS):
        for v in range(2):
            ha, hb = _pair_heads(j, v)
            acc = jnp.zeros(bk.shape, F32)
            for b in range(NUM_BUCKETS):
                val = jnp.where(low, table_ref[b, ha], table_ref[b, hb])
                acc = jnp.where(bk == b, val, acc)
            o_ref[j, v] = acc


def _t5_bucket(rel):
    half = NUM_BUCKETS // 2
    max_exact = half // 2
    n = jnp.abs(rel)
    log_ratio = jnp.log(jnp.maximum(n, 1).astype(F32) / max_exact) / math.log(MAX_DISTANCE / max_exact)
    large = jnp.minimum(max_exact + (log_ratio * (half - max_exact)).astype(jnp.int32), half - 1)
    return jnp.where(rel > 0, half, 0) + jnp.where(n < max_exact, n, large)


def _rel_bias(table):
    q_pos = jnp.arange(CHUNK, dtype=jnp.int32)
    k_pos = jnp.arange(WINDOW + CHUNK, dtype=jnp.int32) - WINDOW
    bucket = _t5_bucket(k_pos[:, None] - q_pos[None, :]).astype(jnp.int32)
    bucket = jnp.concatenate([bucket, bucket], axis=1)
    return pl.pallas_call(
        _bias_body,
        out_shape=jax.ShapeDtypeStruct((N_KV_HEADS, 2, WINDOW + CHUNK, HEAD_PAIR), F32),
        in_specs=[pl.BlockSpec(memory_space=pltpu.VMEM), pl.BlockSpec(memory_space=pltpu.SMEM)],
        out_specs=pl.BlockSpec(memory_space=pltpu.VMEM),
        name="rel_bias",
    )(bucket, table)


def _attn_body(x_ref, kc_ref, vc_ref, gin_ref, gout_ref, wqkv_ref, bqkv_ref, wo_ref, bo_ref,
               sinks_ref, bias_ref, o_ref, kout_ref, vout_ref, qbuf, kzbuf, vzbuf, obuf,
               *, bb, tm, nt, mask_start):
    t = pl.program_id(1)
    rows = bb * tm
    kv_rows = min(WINDOW, tm)
    span = WINDOW + CHUNK
    low = lax.broadcasted_iota(jnp.int32, (1, HEAD_PAIR), 1) < CHUNK

    def expand(buf, r0, kv):
        n = kv.shape[1]
        for pair in range(N_KV_HEADS // 2):
            tile = kv[:, :, pair * HEAD_PAIR:(pair + 1) * HEAD_PAIR]
            swapped = pltpu.roll(tile.reshape(bb * n, HEAD_PAIR), HEAD_DIM, 1).reshape(bb, n, HEAD_PAIR)
            for j, (lo_src, hi_src) in ((2 * pair, (tile, swapped)), (2 * pair + 1, (swapped, tile))):
                buf[:, j, 0, r0:r0 + n, :] = jnp.where(low, lo_src, 0.0).astype(BF16)
                buf[:, j, 1, r0:r0 + n, :] = jnp.where(low, 0.0, hi_src).astype(BF16)

    @pl.when(t == 0)
    def _():
        expand(kzbuf, 0, kc_ref[...])
        expand(vzbuf, 0, vc_ref[...])

    x = x_ref[...].reshape(rows, D_MODEL)
    h = _rms(x, gin_ref[...]).astype(BF16)
    qkv = _dot(h, wqkv_ref[...]) + bqkv_ref[...]
    qbuf[...] = (qkv[:, :Q_DIM] * (HEAD_DIM ** -0.5)).astype(BF16)
    k = qkv[:, Q_DIM:Q_DIM + KV_DIM].reshape(bb, tm, KV_DIM)
    v = qkv[:, Q_DIM + KV_DIM:].reshape(bb, tm, KV_DIM)
    expand(kzbuf, WINDOW, k)
    expand(vzbuf, WINDOW, v)

    @pl.when(t == nt - 1)
    def _():
        kout_ref[...] = k[:, tm - kv_rows:, :]
        vout_ref[...] = v[:, tm - kv_rows:, :]

    key_row = lax.broadcasted_iota(jnp.int32, (span, HEAD_PAIR), 0)
    first_valid = jnp.where(t == 0, WINDOW, 0)
    contract_last = (((1,), (1,)), ((), ()))
    contract_first = (((0,), (0,)), ((), ()))

    def scores(s, c, j):
        r0 = c * CHUNK
        q_rows = slice(s * tm + r0, s * tm + r0 + CHUNK)
        g0 = j * GQA_GROUP * HEAD_DIM
        q_nk = jnp.concatenate([qbuf[q_rows, g0:g0 + HEAD_PAIR],
                                qbuf[q_rows, g0 + HEAD_PAIR:g0 + 2 * HEAD_PAIR]], axis=0)
        out = []
        for v_idx in range(2):
            sc = lax.dot_general(kzbuf[s, j, v_idx, r0:r0 + span, :], q_nk, contract_last,
                                 preferred_element_type=F32)
            sc = sc + bias_ref[j, v_idx]
            if mask_start and r0 < WINDOW:
                sc = jnp.where(key_row + r0 >= first_valid, sc, -jnp.inf)
            out.append(sc)
        return out

    def attend(s, c, j, scs):
        r0 = c * CHUNK
        q_rows = slice(s * tm + r0, s * tm + r0 + CHUNK)
        g0 = j * GQA_GROUP * HEAD_DIM
        probs = []
        for v_idx, sc in enumerate(scs):
            ha, hb = _pair_heads(j, v_idx)
            sink = jnp.where(low, sinks_ref[ha], sinks_ref[hb])
            m = jnp.maximum(jnp.max(sc, axis=0, keepdims=True), sink)
            p = jnp.exp(sc - m)
            denom = jnp.sum(p, axis=0, keepdims=True) + jnp.exp(sink - m)
            probs.append((p * (1.0 / denom)).astype(BF16))
        p_kn = jnp.concatenate(probs, axis=0)
        v_kd = jnp.concatenate([vzbuf[s, j, 0, r0:r0 + span, :],
                                vzbuf[s, j, 1, r0:r0 + span, :]], axis=0)
        o = lax.dot_general(p_kn, v_kd, contract_first, preferred_element_type=F32)
        obuf[q_rows, g0:g0 + HEAD_PAIR] = o[:CHUNK].astype(BF16)
        obuf[q_rows, g0 + HEAD_PAIR:g0 + 2 * HEAD_PAIR] = o[CHUNK:].astype(BF16)

    units = [(s, c, j) for s in range(bb) for c in range(tm // CHUNK) for j in range(N_KV_HEADS)]
    pending = []
    for unit in units:
        pending.append((unit, scores(*unit)))
        if len(pending) > ATTN_LOOKAHEAD:
            done, scs = pending.pop(0)
            attend(*done, scs)
    for done, scs in pending:
        attend(*done, scs)

    y = _dot(obuf[...], wo_ref[...]) + bo_ref[...]
    o_ref[...] = (x + _rms(y, gout_ref[...])).reshape(bb, tm, D_MODEL)

    if nt > 1:
        kzbuf[:, :, :, 0:WINDOW, :] = kzbuf[:, :, :, tm:tm + WINDOW, :]
        vzbuf[:, :, :, 0:WINDOW, :] = vzbuf[:, :, :, tm:tm + WINDOW, :]


def _attn_layer(x, kc, vc, gin, gout, wqkv, bqkv, wo, bo, sinks, bias, *, bb, tm, mask_start):
    B, T, _ = x.shape
    nt = T // tm
    kv_rows = min(WINDOW, tm)
    body = functools.partial(_attn_body, bb=bb, tm=tm, nt=nt, mask_start=mask_start)
    return pl.pallas_call(
        body,
        grid=(B // bb, nt),
        out_shape=(jax.ShapeDtypeStruct(x.shape, F32),
                   jax.ShapeDtypeStruct((B, kv_rows, KV_DIM), F32),
                   jax.ShapeDtypeStruct((B, kv_rows, KV_DIM), F32)),
        in_specs=[
            _seq_spec(bb, tm, D_MODEL),
            _state_spec(bb, WINDOW, KV_DIM),
            _state_spec(bb, WINDOW, KV_DIM),
            _const_spec((1, D_MODEL)),
            _const_spec((1, D_MODEL)),
            _const_spec(wqkv.shape),
            _const_spec(bqkv.shape),
            _const_spec(wo.shape),
            _const_spec(bo.shape),
            pl.BlockSpec(memory_space=pltpu.SMEM),
            _const_spec(bias.shape),
        ],
        out_specs=(_seq_spec(bb, tm, D_MODEL),
                   _state_spec(bb, kv_rows, KV_DIM),
                   _state_spec(bb, kv_rows, KV_DIM)),
        scratch_shapes=[
            pltpu.VMEM((bb * tm, Q_DIM), BF16),
            pltpu.VMEM((bb, N_KV_HEADS, 2, WINDOW + tm, HEAD_PAIR), BF16),
            pltpu.VMEM((bb, N_KV_HEADS, 2, WINDOW + tm, HEAD_PAIR), BF16),
            pltpu.VMEM((bb * tm, Q_DIM), BF16),
        ],
        compiler_params=_params(),
        name="attn_mixer",
    )(x, kc, vc, gin, gout, wqkv, bqkv, wo, bo, sinks, bias)


def _conv_body(x_ref, st_ref, gin_ref, gout_ref, w1_ref, b1_ref, wdw_ref, bdw_ref, lng_ref, lnb_ref,
               w2_ref, b2_ref, o_ref, nst_ref, gbuf, ybuf, cbuf, *, bb, tm, nt):
    t = pl.program_id(1)
    rows = bb * tm
    hist0 = CONV_PAD - CONV_HIST

    @pl.when(t == 0)
    def _():
        gbuf[:, hist0:CONV_PAD, :] = st_ref[...]

    row = lax.broadcasted_iota(jnp.int32, (SUBLANES, CONV_LANE_TILE), 0)
    n_tiles = CONV_ROW_TILE // SUBLANES
    max_a = (hist0 + CONV_WIDTH - 1) // SUBLANES
    tiles_per_block = CONV_ROW_BLOCK // CONV_ROW_TILE

    def row_tiles(rb):
        out = []
        for i in range(tiles_per_block):
            flat = rb * CONV_ROW_BLOCK + i * CONV_ROW_TILE
            out.append((flat // tm, flat % tm, flat))
        return out

    def block_rows(ref, rb):
        flat = rb * CONV_ROW_BLOCK
        if tm >= CONV_ROW_BLOCK:
            return ref.at[flat // tm, flat % tm:flat % tm + CONV_ROW_BLOCK, :]
        return ref.at[flat // tm:(flat + CONV_ROW_BLOCK) // tm, :, :]

    def pointwise_in(rb):
        x = block_rows(x_ref, rb)[...].reshape(CONV_ROW_BLOCK, D_MODEL)
        h = _rms(x, gin_ref[...]).astype(BF16)
        ag = _dot(h, w1_ref[...]) + b1_ref[...]
        glu = ag[:, :D_MODEL] * jax.nn.sigmoid(ag[:, D_MODEL:])
        for i, (s, r0, _) in enumerate(row_tiles(rb)):
            gbuf[s, CONV_PAD + r0:CONV_PAD + r0 + CONV_ROW_TILE, :] = (
                glu[i * CONV_ROW_TILE:(i + 1) * CONV_ROW_TILE])

    def depthwise(rb):
        for l0 in range(0, D_MODEL, CONV_LANE_TILE):
            lanes = slice(l0, l0 + CONV_LANE_TILE)
            carried, carried_for = {}, None
            for s, r0, flat in row_tiles(rb):
                if carried_for != (s, r0):
                    carried = {}
                tiles = [gbuf[s, r0 + SUBLANES * j:r0 + SUBLANES * (j + 1), lanes]
                         for j in range(n_tiles + max_a)]
                out = [None] * n_tiles
                for b in range(SUBLANES):
                    taps = [a for a in range(max_a + 1)
                            if 0 <= SUBLANES * a + b - hist0 < CONV_WIDTH]
                    wts = [wdw_ref[SUBLANES * a + b - hist0:SUBLANES * a + b - hist0 + 1, lanes]
                           for a in taps]
                    z = []
                    for m in range(n_tiles + (1 if b else 0)):
                        if m == 0 and b in carried:
                            z.append(carried[b])
                            continue
                        acc = None
                        for a, wt in zip(taps, wts):
                            term = tiles[m + a] * wt
                            acc = term if acc is None else acc + term
                        z.append(acc)
                    if b:
                        carried[b] = z[n_tiles]
                    for m in range(n_tiles):
                        if b == 0:
                            part = z[m]
                        else:
                            part = pltpu.roll(jnp.where(row >= b, z[m], z[m + 1]), SUBLANES - b, 0)
                        out[m] = part if out[m] is None else out[m] + part
                carried_for = (s, r0 + CONV_ROW_TILE)
                for m in range(n_tiles):
                    lo = flat + SUBLANES * m
                    ybuf[lo:lo + SUBLANES, lanes] = out[m]
        for s, r0, flat in row_tiles(rb):
            rs = slice(flat, flat + CONV_ROW_TILE)
            y = _layernorm(ybuf[rs, :] + bdw_ref[...], lng_ref[...], lnb_ref[...])
            cbuf[rs, :] = (y * jax.nn.sigmoid(y)).astype(BF16)

    def pointwise_out(rb):
        flat = rb * CONV_ROW_BLOCK
        y = _dot(cbuf[flat:flat + CONV_ROW_BLOCK, :], w2_ref[...]) + b2_ref[...]
        x = block_rows(x_ref, rb)[...]
        block_rows(o_ref, rb)[...] = x + _rms(y, gout_ref[...]).reshape(x.shape)

    n_blocks = rows // CONV_ROW_BLOCK
    pointwise_in(0)
    for rb in range(1, n_blocks):
        depthwise(rb - 1)
        pointwise_in(rb)
        pointwise_out(rb - 1)
    depthwise(n_blocks - 1)
    pointwise_out(n_blocks - 1)

    last = gbuf[:, CONV_PAD + tm - CONV_HIST:CONV_PAD + tm, :]

    @pl.when(t == nt - 1)
    def _():
        nst_ref[...] = last

    if nt > 1:
        gbuf[:, hist0:CONV_PAD, :] = last


def _conv_layer(x, st, gin, gout, w1, b1, wdw, bdw, lng, lnb, w2, b2, *, bb, tm):
    B, T, _ = x.shape
    nt = T // tm
    assert tm >= CONV_HIST and tm % CONV_ROW_TILE == 0
    body = functools.partial(_conv_body, bb=bb, tm=tm, nt=nt)
    return pl.pallas_call(
        body,
        grid=(B // bb, nt),
        out_shape=(jax.ShapeDtypeStruct(x.shape, F32),
                   jax.ShapeDtypeStruct((B, CONV_HIST, D_MODEL), F32)),
        in_specs=[
            _seq_spec(bb, tm, D_MODEL),
            _state_spec(bb, CONV_HIST, D_MODEL),
            _const_spec((1, D_MODEL)),
            _const_spec((1, D_MODEL)),
            _const_spec(w1.shape),
            _const_spec(b1.shape),
            _const_spec(wdw.shape),
            _const_spec(bdw.shape),
            _const_spec(lng.shape),
            _const_spec(lnb.shape),
            _const_spec(w2.shape),
            _const_spec(b2.shape),
        ],
        out_specs=(_seq_spec(bb, tm, D_MODEL), _state_spec(bb, CONV_HIST, D_MODEL)),
        scratch_shapes=[
            pltpu.VMEM((bb, CONV_PAD + tm, D_MODEL), F32),
            pltpu.VMEM((bb * tm, D_MODEL), F32),
            pltpu.VMEM((bb * tm, D_MODEL), BF16),
        ],
        compiler_params=_params(),
        name="conv_mixer",
    )(x, st, gin, gout, w1, b1, wdw, bdw, lng, lnb, w2, b2)


def _cmlp_body(x_ref, gin_ref, gout_ref, win_ref, bin_ref, lng_ref, lnb_ref, ws_ref, bs_ref,
               wout_ref, bout_ref, *rest, bb, tm, blk, emit_v):
    if emit_v:
        o_ref, v_ref, vbuf, ybuf = rest
    else:
        o_ref, vbuf, ybuf = rest
    rows = bb * tm
    x = x_ref[...].reshape(rows, D_MODEL)
    h = _rms(x, gin_ref[...]).astype(BF16)
    hv = jax.nn.gelu(_dot(h, win_ref[:, CMLP_DIM:]) + bin_ref[:, CMLP_DIM:])
    vln = _layernorm(hv, lng_ref[...], lnb_ref[...])
    if emit_v:
        v_ref[...] = vln.reshape(bb, tm, CMLP_DIM)
    vbuf[...] = vln.astype(BF16)

    pos_r = lax.broadcasted_iota(jnp.int32, (blk, blk), 0) // CHUNK
    pos_c = lax.broadcasted_iota(jnp.int32, (blk, blk), 1) // CHUNK
    for g in range(CMLP_GROUPS):
        lanes = slice(g * CMLP_GROUP_DIM, (g + 1) * CMLP_GROUP_DIM)
        ug = jax.nn.gelu(_dot(h, win_ref[:, lanes]) + bin_ref[:, lanes])
        wsm = jnp.where(pos_c <= pos_r, ws_ref[g], 0.0).astype(BF16)
        for n in range(rows // blk):
            rs = slice(n * blk, (n + 1) * blk)
            gate = _dot(wsm, vbuf[rs, lanes]) + bs_ref[g]
            ybuf[rs, lanes] = (ug[rs] * gate).astype(BF16)

    y = _dot(ybuf[...], wout_ref[...]) + bout_ref[...]
    o_ref[...] = (x + _rms(y, gout_ref[...])).reshape(bb, tm, D_MODEL)


def _cmlp_layer(x, gin, gout, win, bin_, lng, lnb, ws, bs, wout, bout, *, bb, tm, emit_v):
    B, T, _ = x.shape
    blk = min(T, MIX_BLOCK)
    assert tm % blk == 0
    ws = ws[:, :blk, :blk]
    bs = bs[:, :blk, None]
    body = functools.partial(_cmlp_body, bb=bb, tm=tm, blk=blk, emit_v=emit_v)
    out_shape = [jax.ShapeDtypeStruct(x.shape, F32)]
    out_specs = [_seq_spec(bb, tm, D_MODEL)]
    if emit_v:
        out_shape.append(jax.ShapeDtypeStruct((B, T, CMLP_DIM), F32))
        out_specs.append(_seq_spec(bb, tm, CMLP_DIM))
    return pl.pallas_call(
        body,
        grid=(B // bb, T // tm),
        out_shape=tuple(out_shape),
        in_specs=[
            _seq_spec(bb, tm, D_MODEL),
            _const_spec((1, D_MODEL)),
            _const_spec((1, D_MODEL)),
            _const_spec(win.shape),
            _const_spec(bin_.shape),
            _const_spec(lng.shape),
            _const_spec(lnb.shape),
            _const_spec(ws.shape),
            _const_spec(bs.shape),
            _const_spec(wout.shape),
            _const_spec(bout.shape),
        ],
        out_specs=tuple(out_specs),
        scratch_shapes=[
            pltpu.VMEM((bb * tm, CMLP_DIM), BF16),
            pltpu.VMEM((bb * tm, CMLP_DIM), BF16),
        ],
        compiler_params=_params(),
        name="cmlp_mixer",
    )(x, gin, gout, win, bin_, lng, lnb, ws, bs, wout, bout)


def _ffn_body(x_ref, st_ref, gin_ref, gout_ref, wup_ref, wdw_ref, bdw_ref, wdn_ref,
              o_ref, nst_ref, *scratch, bb, tm, nt):
    n_chunks = D_FF // FFN_CHUNK
    upbufs = scratch[:n_chunks]
    actbufs = scratch[n_chunks:]
    t = pl.program_id(1)
    rows = bb * tm
    hist0 = FFN_PAD - FFN_HIST

    def halves(c):
        return ((c * FFN_CHUNK, slice(0, FFN_CHUNK)),
                (D_FF + c * FFN_CHUNK, slice(FFN_CHUNK, 2 * FFN_CHUNK)))

    @pl.when(t == 0)
    def _():
        for c in range(n_chunks):
            for lo, lanes in halves(c):
                upbufs[c][:, hist0:FFN_PAD, lanes] = st_ref[:, :, lo:lo + FFN_CHUNK]

    x = x_ref[...].reshape(rows, D_MODEL)
    h = _rms(x, gin_ref[...]).astype(BF16)

    def up_proj(c):
        for lo, lanes in halves(c):
            up = _dot(h, wup_ref[:, lo:lo + FFN_CHUNK])
            upbufs[c][:, FFN_PAD:FFN_PAD + tm, lanes] = up.reshape(bb, tm, FFN_CHUNK)

    def conv(c, lo, lanes):
        acc = bdw_ref[:, lo:lo + FFN_CHUNK]
        for kk in range(FFN_CONV_WIDTH):
            acc = acc + (upbufs[c][:, hist0 + kk:hist0 + kk + tm, lanes]
                         * wdw_ref[kk:kk + 1, lo:lo + FFN_CHUNK])
        return acc

    def gate(c, actbuf, col):
        (glo, glanes), (ulo, ulanes) = halves(c)
        act = jax.nn.gelu(conv(c, glo, glanes)) * conv(c, ulo, ulanes)
        actbuf[:, col:col + FFN_CHUNK] = act.reshape(rows, FFN_CHUNK).astype(BF16)

    pending, start = [], 0
    for gi, size in enumerate(FFN_DOWN_GROUPS):
        pending.append((gi, start, start + size))
        start += size
    y = None
    for c in range(n_chunks):
        up_proj(c)
        while pending and (pending[0][2] - 1 + FFN_LOOKAHEAD <= c or c == n_chunks - 1):
            gi, c0, c1 = pending.pop(0)
            for cc in range(c0, c1):
                gate(cc, actbufs[gi], (cc - c0) * FFN_CHUNK)
            part = _dot(actbufs[gi][...], wdn_ref[c0 * FFN_CHUNK:c1 * FFN_CHUNK, :])
            y = part if y is None else y + part
    o_ref[...] = (x + _rms(y, gout_ref[...])).reshape(bb, tm, D_MODEL)

    for c in range(n_chunks):
        for lo, lanes in halves(c):
            last = upbufs[c][:, FFN_PAD + tm - FFN_HIST:FFN_PAD + tm, lanes]

            @pl.when(t == nt - 1)
            def _(last=last, lo=lo):
                nst_ref[:, :, lo:lo + FFN_CHUNK] = last

            if nt > 1:
                upbufs[c][:, hist0:FFN_PAD, lanes] = last


def _ffn_layer(x, st, gin, gout, wup, wdw, bdw, wdn, *, bb, tm):
    B, T, _ = x.shape
    nt = T // tm
    assert tm >= FFN_HIST and sum(FFN_DOWN_GROUPS) * FFN_CHUNK == D_FF
    body = functools.partial(_ffn_body, bb=bb, tm=tm, nt=nt)
    return pl.pallas_call(
        body,
        grid=(B // bb, nt),
        out_shape=(jax.ShapeDtypeStruct(x.shape, F32),
                   jax.ShapeDtypeStruct((B, FFN_HIST, 2 * D_FF), F32)),
        in_specs=[
            _seq_spec(bb, tm, D_MODEL),
            _state_spec(bb, FFN_HIST, 2 * D_FF),
            _const_spec((1, D_MODEL)),
            _const_spec((1, D_MODEL)),
            _const_spec(wup.shape),
            _const_spec(wdw.shape),
            _const_spec(bdw.shape),
            _const_spec(wdn.shape),
        ],
        out_specs=(_seq_spec(bb, tm, D_MODEL), _state_spec(bb, FFN_HIST, 2 * D_FF)),
        scratch_shapes=(
            [pltpu.VMEM((bb, FFN_PAD + tm, 2 * FFN_CHUNK), F32)] * (D_FF // FFN_CHUNK)
            + [pltpu.VMEM((bb * tm, size * FFN_CHUNK), BF16) for size in FFN_DOWN_GROUPS]),
        compiler_params=_params(),
        name="conv_ffn",
    )(x, st, gin, gout, wup, wdw, bdw, wdn)


def _trunk(x, caches, p, bias, *, bb, tm):
    prompt = caches is None
    B = x.shape[0]
    new_k, new_v, new_conv, new_cv, new_ffn = [], [], [], [], []
    for i in range(DEPTH):
        kind, j = i % N_MIXERS, i // N_MIXERS
        g = p['norm_gain'][i]
        gains = [g[n][None, :] for n in range(4)]
        if kind == 0:
            if prompt:
                kc = jnp.zeros((B, WINDOW, KV_DIM), F32)
                vc = kc
            else:
                kc = caches[0][j].reshape(B, WINDOW, KV_DIM)
                vc = caches[1][j].reshape(B, WINDOW, KV_DIM)
            x, k_rows, v_rows = _attn_layer(
                x, kc, vc, gains[0], gains[1], p['attn_w_qkv'][j], p['attn_b_qkv'][j][None, :],
                p['attn_w_o'][j], p['attn_b_o'][j][None, :], p['attn_sinks'][j], bias,
                bb=bb, tm=tm, mask_start=prompt)
            new_k.append(k_rows.reshape(B, -1, N_KV_HEADS, HEAD_DIM))
            new_v.append(v_rows.reshape(B, -1, N_KV_HEADS, HEAD_DIM))
        elif kind == 1:
            st = jnp.zeros((B, CONV_HIST, D_MODEL), F32) if prompt else caches[2][j]
            x, s = _conv_layer(
                x, st, gains[0], gains[1], p['conv_w_pw1'][j], p['conv_b_pw1'][j][None, :],
                p['conv_w_dw'][j], p['conv_b_dw'][j][None, :], p['conv_ln_g'][j][None, :],
                p['conv_ln_b'][j][None, :], p['conv_w_pw2'][j], p['conv_b_pw2'][j][None, :],
                bb=bb, tm=tm)
            new_conv.append(s)
        else:
            res = _cmlp_layer(
                x, gains[0], gains[1], p['cmlp_w_in'][j], p['cmlp_b_in'][j][None, :],
                p['cmlp_ln_g'][j][None, :], p['cmlp_ln_b'][j][None, :], p['cmlp_w_s'][j],
                p['cmlp_b_s'][j], p['cmlp_w_out'][j], p['cmlp_b_out'][j][None, :],
                bb=bb, tm=tm, emit_v=not prompt)
            x = res[0]
            if not prompt:
                new_cv.append(res[1])
        st = jnp.zeros((B, FFN_HIST, 2 * D_FF), F32) if prompt else caches[3][i]
        x, s = _ffn_layer(x, st, gains[2], gains[3], p['ffn_w_up'][i], p['ffn_w_dw'][i],
                          p['ffn_b_dw'][i][None, :], p['ffn_w_down'][i], bb=bb, tm=tm)
        new_ffn.append(s)
    cv = jnp.stack(new_cv) if new_cv else None
    return x, jnp.stack(new_k), jnp.stack(new_v), jnp.stack(new_conv), cv, jnp.stack(new_ffn)


def kernel(x_prompt, x_sample, cache_attn_k, cache_attn_v, state_conv, state_ffn_conv, rel_bias_table, norm_gain, attn_w_qkv, attn_b_qkv, attn_w_o, attn_b_o, attn_sinks, conv_w_pw1, conv_b_pw1, conv_w_dw, conv_b_dw, conv_ln_g, conv_ln_b, conv_w_pw2, conv_b_pw2, cmlp_w_in, cmlp_b_in, cmlp_ln_g, cmlp_ln_b, cmlp_w_s, cmlp_b_s, cmlp_w_out, cmlp_b_out, ffn_w_up, ffn_w_dw, ffn_b_dw, ffn_w_down):
    p = {
        'norm_gain': norm_gain,
        'attn_w_qkv': attn_w_qkv.astype(BF16), 'attn_b_qkv': attn_b_qkv,
        'attn_w_o': attn_w_o.astype(BF16), 'attn_b_o': attn_b_o, 'attn_sinks': attn_sinks,
        'conv_w_pw1': conv_w_pw1.astype(BF16), 'conv_b_pw1': conv_b_pw1, 'conv_w_dw': conv_w_dw,
        'conv_b_dw': conv_b_dw, 'conv_ln_g': conv_ln_g, 'conv_ln_b': conv_ln_b,
        'conv_w_pw2': conv_w_pw2.astype(BF16), 'conv_b_pw2': conv_b_pw2,
        'cmlp_w_in': cmlp_w_in.astype(BF16), 'cmlp_b_in': cmlp_b_in, 'cmlp_ln_g': cmlp_ln_g,
        'cmlp_ln_b': cmlp_ln_b, 'cmlp_w_s': cmlp_w_s, 'cmlp_b_s': cmlp_b_s,
        'cmlp_w_out': cmlp_w_out.astype(BF16), 'cmlp_b_out': cmlp_b_out,
        'ffn_w_up': ffn_w_up.astype(BF16), 'ffn_w_dw': ffn_w_dw, 'ffn_b_dw': ffn_b_dw,
        'ffn_w_down': ffn_w_down.astype(BF16),
    }
    bias = _rel_bias(rel_bias_table)
    y_prompt, p_attn_k, p_attn_v, p_conv, _, p_ffn_conv = _trunk(
        x_prompt, None, p, bias, bb=1, tm=512)
    y_sample, s_attn_k, s_attn_v, s_conv, s_cmlp_v, s_ffn_conv = _trunk(
        x_sample, (cache_attn_k, cache_attn_v, state_conv, state_ffn_conv), p, bias, bb=8, tm=64)
    return (y_prompt, y_sample, p_attn_k, p_attn_v, p_conv, p_ffn_conv,
            s_attn_k, s_attn_v, s_conv, s_cmlp_v, s_ffn_conv)
```

```python
import functools
import math

import jax
import jax.numpy as jnp
from jax import lax
from jax.experimental import pallas as pl
from jax.experimental.pallas import tpu as pltpu

D_MODEL = 1024
DEPTH = 4
CHUNK = 64
N_MIXERS = 3
N_HEADS = 16
N_KV_HEADS = 4
HEAD_DIM = 64
GQA_GROUP = N_HEADS // N_KV_HEADS
WINDOW = 128
KV_DIM = N_KV_HEADS * HEAD_DIM
Q_DIM = N_HEADS * HEAD_DIM
NUM_BUCKETS = 32
MAX_DISTANCE = 128
CONV_WIDTH = 31
CONV_HIST = CONV_WIDTH - 1
MIX_BLOCK = 128
CMLP_GROUPS = 4
CMLP_DIM = 2 * D_MODEL
CMLP_GROUP_DIM = CMLP_DIM // CMLP_GROUPS
D_FF = 2816
FFN_CONV_WIDTH = 3
FFN_HIST = FFN_CONV_WIDTH - 1
RMS_EPS = 1e-6
_GELU_C0 = math.sqrt(2.0 / math.pi)
_GELU_C1 = _GELU_C0 * 0.044715
LN_EPS = 1e-5

F32 = jnp.float32
BF16 = jnp.bfloat16

V7X_VMEM_BYTES = 64 * 1024 * 1024
VMEM_LIMIT_BYTES = V7X_VMEM_BYTES - 8 * 1024 * 1024
SUBLANES = 8
FFN_CHUNK = 256
FFN_DOWN_GROUPS = (4, 4, 3)
FFN_ROW_SPLIT = 1
FFN_PAD = SUBLANES
CONV_PAD = 32
CONV_ROW_TILE = 64
CONV_LANE_TILE = 128
CONV_ROW_BLOCK = 256


def _rms(x, g):
    ms = jnp.mean(x * x, axis=-1, keepdims=True)
    return x * lax.rsqrt(ms + RMS_EPS) * g


def _layernorm(x, g, b):
    mu = jnp.mean(x, axis=-1, keepdims=True)
    xc = x - mu
    var = jnp.mean(xc * xc, axis=-1, keepdims=True)
    return xc * lax.rsqrt(var + LN_EPS) * g + b


def _gelu(x):
    hx = 0.5 * x
    return hx + hx * jnp.tanh(x * (_GELU_C0 + _GELU_C1 * (x * x)))


def _dot(a, b):
    return jnp.dot(a, b, preferred_element_type=F32)


def _const_spec(shape):
    zeros = (0,) * len(shape)
    return pl.BlockSpec(shape, lambda b, t: zeros, pipeline_mode=pl.Buffered(1))


def _layer_spec(stacked, layer):
    zeros = (0,) * (stacked.ndim - 1)
    return pl.BlockSpec((None,) + stacked.shape[1:], lambda b, t: (layer,) + zeros,
                        pipeline_mode=pl.Buffered(1))


def _seq_spec(bb, rows, width):
    return pl.BlockSpec((bb, rows, width), lambda b, t: (b, t, 0))


def _state_spec(bb, rows, width):
    return pl.BlockSpec((bb, rows, width), lambda b, t: (b, 0, 0))


def _params(flags=None):
    return pltpu.CompilerParams(
        dimension_semantics=("arbitrary", "arbitrary"),
        vmem_limit_bytes=VMEM_LIMIT_BYTES,
        flags=flags,
    )


HEAD_PAIR = 2 * HEAD_DIM
ATTN_LOOKAHEAD = 3
PAIR_STEP = GQA_GROUP // 2


def _pair_heads(j, v):
    return GQA_GROUP * j + v, GQA_GROUP * j + PAIR_STEP + v


def _bias_body(bucket_ref, table_ref, o_ref):
    bk = bucket_ref[...]
    low = lax.broadcasted_iota(jnp.int32, (1, HEAD_PAIR), 1) < CHUNK
    for j in range(N_KV_HEADS):
        for v in range(2):
            ha, hb = _pair_heads(j, v)
            acc = jnp.zeros(bk.shape, F32)
            for b in range(NUM_BUCKETS):
                val = jnp.where(low, table_ref[b, ha], table_ref[b, hb])
                acc = jnp.where(bk == b, val, acc)
            o_ref[j, v] = acc


def _t5_bucket(rel):
    half = NUM_BUCKETS // 2
    max_exact = half // 2
    n = jnp.abs(rel)
    log_ratio = jnp.log(jnp.maximum(n, 1).astype(F32) / max_exact) / math.log(MAX_DISTANCE / max_exact)
    large = jnp.minimum(max_exact + (log_ratio * (half - max_exact)).astype(jnp.int32), half - 1)
    return jnp.where(rel > 0, half, 0) + jnp.where(n < max_exact, n, large)


def _rel_bias(table):
    q_pos = jnp.arange(CHUNK, dtype=jnp.int32)
    k_pos = jnp.arange(WINDOW + CHUNK, dtype=jnp.int32) - WINDOW
    bucket = _t5_bucket(k_pos[:, None] - q_pos[None, :]).astype(jnp.int32)
    bucket = jnp.concatenate([bucket, bucket], axis=1)
    return pl.pallas_call(
        _bias_body,
        out_shape=jax.ShapeDtypeStruct((N_KV_HEADS, 2, WINDOW + CHUNK, HEAD_PAIR), F32),
        in_specs=[pl.BlockSpec(memory_space=pltpu.VMEM), pl.BlockSpec(memory_space=pltpu.SMEM)],
        out_specs=pl.BlockSpec(memory_space=pltpu.VMEM),
        name="rel_bias",
    )(bucket, table)


def _attn_body(x_ref, kc_ref, vc_ref, gin_ref, gout_ref, wqkv_ref, bqkv_ref, wo_ref, bo_ref,
               sinks_ref, bias_ref, o_ref, kout_ref, vout_ref, qbuf, kzbuf, vzbuf, obuf,
               *, bb, tm, nt, mask_start):
    t = pl.program_id(1)
    rows = bb * tm
    kv_rows = min(WINDOW, tm)
    span = WINDOW + CHUNK
    low = lax.broadcasted_iota(jnp.int32, (1, HEAD_PAIR), 1) < CHUNK

    def expand(buf, r0, kv):
        n = kv.shape[1]
        for pair in range(N_KV_HEADS // 2):
            tile = kv[:, :, pair * HEAD_PAIR:(pair + 1) * HEAD_PAIR]
            swapped = pltpu.roll(tile.reshape(bb * n, HEAD_PAIR), HEAD_DIM, 1).reshape(bb, n, HEAD_PAIR)
            for j, (lo_src, hi_src) in ((2 * pair, (tile, swapped)), (2 * pair + 1, (swapped, tile))):
                buf[:, j, 0, r0:r0 + n, :] = jnp.where(low, lo_src, 0.0).astype(BF16)
                buf[:, j, 1, r0:r0 + n, :] = jnp.where(low, 0.0, hi_src).astype(BF16)

    @pl.when(t == 0)
    def _():
        expand(kzbuf, 0, kc_ref[...])
        expand(vzbuf, 0, vc_ref[...])

    x = x_ref[...].reshape(rows, D_MODEL)
    h = _rms(x, gin_ref[...]).astype(BF16)
    qkv = _dot(h, wqkv_ref[...]) + bqkv_ref[...]
    qbuf[...] = (qkv[:, :Q_DIM] * (HEAD_DIM ** -0.5)).astype(BF16)
    k = qkv[:, Q_DIM:Q_DIM + KV_DIM].reshape(bb, tm, KV_DIM)
    v = qkv[:, Q_DIM + KV_DIM:].reshape(bb, tm, KV_DIM)
    expand(kzbuf, WINDOW, k)
    expand(vzbuf, WINDOW, v)

    @pl.when(t == nt - 1)
    def _():
        kout_ref[...] = k[:, tm - kv_rows:, :]
        vout_ref[...] = v[:, tm - kv_rows:, :]

    key_row = lax.broadcasted_iota(jnp.int32, (span, HEAD_PAIR), 0)
    first_valid = jnp.where(t == 0, WINDOW, 0)
    contract_last = (((1,), (1,)), ((), ()))
    contract_first = (((0,), (0,)), ((), ()))

    def scores(s, c, j):
        r0 = c * CHUNK
        q_rows = slice(s * tm + r0, s * tm + r0 + CHUNK)
        g0 = j * GQA_GROUP * HEAD_DIM
        q_nk = jnp.concatenate([qbuf[q_rows, g0:g0 + HEAD_PAIR],
                                qbuf[q_rows, g0 + HEAD_PAIR:g0 + 2 * HEAD_PAIR]], axis=0)
        out = []
        for v_idx in range(2):
            sc = lax.dot_general(kzbuf[s, j, v_idx, r0:r0 + span, :], q_nk, contract_last,
                                 preferred_element_type=F32)
            sc = sc + bias_ref[j, v_idx]
            if mask_start and r0 < WINDOW:
                sc = jnp.where(key_row + r0 >= first_valid, sc, -jnp.inf)
            out.append(sc)
        return out

    def attend(s, c, j, scs):
        r0 = c * CHUNK
        q_rows = slice(s * tm + r0, s * tm + r0 + CHUNK)
        g0 = j * GQA_GROUP * HEAD_DIM
        probs = []
        for v_idx, sc in enumerate(scs):
            ha, hb = _pair_heads(j, v_idx)
            sink = jnp.where(low, sinks_ref[ha], sinks_ref[hb])
            m = jnp.maximum(jnp.max(sc, axis=0, keepdims=True), sink)
            p = jnp.exp(sc - m)
            denom = jnp.sum(p, axis=0, keepdims=True) + jnp.exp(sink - m)
            probs.append((p * (1.0 / denom)).astype(BF16))
        p_kn = jnp.concatenate(probs, axis=0)
        v_kd = jnp.concatenate([vzbuf[s, j, 0, r0:r0 + span, :],
                                vzbuf[s, j, 1, r0:r0 + span, :]], axis=0)
        o = lax.dot_general(p_kn, v_kd, contract_first, preferred_element_type=F32)
        obuf[q_rows, g0:g0 + HEAD_PAIR] = o[:CHUNK].astype(BF16)
        obuf[q_rows, g0 + HEAD_PAIR:g0 + 2 * HEAD_PAIR] = o[CHUNK:].astype(BF16)

    units = [(s, c, j) for s in range(bb) for c in range(tm // CHUNK) for j in range(N_KV_HEADS)]
    pending = []
    for unit in units:
        pending.append((unit, scores(*unit)))
        if len(pending) > ATTN_LOOKAHEAD:
            done, scs = pending.pop(0)
            attend(*done, scs)
    for done, scs in pending:
        attend(*done, scs)

    y = _dot(obuf[...], wo_ref[...]) + bo_ref[...]
    o_ref[...] = (x + _rms(y, gout_ref[...])).reshape(bb, tm, D_MODEL)

    if nt > 1:
        kzbuf[:, :, :, 0:WINDOW, :] = kzbuf[:, :, :, tm:tm + WINDOW, :]
        vzbuf[:, :, :, 0:WINDOW, :] = vzbuf[:, :, :, tm:tm + WINDOW, :]


def _attn_layer(x, kc, vc, gin, gout, wqkv, bqkv, wo, bo, sinks, bias, *, layer, bb, tm, mask_start):
    B, T, _ = x.shape
    nt = T // tm
    kv_rows = min(WINDOW, tm)
    body = functools.partial(_attn_body, bb=bb, tm=tm, nt=nt, mask_start=mask_start)
    return pl.pallas_call(
        body,
        grid=(B // bb, nt),
        out_shape=(jax.ShapeDtypeStruct(x.shape, F32),
                   jax.ShapeDtypeStruct((B, kv_rows, KV_DIM), F32),
                   jax.ShapeDtypeStruct((B, kv_rows, KV_DIM), F32)),
        in_specs=[
            _seq_spec(bb, tm, D_MODEL),
            _state_spec(bb, WINDOW, KV_DIM),
            _state_spec(bb, WINDOW, KV_DIM),
            _const_spec((1, D_MODEL)),
            _const_spec((1, D_MODEL)),
            _layer_spec(wqkv, layer),
            _const_spec(bqkv.shape),
            _layer_spec(wo, layer),
            _const_spec(bo.shape),
            pl.BlockSpec(memory_space=pltpu.SMEM),
            _const_spec(bias.shape),
        ],
        out_specs=(_seq_spec(bb, tm, D_MODEL),
                   _state_spec(bb, kv_rows, KV_DIM),
                   _state_spec(bb, kv_rows, KV_DIM)),
        scratch_shapes=[
            pltpu.VMEM((bb * tm, Q_DIM), BF16),
            pltpu.VMEM((bb, N_KV_HEADS, 2, WINDOW + tm, HEAD_PAIR), BF16),
            pltpu.VMEM((bb, N_KV_HEADS, 2, WINDOW + tm, HEAD_PAIR), BF16),
            pltpu.VMEM((bb * tm, Q_DIM), BF16),
        ],
        compiler_params=_params(),
        name="attn_mixer",
    )(x, kc, vc, gin, gout, wqkv, bqkv, wo, bo, sinks, bias)


def _conv_body(x_ref, st_ref, gin_ref, gout_ref, w1_ref, b1_ref, wdw_ref, bdw_ref, lng_ref, lnb_ref,
               w2_ref, b2_ref, o_ref, nst_ref, gbuf, ybuf, cbuf, *, bb, tm, nt):
    t = pl.program_id(1)
    rows = bb * tm
    hist0 = CONV_PAD - CONV_HIST

    @pl.when(t == 0)
    def _():
        gbuf[:, hist0:CONV_PAD, :] = st_ref[...]

    row = lax.broadcasted_iota(jnp.int32, (SUBLANES, CONV_LANE_TILE), 0)
    n_tiles = CONV_ROW_TILE // SUBLANES
    max_a = (hist0 + CONV_WIDTH - 1) // SUBLANES
    tiles_per_block = CONV_ROW_BLOCK // CONV_ROW_TILE

    def row_tiles(rb):
        out = []
        for i in range(tiles_per_block):
            flat = rb * CONV_ROW_BLOCK + i * CONV_ROW_TILE
            out.append((flat // tm, flat % tm, flat))
        return out

    def block_rows(ref, rb):
        flat = rb * CONV_ROW_BLOCK
        if tm >= CONV_ROW_BLOCK:
            return ref.at[flat // tm, flat % tm:flat % tm + CONV_ROW_BLOCK, :]
        return ref.at[flat // tm:(flat + CONV_ROW_BLOCK) // tm, :, :]

    def pointwise_in(rb):
        x = block_rows(x_ref, rb)[...].reshape(CONV_ROW_BLOCK, D_MODEL)
        h = _rms(x, gin_ref[...]).astype(BF16)
        ag = _dot(h, w1_ref[...]) + b1_ref[...]
        glu = ag[:, :D_MODEL] * jax.nn.sigmoid(ag[:, D_MODEL:])
        for i, (s, r0, _) in enumerate(row_tiles(rb)):
            gbuf[s, CONV_PAD + r0:CONV_PAD + r0 + CONV_ROW_TILE, :] = (
                glu[i * CONV_ROW_TILE:(i + 1) * CONV_ROW_TILE])

    def depthwise(rb):
        for l0 in range(0, D_MODEL, CONV_LANE_TILE):
            lanes = slice(l0, l0 + CONV_LANE_TILE)
            carried, carried_for = {}, None
            for s, r0, flat in row_tiles(rb):
                if carried_for != (s, r0):
                    carried = {}
                tiles = [gbuf[s, r0 + SUBLANES * j:r0 + SUBLANES * (j + 1), lanes]
                         for j in range(n_tiles + max_a)]
                out = [None] * n_tiles
                for b in range(SUBLANES):
                    taps = [a for a in range(max_a + 1)
                            if 0 <= SUBLANES * a + b - hist0 < CONV_WIDTH]
                    wts = [wdw_ref[SUBLANES * a + b - hist0:SUBLANES * a + b - hist0 + 1, lanes]
                           for a in taps]
                    z = []
                    for m in range(n_tiles + (1 if b else 0)):
                        if m == 0 and b in carried:
                            z.append(carried[b])
                            continue
                        acc = None
                        for a, wt in zip(taps, wts):
                            term = tiles[m + a] * wt
                            acc = term if acc is None else acc + term
                        z.append(acc)
                    if b:
                        carried[b] = z[n_tiles]
                    for m in range(n_tiles):
                        if b == 0:
                            part = z[m]
                        else:
                            part = pltpu.roll(jnp.where(row >= b, z[m], z[m + 1]), SUBLANES - b, 0)
                        out[m] = part if out[m] is None else out[m] + part
                carried_for = (s, r0 + CONV_ROW_TILE)
                for m in range(n_tiles):
                    lo = flat + SUBLANES * m
                    ybuf[lo:lo + SUBLANES, lanes] = out[m]
        for s, r0, flat in row_tiles(rb):
            rs = slice(flat, flat + CONV_ROW_TILE)
            y = _layernorm(ybuf[rs, :] + bdw_ref[...], lng_ref[...], lnb_ref[...])
            cbuf[rs, :] = (y * jax.nn.sigmoid(y)).astype(BF16)

    def pointwise_out(rb):
        flat = rb * CONV_ROW_BLOCK
        y = _dot(cbuf[flat:flat + CONV_ROW_BLOCK, :], w2_ref[...]) + b2_ref[...]
        x = block_rows(x_ref, rb)[...]
        block_rows(o_ref, rb)[...] = x + _rms(y, gout_ref[...]).reshape(x.shape)

    n_blocks = rows // CONV_ROW_BLOCK
    pointwise_in(0)
    for rb in range(1, n_blocks):
        depthwise(rb - 1)
        pointwise_in(rb)
        pointwise_out(rb - 1)
    depthwise(n_blocks - 1)
    pointwise_out(n_blocks - 1)

    last = gbuf[:, CONV_PAD + tm - CONV_HIST:CONV_PAD + tm, :]

    @pl.when(t == nt - 1)
    def _():
        nst_ref[...] = last

    if nt > 1:
        gbuf[:, hist0:CONV_PAD, :] = last


def _conv_layer(x, st, gin, gout, w1, b1, wdw, bdw, lng, lnb, w2, b2, *, bb, tm):
    B, T, _ = x.shape
    nt = T // tm
    assert tm >= CONV_HIST and tm % CONV_ROW_TILE == 0
    body = functools.partial(_conv_body, bb=bb, tm=tm, nt=nt)
    return pl.pallas_call(
        body,
        grid=(B // bb, nt),
        out_shape=(jax.ShapeDtypeStruct(x.shape, F32),
                   jax.ShapeDtypeStruct((B, CONV_HIST, D_MODEL), F32)),
        in_specs=[
            _seq_spec(bb, tm, D_MODEL),
            _state_spec(bb, CONV_HIST, D_MODEL),
            _const_spec((1, D_MODEL)),
            _const_spec((1, D_MODEL)),
            _const_spec(w1.shape),
            _const_spec(b1.shape),
            _const_spec(wdw.shape),
            _const_spec(bdw.shape),
            _const_spec(lng.shape),
            _const_spec(lnb.shape),
            _const_spec(w2.shape),
            _const_spec(b2.shape),
        ],
        out_specs=(_seq_spec(bb, tm, D_MODEL), _state_spec(bb, CONV_HIST, D_MODEL)),
        scratch_shapes=[
            pltpu.VMEM((bb, CONV_PAD + tm, D_MODEL), F32),
            pltpu.VMEM((bb * tm, D_MODEL), F32),
            pltpu.VMEM((bb * tm, D_MODEL), BF16),
        ],
        compiler_params=_params(),
        name="conv_mixer",
    )(x, st, gin, gout, w1, b1, wdw, bdw, lng, lnb, w2, b2)


def _cmlp_body(x_ref, gin_ref, gout_ref, win_ref, bin_ref, lng_ref, lnb_ref, ws_ref, bs_ref,
               wout_ref, bout_ref, *rest, bb, tm, blk, emit_v):
    if emit_v:
        o_ref, v_ref, vbuf, ybuf = rest
    else:
        o_ref, vbuf, ybuf = rest
    rows = bb * tm
    x = x_ref[...].reshape(rows, D_MODEL)
    h = _rms(x, gin_ref[...]).astype(BF16)
    hv = _gelu(_dot(h, win_ref[:, CMLP_DIM:]) + bin_ref[:, CMLP_DIM:])
    vln = _layernorm(hv, lng_ref[...], lnb_ref[...])
    if emit_v:
        v_ref[...] = vln.reshape(bb, tm, CMLP_DIM)
    vbuf[...] = vln.astype(BF16)

    pos_r = lax.broadcasted_iota(jnp.int32, (blk, blk), 0) // CHUNK
    pos_c = lax.broadcasted_iota(jnp.int32, (blk, blk), 1) // CHUNK
    for g in range(CMLP_GROUPS):
        lanes = slice(g * CMLP_GROUP_DIM, (g + 1) * CMLP_GROUP_DIM)
        ug = _gelu(_dot(h, win_ref[:, lanes]) + bin_ref[:, lanes])
        wsm = jnp.where(pos_c <= pos_r, ws_ref[g], 0.0).astype(BF16)
        for n in range(rows // blk):
            rs = slice(n * blk, (n + 1) * blk)
            gate = _dot(wsm, vbuf[rs, lanes]) + bs_ref[g]
            ybuf[rs, lanes] = (ug[rs] * gate).astype(BF16)

    y = _dot(ybuf[...], wout_ref[...]) + bout_ref[...]
    o_ref[...] = (x + _rms(y, gout_ref[...])).reshape(bb, tm, D_MODEL)


def _cmlp_layer(x, gin, gout, win, bin_, lng, lnb, ws, bs, wout, bout, *, bb, tm, emit_v):
    B, T, _ = x.shape
    blk = min(T, MIX_BLOCK)
    assert tm % blk == 0
    ws = ws[:, :blk, :blk]
    bs = bs[:, :blk, None]
    body = functools.partial(_cmlp_body, bb=bb, tm=tm, blk=blk, emit_v=emit_v)
    out_shape = [jax.ShapeDtypeStruct(x.shape, F32)]
    out_specs = [_seq_spec(bb, tm, D_MODEL)]
    if emit_v:
        out_shape.append(jax.ShapeDtypeStruct((B, T, CMLP_DIM), F32))
        out_specs.append(_seq_spec(bb, tm, CMLP_DIM))
    return pl.pallas_call(
        body,
        grid=(B // bb, T // tm),
        out_shape=tuple(out_shape),
        in_specs=[
            _seq_spec(bb, tm, D_MODEL),
            _const_spec((1, D_MODEL)),
            _const_spec((1, D_MODEL)),
            _const_spec(win.shape),
            _const_spec(bin_.shape),
            _const_spec(lng.shape),
            _const_spec(lnb.shape),
            _const_spec(ws.shape),
            _const_spec(bs.shape),
            _const_spec(wout.shape),
            _const_spec(bout.shape),
        ],
        out_specs=tuple(out_specs),
        scratch_shapes=[
            pltpu.VMEM((bb * tm, CMLP_DIM), BF16),
            pltpu.VMEM((bb * tm, CMLP_DIM), BF16),
        ],
        compiler_params=_params(),
        name="cmlp_mixer",
    )(x, gin, gout, win, bin_, lng, lnb, ws, bs, wout, bout)


def _ffn_body(x_ref, st_ref, gin_ref, gout_ref, wup_ref, wdw_ref, bdw_ref, wdn_ref,
              o_ref, nst_ref, *scratch, bb, tm, nt):
    n_chunks = D_FF // FFN_CHUNK
    upbufs = scratch[:n_chunks]
    actbufs = scratch[n_chunks:]
    t = pl.program_id(1)
    rows = bb * tm
    hist0 = FFN_PAD - FFN_HIST

    def halves(c):
        return ((c * FFN_CHUNK, slice(0, FFN_CHUNK)),
                (D_FF + c * FFN_CHUNK, slice(FFN_CHUNK, 2 * FFN_CHUNK)))

    @pl.when(t == 0)
    def _():
        for c in range(n_chunks):
            upbufs[c][:, 0:hist0, :] = jnp.zeros((bb, hist0, 2 * FFN_CHUNK), F32)
            for lo, lanes in halves(c):
                upbufs[c][:, hist0:FFN_PAD, lanes] = st_ref[:, :, lo:lo + FFN_CHUNK]

    if bb >= FFN_ROW_SPLIT:
        sub_b, sub_t = bb // FFN_ROW_SPLIT, tm
        subs = [(slice(i * sub_b, (i + 1) * sub_b), 0) for i in range(FFN_ROW_SPLIT)]
    else:
        sub_b, sub_t = bb, tm // FFN_ROW_SPLIT
        subs = [(slice(0, bb), i * sub_t) for i in range(FFN_ROW_SPLIT)]
    sub_rows = sub_b * sub_t
    n_tiles = sub_t // SUBLANES
    tile_row = lax.broadcasted_iota(jnp.int32, (sub_b * n_tiles, SUBLANES, FFN_CHUNK), 1)

    def up_proj(h, sub, c):
        sl, r0 = sub
        for lo, lanes in halves(c):
            up = _dot(h, wup_ref[:, lo:lo + FFN_CHUNK])
            upbufs[c][sl, FFN_PAD + r0:FFN_PAD + r0 + sub_t, lanes] = up.reshape(sub_b, sub_t, FFN_CHUNK)

    def delayed(cur, hist):
        cur_r = pltpu.roll(cur, 1, 1)
        hist_r = pltpu.roll(hist, 1, 1)
        cur_r4 = cur_r.reshape(sub_b, n_tiles, SUBLANES, FFN_CHUNK)
        prev_r = jnp.concatenate([hist_r[:, None], cur_r4[:, :n_tiles - 1]], axis=1)
        out = jnp.where(tile_row < 1, prev_r.reshape(sub_b * n_tiles, SUBLANES, FFN_CHUNK), cur_r)
        return out, hist_r

    def conv(sub, c, lo, lanes):
        sl, r0 = sub
        w = wdw_ref[:, lo:lo + FFN_CHUNK]
        cur = upbufs[c][sl, FFN_PAD + r0:FFN_PAD + r0 + sub_t, lanes].reshape(
            sub_b * n_tiles, SUBLANES, FFN_CHUNK)
        hist = upbufs[c][sl, r0:r0 + FFN_PAD, lanes]
        acc, acc_hist = None, None
        for kk in range(FFN_HIST):
            term, term_hist = cur * w[kk:kk + 1], hist * w[kk:kk + 1]
            if acc is not None:
                term, term_hist = term + acc, term_hist + acc_hist
            acc, acc_hist = delayed(term, term_hist)
        out = cur * w[FFN_HIST:FFN_HIST + 1] + bdw_ref[:, lo:lo + FFN_CHUNK] + acc
        return out.reshape(sub_rows, FFN_CHUNK)

    def gate(sub, c, actbuf, col):
        (glo, glanes), (ulo, ulanes) = halves(c)
        act = _gelu(conv(sub, c, glo, glanes)) * conv(sub, c, ulo, ulanes)
        actbuf[:, col:col + FFN_CHUNK] = act.astype(BF16)

    groups, start = [], 0
    for size in FFN_DOWN_GROUPS:
        groups.append((start, start + size))
        start += size
    xs = []
    for si, sub in enumerate(subs):
        sl, r0 = sub
        x = x_ref[sl, r0:r0 + sub_t, :].reshape(sub_rows, D_MODEL)
        xs.append(x)
        h = _rms(x, gin_ref[...]).astype(BF16)
        for gi, (c0, c1) in enumerate(groups):
            for c in range(c0, c1):
                up_proj(h, sub, c)
                gate(sub, c, actbufs[si * len(groups) + gi], (c - c0) * FFN_CHUNK)
    for si, sub in enumerate(subs):
        sl, r0 = sub
        y = None
        for gi, (c0, c1) in enumerate(groups):
            part = _dot(actbufs[si * len(groups) + gi][...], wdn_ref[c0 * FFN_CHUNK:c1 * FFN_CHUNK, :])
            y = part if y is None else y + part
        o_ref[sl, r0:r0 + sub_t, :] = (xs[si] + _rms(y, gout_ref[...])).reshape(sub_b, sub_t, D_MODEL)

    for c in range(n_chunks):
        for lo, lanes in halves(c):
            last = upbufs[c][:, FFN_PAD + tm - FFN_HIST:FFN_PAD + tm, lanes]

            @pl.when(t == nt - 1)
            def _(last=last, lo=lo):
                nst_ref[:, :, lo:lo + FFN_CHUNK] = last

            if nt > 1:
                upbufs[c][:, hist0:FFN_PAD, lanes] = last


def _ffn_layer(x, st, gin, gout, wup, wdw, bdw, wdn, *, layer, bb, tm):
    B, T, _ = x.shape
    nt = T // tm
    assert tm >= FFN_HIST and sum(FFN_DOWN_GROUPS) * FFN_CHUNK == D_FF
    body = functools.partial(_ffn_body, bb=bb, tm=tm, nt=nt)
    return pl.pallas_call(
        body,
        grid=(B // bb, nt),
        out_shape=(jax.ShapeDtypeStruct(x.shape, F32),
                   jax.ShapeDtypeStruct((B, FFN_HIST, 2 * D_FF), F32)),
        in_specs=[
            _seq_spec(bb, tm, D_MODEL),
            _state_spec(bb, FFN_HIST, 2 * D_FF),
            _const_spec((1, D_MODEL)),
            _const_spec((1, D_MODEL)),
            _layer_spec(wup, layer),
            _const_spec(wdw.shape),
            _const_spec(bdw.shape),
            _layer_spec(wdn, layer),
        ],
        out_specs=(_seq_spec(bb, tm, D_MODEL), _state_spec(bb, FFN_HIST, 2 * D_FF)),
        scratch_shapes=(
            [pltpu.VMEM((bb, FFN_PAD + tm, 2 * FFN_CHUNK), F32)] * (D_FF // FFN_CHUNK)
            + [pltpu.VMEM((bb * tm // FFN_ROW_SPLIT, size * FFN_CHUNK), BF16)
               for _ in range(FFN_ROW_SPLIT) for size in FFN_DOWN_GROUPS]),
        compiler_params=_params(),
        name="conv_ffn",
    )(x, st, gin, gout, wup, wdw, bdw, wdn)


def _trunk(x, caches, p, bias, *, bb, tm):
    prompt = caches is None
    B = x.shape[0]
    new_k, new_v, new_conv, new_cv, new_ffn = [], [], [], [], []
    for i in range(DEPTH):
        kind, j = i % N_MIXERS, i // N_MIXERS
        g = p['norm_gain'][i]
        gains = [g[n][None, :] for n in range(4)]
        if kind == 0:
            if prompt:
                kc = jnp.zeros((B, WINDOW, KV_DIM), F32)
                vc = kc
            else:
                kc = caches[0][j].reshape(B, WINDOW, KV_DIM)
                vc = caches[1][j].reshape(B, WINDOW, KV_DIM)
            x, k_rows, v_rows = _attn_layer(
                x, kc, vc, gains[0], gains[1], p['attn_w_qkv'], p['attn_b_qkv'][j][None, :],
                p['attn_w_o'], p['attn_b_o'][j][None, :], p['attn_sinks'][j], bias, layer=j,
                bb=bb, tm=tm, mask_start=prompt)
            new_k.append(k_rows.reshape(B, -1, N_KV_HEADS, HEAD_DIM))
            new_v.append(v_rows.reshape(B, -1, N_KV_HEADS, HEAD_DIM))
        elif kind == 1:
            st = jnp.zeros((B, CONV_HIST, D_MODEL), F32) if prompt else caches[2][j]
            x, s = _conv_layer(
                x, st, gains[0], gains[1], p['conv_w_pw1'][j], p['conv_b_pw1'][j][None, :],
                p['conv_w_dw'][j], p['conv_b_dw'][j][None, :], p['conv_ln_g'][j][None, :],
                p['conv_ln_b'][j][None, :], p['conv_w_pw2'][j], p['conv_b_pw2'][j][None, :],
                bb=bb, tm=tm)
            new_conv.append(s)
        else:
            res = _cmlp_layer(
                x, gains[0], gains[1], p['cmlp_w_in'][j], p['cmlp_b_in'][j][None, :],
                p['cmlp_ln_g'][j][None, :], p['cmlp_ln_b'][j][None, :], p['cmlp_w_s'][j],
                p['cmlp_b_s'][j], p['cmlp_w_out'][j], p['cmlp_b_out'][j][None, :],
                bb=bb, tm=tm, emit_v=not prompt)
            x = res[0]
            if not prompt:
                new_cv.append(res[1])
        st = jnp.zeros((B, FFN_HIST, 2 * D_FF), F32) if prompt else caches[3][i]
        x, s = _ffn_layer(x, st, gains[2], gains[3], p['ffn_w_up'], p['ffn_w_dw'][i],
                          p['ffn_b_dw'][i][None, :], p['ffn_w_down'], layer=i, bb=bb, tm=tm)
        new_ffn.append(s)
    cv = jnp.stack(new_cv) if new_cv else None
    return x, jnp.stack(new_k), jnp.stack(new_v), jnp.stack(new_conv), cv, jnp.stack(new_ffn)


def kernel(x_prompt, x_sample, cache_attn_k, cache_attn_v, state_conv, state_ffn_conv, rel_bias_table, norm_gain, attn_w_qkv, attn_b_qkv, attn_w_o, attn_b_o, attn_sinks, conv_w_pw1, conv_b_pw1, conv_w_dw, conv_b_dw, conv_ln_g, conv_ln_b, conv_w_pw2, conv_b_pw2, cmlp_w_in, cmlp_b_in, cmlp_ln_g, cmlp_ln_b, cmlp_w_s, cmlp_b_s, cmlp_w_out, cmlp_b_out, ffn_w_up, ffn_w_dw, ffn_b_dw, ffn_w_down):
    p = {
        'norm_gain': norm_gain,
        'attn_w_qkv': attn_w_qkv.astype(BF16), 'attn_b_qkv': attn_b_qkv,
        'attn_w_o': attn_w_o.astype(BF16), 'attn_b_o': attn_b_o, 'attn_sinks': attn_sinks,
        'conv_w_pw1': conv_w_pw1.astype(BF16), 'conv_b_pw1': conv_b_pw1, 'conv_w_dw': conv_w_dw,
        'conv_b_dw': conv_b_dw, 'conv_ln_g': conv_ln_g, 'conv_ln_b': conv_ln_b,
        'conv_w_pw2': conv_w_pw2.astype(BF16), 'conv_b_pw2': conv_b_pw2,
        'cmlp_w_in': cmlp_w_in.astype(BF16), 'cmlp_b_in': cmlp_b_in, 'cmlp_ln_g': cmlp_ln_g,
        'cmlp_ln_b': cmlp_ln_b, 'cmlp_w_s': cmlp_w_s, 'cmlp_b_s': cmlp_b_s,
        'cmlp_w_out': cmlp_w_out.astype(BF16), 'cmlp_b_out': cmlp_b_out,
        'ffn_w_up': ffn_w_up.astype(BF16), 'ffn_w_dw': ffn_w_dw, 'ffn_b_dw': ffn_b_dw,
        'ffn_w_down': ffn_w_down.astype(BF16),
    }
    bias = _rel_bias(rel_bias_table)
    y_prompt, p_attn_k, p_attn_v, p_conv, _, p_ffn_conv = _trunk(
        x_prompt, None, p, bias, bb=1, tm=512)
    y_sample, s_attn_k, s_attn_v, s_conv, s_cmlp_v, s_ffn_conv = _trunk(
        x_sample, (cache_attn_k, cache_attn_v, state_conv, state_ffn_conv), p, bias, bb=8, tm=64)
    return (y_prompt, y_sample, p_attn_k, p_attn_v, p_conv, p_ffn_conv,
            s_attn_k, s_attn_v, s_conv, s_cmlp_v, s_ffn_conv)
```

```python
import functools
import math

import jax
import jax.numpy as jnp
from jax import lax
from jax.experimental import pallas as pl
from jax.experimental.pallas import tpu as pltpu

D_MODEL = 1024
DEPTH = 4
CHUNK = 64
N_MIXERS = 3
N_HEADS = 16
N_KV_HEADS = 4
HEAD_DIM = 64
GQA_GROUP = N_HEADS // N_KV_HEADS
WINDOW = 128
KV_DIM = N_KV_HEADS * HEAD_DIM
Q_DIM = N_HEADS * HEAD_DIM
NUM_BUCKETS = 32
MAX_DISTANCE = 128
CONV_WIDTH = 31
CONV_HIST = CONV_WIDTH - 1
MIX_BLOCK = 128
CMLP_GROUPS = 4
CMLP_DIM = 2 * D_MODEL
CMLP_GROUP_DIM = CMLP_DIM // CMLP_GROUPS
D_FF = 2816
FFN_CONV_WIDTH = 3
FFN_HIST = FFN_CONV_WIDTH - 1
RMS_EPS = 1e-6
_GELU_C0 = math.sqrt(2.0 / math.pi)
_GELU_C1 = _GELU_C0 * 0.044715
LN_EPS = 1e-5

F32 = jnp.float32
BF16 = jnp.bfloat16

V7X_VMEM_BYTES = 64 * 1024 * 1024
VMEM_LIMIT_BYTES = V7X_VMEM_BYTES - 8 * 1024 * 1024
SUBLANES = 8
FFN_CHUNK = 256
FFN_DOWN_GROUPS = (4, 4, 3)
FFN_ROW_SPLIT = 1
FFN_PAD = SUBLANES
CONV_PAD = 32
CONV_ROW_TILE = 64
CONV_LANE_TILE = 128
CONV_ROW_BLOCK = 256


def _rms(x, g):
    ms = jnp.mean(x * x, axis=-1, keepdims=True)
    return x * lax.rsqrt(ms + RMS_EPS) * g


def _layernorm(x, g, b):
    mu = jnp.mean(x, axis=-1, keepdims=True)
    xc = x - mu
    var = jnp.mean(xc * xc, axis=-1, keepdims=True)
    return xc * lax.rsqrt(var + LN_EPS) * g + b


def _gelu(x):
    hx = 0.5 * x
    return hx + hx * jnp.tanh(x * (_GELU_C0 + _GELU_C1 * (x * x)))


def _dot(a, b):
    return jnp.dot(a, b, preferred_element_type=F32)


def _const_spec(shape):
    zeros = (0,) * len(shape)
    return pl.BlockSpec(shape, lambda b, t: zeros, pipeline_mode=pl.Buffered(1))


def _layer_spec(stacked, layer):
    zeros = (0,) * (stacked.ndim - 1)
    return pl.BlockSpec((None,) + stacked.shape[1:], lambda b, t: (layer,) + zeros,
                        pipeline_mode=pl.Buffered(1))


def _seq_spec(bb, rows, width):
    return pl.BlockSpec((bb, rows, width), lambda b, t: (b, t, 0))


def _state_spec(bb, rows, width):
    return pl.BlockSpec((bb, rows, width), lambda b, t: (b, 0, 0))


def _params(flags=None):
    return pltpu.CompilerParams(
        dimension_semantics=("arbitrary", "arbitrary"),
        vmem_limit_bytes=VMEM_LIMIT_BYTES,
        flags=flags,
    )


HEAD_PAIR = 2 * HEAD_DIM
ATTN_LOOKAHEAD = 2
ATTN_CHUNKS_PER_UNIT = 2
ATTN_ROW_SPLIT = 2
PAIR_STEP = GQA_GROUP // 2


def _pair_heads(j, v):
    return GQA_GROUP * j + v, GQA_GROUP * j + PAIR_STEP + v


def _bias_body(bucket_ref, table_ref, o_ref):
    bk = bucket_ref[...]
    low = lax.broadcasted_iota(jnp.int32, (1, HEAD_PAIR), 1) < CHUNK
    for j in range(N_KV_HEADS):
        for v in range(2):
            ha, hb = _pair_heads(j, v)
            acc = jnp.zeros(bk.shape, F32)
            for b in range(NUM_BUCKETS):
                val = jnp.where(low, table_ref[b, ha], table_ref[b, hb])
                acc = jnp.where(bk == b, val, acc)
            o_ref[j, v] = acc


def _t5_bucket(rel):
    half = NUM_BUCKETS // 2
    max_exact = half // 2
    n = jnp.abs(rel)
    log_ratio = jnp.log(jnp.maximum(n, 1).astype(F32) / max_exact) / math.log(MAX_DISTANCE / max_exact)
    large = jnp.minimum(max_exact + (log_ratio * (half - max_exact)).astype(jnp.int32), half - 1)
    return jnp.where(rel > 0, half, 0) + jnp.where(n < max_exact, n, large)


def _rel_bias(table):
    q_pos = jnp.arange(CHUNK, dtype=jnp.int32)
    k_pos = jnp.arange(WINDOW + CHUNK, dtype=jnp.int32) - WINDOW
    bucket = _t5_bucket(k_pos[:, None] - q_pos[None, :]).astype(jnp.int32)
    bucket = jnp.concatenate([bucket, bucket], axis=1)
    return pl.pallas_call(
        _bias_body,
        out_shape=jax.ShapeDtypeStruct((N_KV_HEADS, 2, WINDOW + CHUNK, HEAD_PAIR), F32),
        in_specs=[pl.BlockSpec(memory_space=pltpu.VMEM), pl.BlockSpec(memory_space=pltpu.SMEM)],
        out_specs=pl.BlockSpec(memory_space=pltpu.VMEM),
        name="rel_bias",
    )(bucket, table)


def _attn_body(x_ref, kc_ref, vc_ref, gin_ref, gout_ref, wqkv_ref, bqkv_ref, wo_ref, bo_ref,
               sinks_ref, bias_ref, o_ref, kout_ref, vout_ref, qbuf, kzbuf, vzbuf, obuf, ybuf,
               *, bb, tm, nt, mask_start):
    t = pl.program_id(1)
    rows = bb * tm
    kv_rows = min(WINDOW, tm)
    span = WINDOW + CHUNK
    low = lax.broadcasted_iota(jnp.int32, (1, HEAD_PAIR), 1) < CHUNK

    def expand(buf, sl, r0, kv):
        nb, n = kv.shape[0], kv.shape[1]
        for pair in range(N_KV_HEADS // 2):
            tile = kv[:, :, pair * HEAD_PAIR:(pair + 1) * HEAD_PAIR]
            swapped = pltpu.roll(tile.reshape(nb * n, HEAD_PAIR), HEAD_DIM, 1).reshape(nb, n, HEAD_PAIR)
            for j, (lo_src, hi_src) in ((2 * pair, (tile, swapped)), (2 * pair + 1, (swapped, tile))):
                buf[sl, j, 0, r0:r0 + n, :] = jnp.where(low, lo_src, 0.0).astype(BF16)
                buf[sl, j, 1, r0:r0 + n, :] = jnp.where(low, 0.0, hi_src).astype(BF16)

    @pl.when(t == 0)
    def _():
        expand(kzbuf, slice(0, bb), 0, kc_ref[...])
        expand(vzbuf, slice(0, bb), 0, vc_ref[...])

    if bb >= ATTN_ROW_SPLIT:
        sub_b, sub_t = bb // ATTN_ROW_SPLIT, tm
        subs = [(slice(i * sub_b, (i + 1) * sub_b), 0) for i in range(ATTN_ROW_SPLIT)]
    else:
        sub_b, sub_t = bb, tm // ATTN_ROW_SPLIT
        subs = [(slice(0, bb), i * sub_t) for i in range(ATTN_ROW_SPLIT)]
    sub_rows = sub_b * sub_t
    assert sub_t % CHUNK == 0 and (sub_b > 1 or sub_t >= kv_rows or ATTN_ROW_SPLIT == 1)
    n_piece = GQA_GROUP * HEAD_DIM

    def flat0(sub):
        sl, r0 = sub
        return sl.start * tm + r0

    def load_x(sub):
        sl, r0 = sub
        return x_ref[sl, r0:r0 + sub_t, :].reshape(sub_rows, D_MODEL)

    def in_pieces(sub):
        sl, r0 = sub
        h = _rms(load_x(sub), gin_ref[...]).astype(BF16)
        f0 = flat0(sub)

        def kv_piece(col, buf, out_ref):
            def run():
                kv = (_dot(h, wqkv_ref[:, col:col + KV_DIM]) + bqkv_ref[:, col:col + KV_DIM]).reshape(
                    sub_b, sub_t, KV_DIM)
                expand(buf, sl, WINDOW + r0, kv)
                if sub_b > 1 or r0 + sub_t == tm:
                    out_ref[sl] = kv[:, sub_t - kv_rows:, :]
            return run

        def q_piece(j):
            def run():
                col = j * n_piece
                q = _dot(h, wqkv_ref[:, col:col + n_piece]) + bqkv_ref[:, col:col + n_piece]
                qbuf[f0:f0 + sub_rows, col:col + n_piece] = (q * (HEAD_DIM ** -0.5)).astype(BF16)
            return run

        return ([kv_piece(Q_DIM, kzbuf, kout_ref), kv_piece(Q_DIM + KV_DIM, vzbuf, vout_ref)]
                + [q_piece(j) for j in range(N_KV_HEADS)])

    def out_pieces(sub):
        sl, r0 = sub
        f0 = flat0(sub)

        def piece(n):
            def run():
                col = n * n_piece
                ybuf[f0:f0 + sub_rows, col:col + n_piece] = (
                    _dot(obuf[f0:f0 + sub_rows, :], wo_ref[:, col:col + n_piece]) + bo_ref[:, col:col + n_piece])
            return run

        def finish():
            y = ybuf[f0:f0 + sub_rows, :]
            o_ref[sl, r0:r0 + sub_t, :] = (load_x(sub) + _rms(y, gout_ref[...])).reshape(sub_b, sub_t, D_MODEL)

        return [piece(n) for n in range(D_MODEL // n_piece)] + [finish]

    key_row = lax.broadcasted_iota(jnp.int32, (span, HEAD_PAIR), 0)
    first_valid = jnp.where(t == 0, WINDOW, 0)
    contract_last = (((1,), (1,)), ((), ()))
    contract_first = (((0,), (0,)), ((), ()))

    def scores(s, chunks, j):
        g0 = j * GQA_GROUP * HEAD_DIM
        r0 = chunks[0] * CHUNK
        n_keys = span + (len(chunks) - 1) * CHUNK
        q_parts = []
        for c in chunks:
            q_rows = slice(s * tm + c * CHUNK, s * tm + (c + 1) * CHUNK)
            q_parts += [qbuf[q_rows, g0:g0 + HEAD_PAIR], qbuf[q_rows, g0 + HEAD_PAIR:g0 + 2 * HEAD_PAIR]]
        keys = jnp.concatenate([kzbuf[s, j, v_idx, r0:r0 + n_keys, :] for v_idx in range(2)], axis=0)
        sc_all = lax.dot_general(keys, jnp.concatenate(q_parts, axis=0), contract_last,
                                 preferred_element_type=F32)
        out = []
        for n, c in enumerate(chunks):
            tiles = []
            for v_idx in range(2):
                k0 = v_idx * n_keys + n * CHUNK
                sc = sc_all[k0:k0 + span, n * HEAD_PAIR:(n + 1) * HEAD_PAIR] + bias_ref[j, v_idx]
                if mask_start and c * CHUNK < WINDOW:
                    sc = jnp.where(key_row + c * CHUNK >= first_valid, sc, -jnp.inf)
                tiles.append(sc)
            out.append((c, tiles))
        return out

    def attend(s, c, j, scs):
        r0 = c * CHUNK
        q_rows = slice(s * tm + r0, s * tm + r0 + CHUNK)
        g0 = j * GQA_GROUP * HEAD_DIM
        probs = []
        for v_idx, sc in enumerate(scs):
            ha, hb = _pair_heads(j, v_idx)
            sink = jnp.where(low, sinks_ref[ha], sinks_ref[hb])
            m = jnp.maximum(jnp.max(sc, axis=0, keepdims=True), sink)
            p = jnp.exp(sc - m)
            denom = jnp.sum(p, axis=0, keepdims=True) + jnp.exp(sink - m)
            probs.append((p * (1.0 / denom)).astype(BF16))
        p_kn = jnp.concatenate(probs, axis=0)
        v_kd = jnp.concatenate([vzbuf[s, j, 0, r0:r0 + span, :],
                                vzbuf[s, j, 1, r0:r0 + span, :]], axis=0)
        o = lax.dot_general(p_kn, v_kd, contract_first, preferred_element_type=F32)
        obuf[q_rows, g0:g0 + HEAD_PAIR] = o[:CHUNK].astype(BF16)
        obuf[q_rows, g0 + HEAD_PAIR:g0 + 2 * HEAD_PAIR] = o[CHUNK:].astype(BF16)

    def units_of(sub):
        sl, r0 = sub
        c0, c1 = r0 // CHUNK, (r0 + sub_t) // CHUNK
        step = ATTN_CHUNKS_PER_UNIT if (c1 - c0) % ATTN_CHUNKS_PER_UNIT == 0 else 1
        return [(s, tuple(range(c, c + step)), j) for s in range(sl.start, sl.stop)
                for c in range(c0, c1, step) for j in range(N_KV_HEADS)]

    def run_units(units, fillers):
        every = max(1, len(units) // max(1, len(fillers)))
        pending = []

        def retire():
            (s, _, j), per_chunk = pending.pop(0)
            for c, tiles in per_chunk:
                attend(s, c, j, tiles)

        for n, unit in enumerate(units):
            pending.append((unit, scores(*unit)))
            if len(pending) > ATTN_LOOKAHEAD:
                retire()
            if fillers and (n + 1) % every == 0:
                fillers.pop(0)()
        while pending:
            retire()
        for filler in fillers:
            filler()

    for piece in in_pieces(subs[0]):
        piece()
    for i, sub in enumerate(subs):
        fillers = []
        if i > 0:
            fillers += out_pieces(subs[i - 1])
        if i + 1 < len(subs):
            fillers += in_pieces(subs[i + 1])
        run_units(units_of(sub), fillers)
    for piece in out_pieces(subs[-1]):
        piece()

    if nt > 1:
        kzbuf[:, :, :, 0:WINDOW, :] = kzbuf[:, :, :, tm:tm + WINDOW, :]
        vzbuf[:, :, :, 0:WINDOW, :] = vzbuf[:, :, :, tm:tm + WINDOW, :]


def _attn_layer(x, kc, vc, gin, gout, wqkv, bqkv, wo, bo, sinks, bias, *, layer, bb, tm, mask_start):
    B, T, _ = x.shape
    nt = T // tm
    kv_rows = min(WINDOW, tm)
    body = functools.partial(_attn_body, bb=bb, tm=tm, nt=nt, mask_start=mask_start)
    return pl.pallas_call(
        body,
        grid=(B // bb, nt),
        out_shape=(jax.ShapeDtypeStruct(x.shape, F32),
                   jax.ShapeDtypeStruct((B, kv_rows, KV_DIM), F32),
                   jax.ShapeDtypeStruct((B, kv_rows, KV_DIM), F32)),
        in_specs=[
            _seq_spec(bb, tm, D_MODEL),
            _state_spec(bb, WINDOW, KV_DIM),
            _state_spec(bb, WINDOW, KV_DIM),
            _const_spec((1, D_MODEL)),
            _const_spec((1, D_MODEL)),
            _layer_spec(wqkv, layer),
            _const_spec(bqkv.shape),
            _layer_spec(wo, layer),
            _const_spec(bo.shape),
            pl.BlockSpec(memory_space=pltpu.SMEM),
            _const_spec(bias.shape),
        ],
        out_specs=(_seq_spec(bb, tm, D_MODEL),
                   _state_spec(bb, kv_rows, KV_DIM),
                   _state_spec(bb, kv_rows, KV_DIM)),
        scratch_shapes=[
            pltpu.VMEM((bb * tm, Q_DIM), BF16),
            pltpu.VMEM((bb, N_KV_HEADS, 2, WINDOW + tm, HEAD_PAIR), BF16),
            pltpu.VMEM((bb, N_KV_HEADS, 2, WINDOW + tm, HEAD_PAIR), BF16),
            pltpu.VMEM((bb * tm, Q_DIM), BF16),
            pltpu.VMEM((bb * tm, D_MODEL), F32),
        ],
        compiler_params=_params(),
        name="attn_mixer",
    )(x, kc, vc, gin, gout, wqkv, bqkv, wo, bo, sinks, bias)


def _conv_body(x_ref, st_ref, gin_ref, gout_ref, w1_ref, b1_ref, wdw_ref, bdw_ref, lng_ref, lnb_ref,
               w2_ref, b2_ref, o_ref, nst_ref, gbuf, ybuf, cbuf, *, bb, tm, nt):
    t = pl.program_id(1)
    rows = bb * tm
    hist0 = CONV_PAD - CONV_HIST

    @pl.when(t == 0)
    def _():
        gbuf[:, hist0:CONV_PAD, :] = st_ref[...]

    row = lax.broadcasted_iota(jnp.int32, (SUBLANES, CONV_LANE_TILE), 0)
    n_tiles = CONV_ROW_TILE // SUBLANES
    max_a = (hist0 + CONV_WIDTH - 1) // SUBLANES
    tiles_per_block = CONV_ROW_BLOCK // CONV_ROW_TILE

    def row_tiles(rb):
        out = []
        for i in range(tiles_per_block):
            flat = rb * CONV_ROW_BLOCK + i * CONV_ROW_TILE
            out.append((flat // tm, flat % tm, flat))
        return out

    def block_rows(ref, rb):
        flat = rb * CONV_ROW_BLOCK
        if tm >= CONV_ROW_BLOCK:
            return ref.at[flat // tm, flat % tm:flat % tm + CONV_ROW_BLOCK, :]
        return ref.at[flat // tm:(flat + CONV_ROW_BLOCK) // tm, :, :]

    def pointwise_in(rb):
        x = block_rows(x_ref, rb)[...].reshape(CONV_ROW_BLOCK, D_MODEL)
        h = _rms(x, gin_ref[...]).astype(BF16)
        ag = _dot(h, w1_ref[...]) + b1_ref[...]
        glu = ag[:, :D_MODEL] * jax.nn.sigmoid(ag[:, D_MODEL:])
        for i, (s, r0, _) in enumerate(row_tiles(rb)):
            gbuf[s, CONV_PAD + r0:CONV_PAD + r0 + CONV_ROW_TILE, :] = (
                glu[i * CONV_ROW_TILE:(i + 1) * CONV_ROW_TILE])

    def depthwise(rb):
        for l0 in range(0, D_MODEL, CONV_LANE_TILE):
            lanes = slice(l0, l0 + CONV_LANE_TILE)
            carried, carried_for = {}, None
            for s, r0, flat in row_tiles(rb):
                if carried_for != (s, r0):
                    carried = {}
                tiles = [gbuf[s, r0 + SUBLANES * j:r0 + SUBLANES * (j + 1), lanes]
                         for j in range(n_tiles + max_a)]
                out = [None] * n_tiles
                for b in range(SUBLANES):
                    taps = [a for a in range(max_a + 1)
                            if 0 <= SUBLANES * a + b - hist0 < CONV_WIDTH]
                    wts = [wdw_ref[SUBLANES * a + b - hist0:SUBLANES * a + b - hist0 + 1, lanes]
                           for a in taps]
                    z = []
                    for m in range(n_tiles + (1 if b else 0)):
                        if m == 0 and b in carried:
                            z.append(carried[b])
                            continue
                        acc = None
                        for a, wt in zip(taps, wts):
                            term = tiles[m + a] * wt
                            acc = term if acc is None else acc + term
                        z.append(acc)
                    if b:
                        carried[b] = z[n_tiles]
                    for m in range(n_tiles):
                        if b == 0:
                            part = z[m]
                        else:
                            part = pltpu.roll(jnp.where(row >= b, z[m], z[m + 1]), SUBLANES - b, 0)
                        out[m] = part if out[m] is None else out[m] + part
                carried_for = (s, r0 + CONV_ROW_TILE)
                for m in range(n_tiles):
                    lo = flat + SUBLANES * m
                    ybuf[lo:lo + SUBLANES, lanes] = out[m]
        for s, r0, flat in row_tiles(rb):
            rs = slice(flat, flat + CONV_ROW_TILE)
            y = _layernorm(ybuf[rs, :] + bdw_ref[...], lng_ref[...], lnb_ref[...])
            cbuf[rs, :] = (y * jax.nn.sigmoid(y)).astype(BF16)

    def pointwise_out(rb):
        flat = rb * CONV_ROW_BLOCK
        y = _dot(cbuf[flat:flat + CONV_ROW_BLOCK, :], w2_ref[...]) + b2_ref[...]
        x = block_rows(x_ref, rb)[...]
        block_rows(o_ref, rb)[...] = x + _rms(y, gout_ref[...]).reshape(x.shape)

    n_blocks = rows // CONV_ROW_BLOCK
    pointwise_in(0)
    for rb in range(1, n_blocks):
        depthwise(rb - 1)
        pointwise_in(rb)
        pointwise_out(rb - 1)
    depthwise(n_blocks - 1)
    pointwise_out(n_blocks - 1)

    last = gbuf[:, CONV_PAD + tm - CONV_HIST:CONV_PAD + tm, :]

    @pl.when(t == nt - 1)
    def _():
        nst_ref[...] = last

    if nt > 1:
        gbuf[:, hist0:CONV_PAD, :] = last


def _conv_layer(x, st, gin, gout, w1, b1, wdw, bdw, lng, lnb, w2, b2, *, bb, tm):
    B, T, _ = x.shape
    nt = T // tm
    assert tm >= CONV_HIST and tm % CONV_ROW_TILE == 0
    body = functools.partial(_conv_body, bb=bb, tm=tm, nt=nt)
    return pl.pallas_call(
        body,
        grid=(B // bb, nt),
        out_shape=(jax.ShapeDtypeStruct(x.shape, F32),
                   jax.ShapeDtypeStruct((B, CONV_HIST, D_MODEL), F32)),
        in_specs=[
            _seq_spec(bb, tm, D_MODEL),
            _state_spec(bb, CONV_HIST, D_MODEL),
            _const_spec((1, D_MODEL)),
            _const_spec((1, D_MODEL)),
            _const_spec(w1.shape),
            _const_spec(b1.shape),
            _const_spec(wdw.shape),
            _const_spec(bdw.shape),
            _const_spec(lng.shape),
            _const_spec(lnb.shape),
            _const_spec(w2.shape),
            _const_spec(b2.shape),
        ],
        out_specs=(_seq_spec(bb, tm, D_MODEL), _state_spec(bb, CONV_HIST, D_MODEL)),
        scratch_shapes=[
            pltpu.VMEM((bb, CONV_PAD + tm, D_MODEL), F32),
            pltpu.VMEM((bb * tm, D_MODEL), F32),
            pltpu.VMEM((bb * tm, D_MODEL), BF16),
        ],
        compiler_params=_params(),
        name="conv_mixer",
    )(x, st, gin, gout, w1, b1, wdw, bdw, lng, lnb, w2, b2)


def _cmlp_body(x_ref, gin_ref, gout_ref, win_ref, bin_ref, lng_ref, lnb_ref, ws_ref, bs_ref,
               wout_ref, bout_ref, *rest, bb, tm, blk, emit_v):
    if emit_v:
        o_ref, v_ref, vbuf, ybuf = rest
    else:
        o_ref, vbuf, ybuf = rest
    rows = bb * tm
    x = x_ref[...].reshape(rows, D_MODEL)
    h = _rms(x, gin_ref[...]).astype(BF16)
    hv = _gelu(_dot(h, win_ref[:, CMLP_DIM:]) + bin_ref[:, CMLP_DIM:])
    vln = _layernorm(hv, lng_ref[...], lnb_ref[...])
    if emit_v:
        v_ref[...] = vln.reshape(bb, tm, CMLP_DIM)
    vbuf[...] = vln.astype(BF16)

    pos_r = lax.broadcasted_iota(jnp.int32, (blk, blk), 0) // CHUNK
    pos_c = lax.broadcasted_iota(jnp.int32, (blk, blk), 1) // CHUNK
    for g in range(CMLP_GROUPS):
        lanes = slice(g * CMLP_GROUP_DIM, (g + 1) * CMLP_GROUP_DIM)
        ug = _gelu(_dot(h, win_ref[:, lanes]) + bin_ref[:, lanes])
        wsm = jnp.where(pos_c <= pos_r, ws_ref[g], 0.0).astype(BF16)
        for n in range(rows // blk):
            rs = slice(n * blk, (n + 1) * blk)
            gate = _dot(wsm, vbuf[rs, lanes]) + bs_ref[g]
            ybuf[rs, lanes] = (ug[rs] * gate).astype(BF16)

    y = _dot(ybuf[...], wout_ref[...]) + bout_ref[...]
    o_ref[...] = (x + _rms(y, gout_ref[...])).reshape(bb, tm, D_MODEL)


def _cmlp_layer(x, gin, gout, win, bin_, lng, lnb, ws, bs, wout, bout, *, bb, tm, emit_v):
    B, T, _ = x.shape
    blk = min(T, MIX_BLOCK)
    assert tm % blk == 0
    ws = ws[:, :blk, :blk]
    bs = bs[:, :blk, None]
    body = functools.partial(_cmlp_body, bb=bb, tm=tm, blk=blk, emit_v=emit_v)
    out_shape = [jax.ShapeDtypeStruct(x.shape, F32)]
    out_specs = [_seq_spec(bb, tm, D_MODEL)]
    if emit_v:
        out_shape.append(jax.ShapeDtypeStruct((B, T, CMLP_DIM), F32))
        out_specs.append(_seq_spec(bb, tm, CMLP_DIM))
    return pl.pallas_call(
        body,
        grid=(B // bb, T // tm),
        out_shape=tuple(out_shape),
        in_specs=[
            _seq_spec(bb, tm, D_MODEL),
            _const_spec((1, D_MODEL)),
            _const_spec((1, D_MODEL)),
            _const_spec(win.shape),
            _const_spec(bin_.shape),
            _const_spec(lng.shape),
            _const_spec(lnb.shape),
            _const_spec(ws.shape),
            _const_spec(bs.shape),
            _const_spec(wout.shape),
            _const_spec(bout.shape),
        ],
        out_specs=tuple(out_specs),
        scratch_shapes=[
            pltpu.VMEM((bb * tm, CMLP_DIM), BF16),
            pltpu.VMEM((bb * tm, CMLP_DIM), BF16),
        ],
        compiler_params=_params(),
        name="cmlp_mixer",
    )(x, gin, gout, win, bin_, lng, lnb, ws, bs, wout, bout)


def _ffn_body(x_ref, st_ref, gin_ref, gout_ref, wup_ref, wdw_ref, bdw_ref, wdn_ref,
              o_ref, nst_ref, *scratch, bb, tm, nt):
    n_chunks = D_FF // FFN_CHUNK
    upbufs = scratch[:n_chunks]
    actbufs = scratch[n_chunks:]
    t = pl.program_id(1)
    rows = bb * tm
    hist0 = FFN_PAD - FFN_HIST

    def halves(c):
        return ((c * FFN_CHUNK, slice(0, FFN_CHUNK)),
                (D_FF + c * FFN_CHUNK, slice(FFN_CHUNK, 2 * FFN_CHUNK)))

    @pl.when(t == 0)
    def _():
        for c in range(n_chunks):
            upbufs[c][:, 0:hist0, :] = jnp.zeros((bb, hist0, 2 * FFN_CHUNK), F32)
            for lo, lanes in halves(c):
                upbufs[c][:, hist0:FFN_PAD, lanes] = st_ref[:, :, lo:lo + FFN_CHUNK]

    if bb >= FFN_ROW_SPLIT:
        sub_b, sub_t = bb // FFN_ROW_SPLIT, tm
        subs = [(slice(i * sub_b, (i + 1) * sub_b), 0) for i in range(FFN_ROW_SPLIT)]
    else:
        sub_b, sub_t = bb, tm // FFN_ROW_SPLIT
        subs = [(slice(0, bb), i * sub_t) for i in range(FFN_ROW_SPLIT)]
    sub_rows = sub_b * sub_t
    n_tiles = sub_t // SUBLANES
    tile_row = lax.broadcasted_iota(jnp.int32, (sub_b * n_tiles, SUBLANES, FFN_CHUNK), 1)

    def up_proj(h, sub, c):
        sl, r0 = sub
        for lo, lanes in halves(c):
            up = _dot(h, wup_ref[:, lo:lo + FFN_CHUNK])
            upbufs[c][sl, FFN_PAD + r0:FFN_PAD + r0 + sub_t, lanes] = up.reshape(sub_b, sub_t, FFN_CHUNK)

    def delayed(cur, hist):
        cur_r = pltpu.roll(cur, 1, 1)
        hist_r = pltpu.roll(hist, 1, 1)
        cur_r4 = cur_r.reshape(sub_b, n_tiles, SUBLANES, FFN_CHUNK)
        prev_r = jnp.concatenate([hist_r[:, None], cur_r4[:, :n_tiles - 1]], axis=1)
        out = jnp.where(tile_row < 1, prev_r.reshape(sub_b * n_tiles, SUBLANES, FFN_CHUNK), cur_r)
        return out, hist_r

    def conv(sub, c, lo, lanes):
        sl, r0 = sub
        w = wdw_ref[:, lo:lo + FFN_CHUNK]
        cur = upbufs[c][sl, FFN_PAD + r0:FFN_PAD + r0 + sub_t, lanes].reshape(
            sub_b * n_tiles, SUBLANES, FFN_CHUNK)
        hist = upbufs[c][sl, r0:r0 + FFN_PAD, lanes]
        acc, acc_hist = None, None
        for kk in range(FFN_HIST):
            term, term_hist = cur * w[kk:kk + 1], hist * w[kk:kk + 1]
            if acc is not None:
                term, term_hist = term + acc, term_hist + acc_hist
            acc, acc_hist = delayed(term, term_hist)
        out = cur * w[FFN_HIST:FFN_HIST + 1] + bdw_ref[:, lo:lo + FFN_CHUNK] + acc
        return out.reshape(sub_rows, FFN_CHUNK)

    def gate(sub, c, actbuf, col):
        (glo, glanes), (ulo, ulanes) = halves(c)
        act = _gelu(conv(sub, c, glo, glanes)) * conv(sub, c, ulo, ulanes)
        actbuf[:, col:col + FFN_CHUNK] = act.astype(BF16)

    groups, start = [], 0
    for size in FFN_DOWN_GROUPS:
        groups.append((start, start + size))
        start += size
    xs = []
    for si, sub in enumerate(subs):
        sl, r0 = sub
        x = x_ref[sl, r0:r0 + sub_t, :].reshape(sub_rows, D_MODEL)
        xs.append(x)
        h = _rms(x, gin_ref[...]).astype(BF16)
        for gi, (c0, c1) in enumerate(groups):
            for c in range(c0, c1):
                up_proj(h, sub, c)
                gate(sub, c, actbufs[si * len(groups) + gi], (c - c0) * FFN_CHUNK)
    for si, sub in enumerate(subs):
        sl, r0 = sub
        y = None
        for gi, (c0, c1) in enumerate(groups):
            part = _dot(actbufs[si * len(groups) + gi][...], wdn_ref[c0 * FFN_CHUNK:c1 * FFN_CHUNK, :])
            y = part if y is None else y + part
        o_ref[sl, r0:r0 + sub_t, :] = (xs[si] + _rms(y, gout_ref[...])).reshape(sub_b, sub_t, D_MODEL)

    for c in range(n_chunks):
        for lo, lanes in halves(c):
            last = upbufs[c][:, FFN_PAD + tm - FFN_HIST:FFN_PAD + tm, lanes]

            @pl.when(t == nt - 1)
            def _(last=last, lo=lo):
                nst_ref[:, :, lo:lo + FFN_CHUNK] = last

            if nt > 1:
                upbufs[c][:, hist0:FFN_PAD, lanes] = last


def _ffn_layer(x, st, gin, gout, wup, wdw, bdw, wdn, *, layer, bb, tm):
    B, T, _ = x.shape
    nt = T // tm
    assert tm >= FFN_HIST and sum(FFN_DOWN_GROUPS) * FFN_CHUNK == D_FF
    body = functools.partial(_ffn_body, bb=bb, tm=tm, nt=nt)
    return pl.pallas_call(
        body,
        grid=(B // bb, nt),
        out_shape=(jax.ShapeDtypeStruct(x.shape, F32),
                   jax.ShapeDtypeStruct((B, FFN_HIST, 2 * D_FF), F32)),
        in_specs=[
            _seq_spec(bb, tm, D_MODEL),
            _state_spec(bb, FFN_HIST, 2 * D_FF),
            _const_spec((1, D_MODEL)),
            _const_spec((1, D_MODEL)),
            _layer_spec(wup, layer),
            _const_spec(wdw.shape),
            _const_spec(bdw.shape),
            _layer_spec(wdn, layer),
        ],
        out_specs=(_seq_spec(bb, tm, D_MODEL), _state_spec(bb, FFN_HIST, 2 * D_FF)),
        scratch_shapes=(
            [pltpu.VMEM((bb, FFN_PAD + tm, 2 * FFN_CHUNK), F32)] * (D_FF // FFN_CHUNK)
            + [pltpu.VMEM((bb * tm // FFN_ROW_SPLIT, size * FFN_CHUNK), BF16)
               for _ in range(FFN_ROW_SPLIT) for size in FFN_DOWN_GROUPS]),
        compiler_params=_params(),
        name="conv_ffn",
    )(x, st, gin, gout, wup, wdw, bdw, wdn)


def _trunk(x, caches, p, bias, *, bb, tm):
    prompt = caches is None
    B = x.shape[0]
    new_k, new_v, new_conv, new_cv, new_ffn = [], [], [], [], []
    for i in range(DEPTH):
        kind, j = i % N_MIXERS, i // N_MIXERS
        g = p['norm_gain'][i]
        gains = [g[n][None, :] for n in range(4)]
        if kind == 0:
            if prompt:
                kc = jnp.zeros((B, WINDOW, KV_DIM), F32)
                vc = kc
            else:
                kc = caches[0][j].reshape(B, WINDOW, KV_DIM)
                vc = caches[1][j].reshape(B, WINDOW, KV_DIM)
            x, k_rows, v_rows = _attn_layer(
                x, kc, vc, gains[0], gains[1], p['attn_w_qkv'], p['attn_b_qkv'][j][None, :],
                p['attn_w_o'], p['attn_b_o'][j][None, :], p['attn_sinks'][j], bias, layer=j,
                bb=bb, tm=tm, mask_start=prompt)
            new_k.append(k_rows.reshape(B, -1, N_KV_HEADS, HEAD_DIM))
            new_v.append(v_rows.reshape(B, -1, N_KV_HEADS, HEAD_DIM))
        elif kind == 1:
            st = jnp.zeros((B, CONV_HIST, D_MODEL), F32) if prompt else caches[2][j]
            x, s = _conv_layer(
                x, st, gains[0], gains[1], p['conv_w_pw1'][j], p['conv_b_pw1'][j][None, :],
                p['conv_w_dw'][j], p['conv_b_dw'][j][None, :], p['conv_ln_g'][j][None, :],
                p['conv_ln_b'][j][None, :], p['conv_w_pw2'][j], p['conv_b_pw2'][j][None, :],
                bb=bb, tm=tm)
            new_conv.append(s)
        else:
            res = _cmlp_layer(
                x, gains[0], gains[1], p['cmlp_w_in'][j], p['cmlp_b_in'][j][None, :],
                p['cmlp_ln_g'][j][None, :], p['cmlp_ln_b'][j][None, :], p['cmlp_w_s'][j],
                p['cmlp_b_s'][j], p['cmlp_w_out'][j], p['cmlp_b_out'][j][None, :],
                bb=bb, tm=tm, emit_v=not prompt)
            x = res[0]
            if not prompt:
                new_cv.append(res[1])
        st = jnp.zeros((B, FFN_HIST, 2 * D_FF), F32) if prompt else caches[3][i]
        x, s = _ffn_layer(x, st, gains[2], gains[3], p['ffn_w_up'], p['ffn_w_dw'][i],
                          p['ffn_b_dw'][i][None, :], p['ffn_w_down'], layer=i, bb=bb, tm=tm)
        new_ffn.append(s)
    cv = jnp.stack(new_cv) if new_cv else None
    return x, jnp.stack(new_k), jnp.stack(new_v), jnp.stack(new_conv), cv, jnp.stack(new_ffn)


def kernel(x_prompt, x_sample, cache_attn_k, cache_attn_v, state_conv, state_ffn_conv, rel_bias_table, norm_gain, attn_w_qkv, attn_b_qkv, attn_w_o, attn_b_o, attn_sinks, conv_w_pw1, conv_b_pw1, conv_w_dw, conv_b_dw, conv_ln_g, conv_ln_b, conv_w_pw2, conv_b_pw2, cmlp_w_in, cmlp_b_in, cmlp_ln_g, cmlp_ln_b, cmlp_w_s, cmlp_b_s, cmlp_w_out, cmlp_b_out, ffn_w_up, ffn_w_dw, ffn_b_dw, ffn_w_down):
    p = {
        'norm_gain': norm_gain,
        'attn_w_qkv': attn_w_qkv.astype(BF16), 'attn_b_qkv': attn_b_qkv,
        'attn_w_o': attn_w_o.astype(BF16), 'attn_b_o': attn_b_o, 'attn_sinks': attn_sinks,
        'conv_w_pw1': conv_w_pw1.astype(BF16), 'conv_b_pw1': conv_b_pw1, 'conv_w_dw': conv_w_dw,
        'conv_b_dw': conv_b_dw, 'conv_ln_g': conv_ln_g, 'conv_ln_b': conv_ln_b,
        'conv_w_pw2': conv_w_pw2.astype(BF16), 'conv_b_pw2': conv_b_pw2,
        'cmlp_w_in': cmlp_w_in.astype(BF16), 'cmlp_b_in': cmlp_b_in, 'cmlp_ln_g': cmlp_ln_g,
        'cmlp_ln_b': cmlp_ln_b, 'cmlp_w_s': cmlp_w_s, 'cmlp_b_s': cmlp_b_s,
        'cmlp_w_out': cmlp_w_out.astype(BF16), 'cmlp_b_out': cmlp_b_out,
        'ffn_w_up': ffn_w_up.astype(BF16), 'ffn_w_dw': ffn_w_dw, 'ffn_b_dw': ffn_b_dw,
        'ffn_w_down': ffn_w_down.astype(BF16),
    }
    bias = _rel_bias(rel_bias_table)
    y_prompt, p_attn_k, p_attn_v, p_conv, _, p_ffn_conv = _trunk(
        x_prompt, None, p, bias, bb=1, tm=512)
    y_sample, s_attn_k, s_attn_v, s_conv, s_cmlp_v, s_ffn_conv = _trunk(
        x_sample, (cache_attn_k, cache_attn_v, state_conv, state_ffn_conv), p, bias, bb=8, tm=64)
    return (y_prompt, y_sample, p_attn_k, p_attn_v, p_conv, p_ffn_conv,
            s_attn_k, s_attn_v, s_conv, s_cmlp_v, s_ffn_conv)
```

```python
import functools
import math

import jax
import jax.numpy as jnp
from jax import lax
from jax.experimental import pallas as pl
from jax.experimental.pallas import tpu as pltpu

D_MODEL = 1024
DEPTH = 4
CHUNK = 64
N_MIXERS = 3
N_HEADS = 16
N_KV_HEADS = 4
HEAD_DIM = 64
GQA_GROUP = N_HEADS // N_KV_HEADS
WINDOW = 128
KV_DIM = N_KV_HEADS * HEAD_DIM
Q_DIM = N_HEADS * HEAD_DIM
NUM_BUCKETS = 32
MAX_DISTANCE = 128
CONV_WIDTH = 31
CONV_HIST = CONV_WIDTH - 1
MIX_BLOCK = 128
CMLP_GROUPS = 4
CMLP_DIM = 2 * D_MODEL
CMLP_GROUP_DIM = CMLP_DIM // CMLP_GROUPS
D_FF = 2816
FFN_CONV_WIDTH = 3
FFN_HIST = FFN_CONV_WIDTH - 1
RMS_EPS = 1e-6
_GELU_C0 = math.sqrt(2.0 / math.pi)
_GELU_C1 = _GELU_C0 * 0.044715
LN_EPS = 1e-5

F32 = jnp.float32
BF16 = jnp.bfloat16

V7X_VMEM_BYTES = 64 * 1024 * 1024
VMEM_LIMIT_BYTES = V7X_VMEM_BYTES - 8 * 1024 * 1024
SUBLANES = 8
FFN_CHUNK = 256
FFN_DOWN_GROUPS = (4, 4, 3)
FFN_DOWN_ROW_PARTS = 2
FFN_ROW_SPLIT = 1
FFN_PAD = SUBLANES
CONV_PAD = 32
CONV_ROW_TILE = 64
CONV_LANE_TILE = 128
CONV_ROW_BLOCK = 256


def _rms(x, g):
    ms = jnp.mean(x * x, axis=-1, keepdims=True)
    return x * lax.rsqrt(ms + RMS_EPS) * g


def _layernorm(x, g, b):
    mu = jnp.mean(x, axis=-1, keepdims=True)
    xc = x - mu
    var = jnp.mean(xc * xc, axis=-1, keepdims=True)
    return xc * lax.rsqrt(var + LN_EPS) * g + b


def _gelu(x):
    hx = 0.5 * x
    return hx + hx * jnp.tanh(x * (_GELU_C0 + _GELU_C1 * (x * x)))


def _dot(a, b):
    return jnp.dot(a, b, preferred_element_type=F32)


def _const_spec(shape):
    zeros = (0,) * len(shape)
    return pl.BlockSpec(shape, lambda b, t: zeros, pipeline_mode=pl.Buffered(1))


def _layer_spec(stacked, layer):
    zeros = (0,) * (stacked.ndim - 1)
    return pl.BlockSpec((None,) + stacked.shape[1:], lambda b, t: (layer,) + zeros,
                        pipeline_mode=pl.Buffered(1))


def _seq_spec(bb, rows, width):
    return pl.BlockSpec((bb, rows, width), lambda b, t: (b, t, 0))


def _state_spec(bb, rows, width):
    return pl.BlockSpec((bb, rows, width), lambda b, t: (b, 0, 0))


def _params(flags=None):
    return pltpu.CompilerParams(
        dimension_semantics=("arbitrary", "arbitrary"),
        vmem_limit_bytes=VMEM_LIMIT_BYTES,
        flags=flags,
    )


HEAD_PAIR = 2 * HEAD_DIM
ATTN_LOOKAHEAD = 2
ATTN_CHUNKS_PER_UNIT = 2
ATTN_ROW_SPLIT = 2
PAIR_STEP = GQA_GROUP // 2


def _pair_heads(j, v):
    return GQA_GROUP * j + v, GQA_GROUP * j + PAIR_STEP + v


def _bias_body(bucket_ref, table_ref, o_ref):
    bk = bucket_ref[...]
    low = lax.broadcasted_iota(jnp.int32, (1, HEAD_PAIR), 1) < CHUNK
    for j in range(N_KV_HEADS):
        for v in range(2):
            ha, hb = _pair_heads(j, v)
            acc = jnp.zeros(bk.shape, F32)
            for b in range(NUM_BUCKETS):
                val = jnp.where(low, table_ref[b, ha], table_ref[b, hb])
                acc = jnp.where(bk == b, val, acc)
            o_ref[j, v] = acc


def _t5_bucket(rel):
    half = NUM_BUCKETS // 2
    max_exact = half // 2
    n = jnp.abs(rel)
    log_ratio = jnp.log(jnp.maximum(n, 1).astype(F32) / max_exact) / math.log(MAX_DISTANCE / max_exact)
    large = jnp.minimum(max_exact + (log_ratio * (half - max_exact)).astype(jnp.int32), half - 1)
    return jnp.where(rel > 0, half, 0) + jnp.where(n < max_exact, n, large)


def _rel_bias(table):
    q_pos = jnp.arange(CHUNK, dtype=jnp.int32)
    k_pos = jnp.arange(WINDOW + CHUNK, dtype=jnp.int32) - WINDOW
    bucket = _t5_bucket(k_pos[:, None] - q_pos[None, :]).astype(jnp.int32)
    bucket = jnp.concatenate([bucket, bucket], axis=1)
    return pl.pallas_call(
        _bias_body,
        out_shape=jax.ShapeDtypeStruct((N_KV_HEADS, 2, WINDOW + CHUNK, HEAD_PAIR), F32),
        in_specs=[pl.BlockSpec(memory_space=pltpu.VMEM), pl.BlockSpec(memory_space=pltpu.SMEM)],
        out_specs=pl.BlockSpec(memory_space=pltpu.VMEM),
        name="rel_bias",
    )(bucket, table)


def _attn_body(x_ref, kc_ref, vc_ref, gin_ref, gout_ref, wqkv_ref, bqkv_ref, wo_ref, bo_ref,
               sinks_ref, bias_ref, o_ref, kout_ref, vout_ref, qbuf, kzbuf, vzbuf, obuf, ybuf,
               *, bb, tm, nt, mask_start):
    t = pl.program_id(1)
    rows = bb * tm
    kv_rows = min(WINDOW, tm)
    span = WINDOW + CHUNK
    low = lax.broadcasted_iota(jnp.int32, (1, HEAD_PAIR), 1) < CHUNK

    def expand(buf, sl, r0, kv):
        nb, n = kv.shape[0], kv.shape[1]
        for pair in range(N_KV_HEADS // 2):
            tile = kv[:, :, pair * HEAD_PAIR:(pair + 1) * HEAD_PAIR]
            swapped = pltpu.roll(tile.reshape(nb * n, HEAD_PAIR), HEAD_DIM, 1).reshape(nb, n, HEAD_PAIR)
            for j, (lo_src, hi_src) in ((2 * pair, (tile, swapped)), (2 * pair + 1, (swapped, tile))):
                buf[sl, j, 0, r0:r0 + n, :] = jnp.where(low, lo_src, 0.0).astype(BF16)
                buf[sl, j, 1, r0:r0 + n, :] = jnp.where(low, 0.0, hi_src).astype(BF16)

    @pl.when(t == 0)
    def _():
        expand(kzbuf, slice(0, bb), 0, kc_ref[...])
        expand(vzbuf, slice(0, bb), 0, vc_ref[...])

    if bb >= ATTN_ROW_SPLIT:
        sub_b, sub_t = bb // ATTN_ROW_SPLIT, tm
        subs = [(slice(i * sub_b, (i + 1) * sub_b), 0) for i in range(ATTN_ROW_SPLIT)]
    else:
        sub_b, sub_t = bb, tm // ATTN_ROW_SPLIT
        subs = [(slice(0, bb), i * sub_t) for i in range(ATTN_ROW_SPLIT)]
    sub_rows = sub_b * sub_t
    assert sub_t % CHUNK == 0 and (sub_b > 1 or sub_t >= kv_rows or ATTN_ROW_SPLIT == 1)
    n_piece = GQA_GROUP * HEAD_DIM

    def flat0(sub):
        sl, r0 = sub
        return sl.start * tm + r0

    def load_x(sub):
        sl, r0 = sub
        return x_ref[sl, r0:r0 + sub_t, :].reshape(sub_rows, D_MODEL)

    def in_pieces(sub):
        sl, r0 = sub
        h = _rms(load_x(sub), gin_ref[...]).astype(BF16)
        f0 = flat0(sub)

        def kv_piece(col, buf, out_ref):
            def run():
                kv = (_dot(h, wqkv_ref[:, col:col + KV_DIM]) + bqkv_ref[:, col:col + KV_DIM]).reshape(
                    sub_b, sub_t, KV_DIM)
                expand(buf, sl, WINDOW + r0, kv)
                if sub_b > 1 or r0 + sub_t == tm:
                    out_ref[sl] = kv[:, sub_t - kv_rows:, :]
            return run

        def q_piece(j):
            def run():
                col = j * n_piece
                q = _dot(h, wqkv_ref[:, col:col + n_piece]) + bqkv_ref[:, col:col + n_piece]
                qbuf[f0:f0 + sub_rows, col:col + n_piece] = (q * (HEAD_DIM ** -0.5)).astype(BF16)
            return run

        return ([kv_piece(Q_DIM, kzbuf, kout_ref), kv_piece(Q_DIM + KV_DIM, vzbuf, vout_ref)]
                + [q_piece(j) for j in range(N_KV_HEADS)])

    def out_pieces(sub):
        sl, r0 = sub
        f0 = flat0(sub)

        def piece(n):
            def run():
                col = n * n_piece
                ybuf[f0:f0 + sub_rows, col:col + n_piece] = (
                    _dot(obuf[f0:f0 + sub_rows, :], wo_ref[:, col:col + n_piece]) + bo_ref[:, col:col + n_piece])
            return run

        def finish():
            y = ybuf[f0:f0 + sub_rows, :]
            o_ref[sl, r0:r0 + sub_t, :] = (load_x(sub) + _rms(y, gout_ref[...])).reshape(sub_b, sub_t, D_MODEL)

        return [piece(n) for n in range(D_MODEL // n_piece)] + [finish]

    key_row = lax.broadcasted_iota(jnp.int32, (span, HEAD_PAIR), 0)
    first_valid = jnp.where(t == 0, WINDOW, 0)
    contract_last = (((1,), (1,)), ((), ()))
    contract_first = (((0,), (0,)), ((), ()))

    def scores(s, chunks, j):
        g0 = j * GQA_GROUP * HEAD_DIM
        r0 = chunks[0] * CHUNK
        n_keys = span + (len(chunks) - 1) * CHUNK
        q_parts = []
        for c in chunks:
            q_rows = slice(s * tm + c * CHUNK, s * tm + (c + 1) * CHUNK)
            q_parts += [qbuf[q_rows, g0:g0 + HEAD_PAIR], qbuf[q_rows, g0 + HEAD_PAIR:g0 + 2 * HEAD_PAIR]]
        keys = jnp.concatenate([kzbuf[s, j, v_idx, r0:r0 + n_keys, :] for v_idx in range(2)], axis=0)
        sc_all = lax.dot_general(keys, jnp.concatenate(q_parts, axis=0), contract_last,
                                 preferred_element_type=F32)
        out = []
        for n, c in enumerate(chunks):
            tiles = []
            for v_idx in range(2):
                k0 = v_idx * n_keys + n * CHUNK
                sc = sc_all[k0:k0 + span, n * HEAD_PAIR:(n + 1) * HEAD_PAIR] + bias_ref[j, v_idx]
                if mask_start and c * CHUNK < WINDOW:
                    sc = jnp.where(key_row + c * CHUNK >= first_valid, sc, -jnp.inf)
                tiles.append(sc)
            out.append((c, tiles))
        return out

    def attend(s, c, j, scs):
        r0 = c * CHUNK
        q_rows = slice(s * tm + r0, s * tm + r0 + CHUNK)
        g0 = j * GQA_GROUP * HEAD_DIM
        probs = []
        for v_idx, sc in enumerate(scs):
            ha, hb = _pair_heads(j, v_idx)
            sink = jnp.where(low, sinks_ref[ha], sinks_ref[hb])
            m = jnp.maximum(jnp.max(sc, axis=0, keepdims=True), sink)
            p = jnp.exp(sc - m)
            denom = jnp.sum(p, axis=0, keepdims=True) + jnp.exp(sink - m)
            probs.append((p * (1.0 / denom)).astype(BF16))
        p_kn = jnp.concatenate(probs, axis=0)
        v_kd = jnp.concatenate([vzbuf[s, j, 0, r0:r0 + span, :],
                                vzbuf[s, j, 1, r0:r0 + span, :]], axis=0)
        o = lax.dot_general(p_kn, v_kd, contract_first, preferred_element_type=F32)
        obuf[q_rows, g0:g0 + HEAD_PAIR] = o[:CHUNK].astype(BF16)
        obuf[q_rows, g0 + HEAD_PAIR:g0 + 2 * HEAD_PAIR] = o[CHUNK:].astype(BF16)

    def units_of(sub):
        sl, r0 = sub
        c0, c1 = r0 // CHUNK, (r0 + sub_t) // CHUNK
        step = ATTN_CHUNKS_PER_UNIT if (c1 - c0) % ATTN_CHUNKS_PER_UNIT == 0 else 1
        return [(s, tuple(range(c, c + step)), j) for s in range(sl.start, sl.stop)
                for c in range(c0, c1, step) for j in range(N_KV_HEADS)]

    def run_units(units, fillers):
        every = max(1, len(units) // max(1, len(fillers)))
        pending = []

        def retire():
            (s, _, j), per_chunk = pending.pop(0)
            for c, tiles in per_chunk:
                attend(s, c, j, tiles)

        for n, unit in enumerate(units):
            pending.append((unit, scores(*unit)))
            if len(pending) > ATTN_LOOKAHEAD:
                retire()
            if fillers and (n + 1) % every == 0:
                fillers.pop(0)()
        while pending:
            retire()
        for filler in fillers:
            filler()

    for piece in in_pieces(subs[0]):
        piece()
    for i, sub in enumerate(subs):
        fillers = []
        if i > 0:
            fillers += out_pieces(subs[i - 1])
        if i + 1 < len(subs):
            fillers += in_pieces(subs[i + 1])
        run_units(units_of(sub), fillers)
    for piece in out_pieces(subs[-1]):
        piece()

    if nt > 1:
        kzbuf[:, :, :, 0:WINDOW, :] = kzbuf[:, :, :, tm:tm + WINDOW, :]
        vzbuf[:, :, :, 0:WINDOW, :] = vzbuf[:, :, :, tm:tm + WINDOW, :]


def _attn_layer(x, kc, vc, gin, gout, wqkv, bqkv, wo, bo, sinks, bias, *, layer, bb, tm, mask_start):
    B, T, _ = x.shape
    nt = T // tm
    kv_rows = min(WINDOW, tm)
    body = functools.partial(_attn_body, bb=bb, tm=tm, nt=nt, mask_start=mask_start)
    return pl.pallas_call(
        body,
        grid=(B // bb, nt),
        out_shape=(jax.ShapeDtypeStruct(x.shape, F32),
                   jax.ShapeDtypeStruct((B, kv_rows, KV_DIM), F32),
                   jax.ShapeDtypeStruct((B, kv_rows, KV_DIM), F32)),
        in_specs=[
            _seq_spec(bb, tm, D_MODEL),
            _state_spec(bb, WINDOW, KV_DIM),
            _state_spec(bb, WINDOW, KV_DIM),
            _const_spec((1, D_MODEL)),
            _const_spec((1, D_MODEL)),
            _layer_spec(wqkv, layer),
            _const_spec(bqkv.shape),
            _layer_spec(wo, layer),
            _const_spec(bo.shape),
            pl.BlockSpec(memory_space=pltpu.SMEM),
            _const_spec(bias.shape),
        ],
        out_specs=(_seq_spec(bb, tm, D_MODEL),
                   _state_spec(bb, kv_rows, KV_DIM),
                   _state_spec(bb, kv_rows, KV_DIM)),
        scratch_shapes=[
            pltpu.VMEM((bb * tm, Q_DIM), BF16),
            pltpu.VMEM((bb, N_KV_HEADS, 2, WINDOW + tm, HEAD_PAIR), BF16),
            pltpu.VMEM((bb, N_KV_HEADS, 2, WINDOW + tm, HEAD_PAIR), BF16),
            pltpu.VMEM((bb * tm, Q_DIM), BF16),
            pltpu.VMEM((bb * tm, D_MODEL), F32),
        ],
        compiler_params=_params(),
        name="attn_mixer",
    )(x, kc, vc, gin, gout, wqkv, bqkv, wo, bo, sinks, bias)


def _conv_body(x_ref, st_ref, gin_ref, gout_ref, w1_ref, b1_ref, wdw_ref, bdw_ref, lng_ref, lnb_ref,
               w2_ref, b2_ref, o_ref, nst_ref, gbuf, ybuf, cbuf, *, bb, tm, nt):
    t = pl.program_id(1)
    rows = bb * tm
    hist0 = CONV_PAD - CONV_HIST

    @pl.when(t == 0)
    def _():
        gbuf[:, hist0:CONV_PAD, :] = st_ref[...]

    row = lax.broadcasted_iota(jnp.int32, (SUBLANES, CONV_LANE_TILE), 0)
    n_tiles = CONV_ROW_TILE // SUBLANES
    max_a = (hist0 + CONV_WIDTH - 1) // SUBLANES
    tiles_per_block = CONV_ROW_BLOCK // CONV_ROW_TILE

    def row_tiles(rb):
        out = []
        for i in range(tiles_per_block):
            flat = rb * CONV_ROW_BLOCK + i * CONV_ROW_TILE
            out.append((flat // tm, flat % tm, flat))
        return out

    def block_rows(ref, rb):
        flat = rb * CONV_ROW_BLOCK
        if tm >= CONV_ROW_BLOCK:
            return ref.at[flat // tm, flat % tm:flat % tm + CONV_ROW_BLOCK, :]
        return ref.at[flat // tm:(flat + CONV_ROW_BLOCK) // tm, :, :]

    def pointwise_in(rb):
        x = block_rows(x_ref, rb)[...].reshape(CONV_ROW_BLOCK, D_MODEL)
        h = _rms(x, gin_ref[...]).astype(BF16)
        ag = _dot(h, w1_ref[...]) + b1_ref[...]
        glu = ag[:, :D_MODEL] * jax.nn.sigmoid(ag[:, D_MODEL:])
        for i, (s, r0, _) in enumerate(row_tiles(rb)):
            gbuf[s, CONV_PAD + r0:CONV_PAD + r0 + CONV_ROW_TILE, :] = (
                glu[i * CONV_ROW_TILE:(i + 1) * CONV_ROW_TILE])

    def depthwise(rb):
        for l0 in range(0, D_MODEL, CONV_LANE_TILE):
            lanes = slice(l0, l0 + CONV_LANE_TILE)
            carried, carried_for = {}, None
            for s, r0, flat in row_tiles(rb):
                if carried_for != (s, r0):
                    carried = {}
                tiles = [gbuf[s, r0 + SUBLANES * j:r0 + SUBLANES * (j + 1), lanes]
                         for j in range(n_tiles + max_a)]
                out = [None] * n_tiles
                for b in range(SUBLANES):
                    taps = [a for a in range(max_a + 1)
                            if 0 <= SUBLANES * a + b - hist0 < CONV_WIDTH]
                    wts = [wdw_ref[SUBLANES * a + b - hist0:SUBLANES * a + b - hist0 + 1, lanes]
                           for a in taps]
                    z = []
                    for m in range(n_tiles + (1 if b else 0)):
                        if m == 0 and b in carried:
                            z.append(carried[b])
                            continue
                        acc = None
                        for a, wt in zip(taps, wts):
                            term = tiles[m + a] * wt
                            acc = term if acc is None else acc + term
                        z.append(acc)
                    if b:
                        carried[b] = z[n_tiles]
                    for m in range(n_tiles):
                        if b == 0:
                            part = z[m]
                        else:
                            part = pltpu.roll(jnp.where(row >= b, z[m], z[m + 1]), SUBLANES - b, 0)
                        out[m] = part if out[m] is None else out[m] + part
                carried_for = (s, r0 + CONV_ROW_TILE)
                for m in range(n_tiles):
                    lo = flat + SUBLANES * m
                    ybuf[lo:lo + SUBLANES, lanes] = out[m]
        for s, r0, flat in row_tiles(rb):
            rs = slice(flat, flat + CONV_ROW_TILE)
            y = _layernorm(ybuf[rs, :] + bdw_ref[...], lng_ref[...], lnb_ref[...])
            cbuf[rs, :] = (y * jax.nn.sigmoid(y)).astype(BF16)

    def pointwise_out(rb):
        flat = rb * CONV_ROW_BLOCK
        y = _dot(cbuf[flat:flat + CONV_ROW_BLOCK, :], w2_ref[...]) + b2_ref[...]
        x = block_rows(x_ref, rb)[...]
        block_rows(o_ref, rb)[...] = x + _rms(y, gout_ref[...]).reshape(x.shape)

    n_blocks = rows // CONV_ROW_BLOCK
    pointwise_in(0)
    for rb in range(1, n_blocks):
        depthwise(rb - 1)
        pointwise_in(rb)
        pointwise_out(rb - 1)
    depthwise(n_blocks - 1)
    pointwise_out(n_blocks - 1)

    last = gbuf[:, CONV_PAD + tm - CONV_HIST:CONV_PAD + tm, :]

    @pl.when(t == nt - 1)
    def _():
        nst_ref[...] = last

    if nt > 1:
        gbuf[:, hist0:CONV_PAD, :] = last


def _conv_layer(x, st, gin, gout, w1, b1, wdw, bdw, lng, lnb, w2, b2, *, bb, tm):
    B, T, _ = x.shape
    nt = T // tm
    assert tm >= CONV_HIST and tm % CONV_ROW_TILE == 0
    body = functools.partial(_conv_body, bb=bb, tm=tm, nt=nt)
    return pl.pallas_call(
        body,
        grid=(B // bb, nt),
        out_shape=(jax.ShapeDtypeStruct(x.shape, F32),
                   jax.ShapeDtypeStruct((B, CONV_HIST, D_MODEL), F32)),
        in_specs=[
            _seq_spec(bb, tm, D_MODEL),
            _state_spec(bb, CONV_HIST, D_MODEL),
            _const_spec((1, D_MODEL)),
            _const_spec((1, D_MODEL)),
            _const_spec(w1.shape),
            _const_spec(b1.shape),
            _const_spec(wdw.shape),
            _const_spec(bdw.shape),
            _const_spec(lng.shape),
            _const_spec(lnb.shape),
            _const_spec(w2.shape),
            _const_spec(b2.shape),
        ],
        out_specs=(_seq_spec(bb, tm, D_MODEL), _state_spec(bb, CONV_HIST, D_MODEL)),
        scratch_shapes=[
            pltpu.VMEM((bb, CONV_PAD + tm, D_MODEL), F32),
            pltpu.VMEM((bb * tm, D_MODEL), F32),
            pltpu.VMEM((bb * tm, D_MODEL), BF16),
        ],
        compiler_params=_params(),
        name="conv_mixer",
    )(x, st, gin, gout, w1, b1, wdw, bdw, lng, lnb, w2, b2)


def _cmlp_body(x_ref, gin_ref, gout_ref, win_ref, bin_ref, lng_ref, lnb_ref, ws_ref, bs_ref,
               wout_ref, bout_ref, *rest, bb, tm, blk, emit_v):
    if emit_v:
        o_ref, v_ref, vbuf, ybuf = rest
    else:
        o_ref, vbuf, ybuf = rest
    rows = bb * tm
    x = x_ref[...].reshape(rows, D_MODEL)
    h = _rms(x, gin_ref[...]).astype(BF16)
    hv = _gelu(_dot(h, win_ref[:, CMLP_DIM:]) + bin_ref[:, CMLP_DIM:])
    vln = _layernorm(hv, lng_ref[...], lnb_ref[...])
    if emit_v:
        v_ref[...] = vln.reshape(bb, tm, CMLP_DIM)
    vbuf[...] = vln.astype(BF16)

    pos_r = lax.broadcasted_iota(jnp.int32, (blk, blk), 0) // CHUNK
    pos_c = lax.broadcasted_iota(jnp.int32, (blk, blk), 1) // CHUNK
    for g in range(CMLP_GROUPS):
        lanes = slice(g * CMLP_GROUP_DIM, (g + 1) * CMLP_GROUP_DIM)
        ug = _gelu(_dot(h, win_ref[:, lanes]) + bin_ref[:, lanes])
        wsm = jnp.where(pos_c <= pos_r, ws_ref[g], 0.0).astype(BF16)
        for n in range(rows // blk):
            rs = slice(n * blk, (n + 1) * blk)
            gate = _dot(wsm, vbuf[rs, lanes]) + bs_ref[g]
            ybuf[rs, lanes] = (ug[rs] * gate).astype(BF16)

    y = _dot(ybuf[...], wout_ref[...]) + bout_ref[...]
    o_ref[...] = (x + _rms(y, gout_ref[...])).reshape(bb, tm, D_MODEL)


def _cmlp_layer(x, gin, gout, win, bin_, lng, lnb, ws, bs, wout, bout, *, bb, tm, emit_v):
    B, T, _ = x.shape
    blk = min(T, MIX_BLOCK)
    assert tm % blk == 0
    ws = ws[:, :blk, :blk]
    bs = bs[:, :blk, None]
    body = functools.partial(_cmlp_body, bb=bb, tm=tm, blk=blk, emit_v=emit_v)
    out_shape = [jax.ShapeDtypeStruct(x.shape, F32)]
    out_specs = [_seq_spec(bb, tm, D_MODEL)]
    if emit_v:
        out_shape.append(jax.ShapeDtypeStruct((B, T, CMLP_DIM), F32))
        out_specs.append(_seq_spec(bb, tm, CMLP_DIM))
    return pl.pallas_call(
        body,
        grid=(B // bb, T // tm),
        out_shape=tuple(out_shape),
        in_specs=[
            _seq_spec(bb, tm, D_MODEL),
            _const_spec((1, D_MODEL)),
            _const_spec((1, D_MODEL)),
            _const_spec(win.shape),
            _const_spec(bin_.shape),
            _const_spec(lng.shape),
            _const_spec(lnb.shape),
            _const_spec(ws.shape),
            _const_spec(bs.shape),
            _const_spec(wout.shape),
            _const_spec(bout.shape),
        ],
        out_specs=tuple(out_specs),
        scratch_shapes=[
            pltpu.VMEM((bb * tm, CMLP_DIM), BF16),
            pltpu.VMEM((bb * tm, CMLP_DIM), BF16),
        ],
        compiler_params=_params(),
        name="cmlp_mixer",
    )(x, gin, gout, win, bin_, lng, lnb, ws, bs, wout, bout)


def _ffn_body(x_ref, st_ref, gin_ref, gout_ref, wup_ref, wdw_ref, bdw_ref, wdn_ref,
              o_ref, nst_ref, *scratch, bb, tm, nt):
    n_chunks = D_FF // FFN_CHUNK
    upbufs = scratch[:n_chunks]
    actbufs = scratch[n_chunks:]
    t = pl.program_id(1)
    rows = bb * tm
    hist0 = FFN_PAD - FFN_HIST

    def halves(c):
        return ((c * FFN_CHUNK, slice(0, FFN_CHUNK)),
                (D_FF + c * FFN_CHUNK, slice(FFN_CHUNK, 2 * FFN_CHUNK)))

    @pl.when(t == 0)
    def _():
        for c in range(n_chunks):
            upbufs[c][:, 0:hist0, :] = jnp.zeros((bb, hist0, 2 * FFN_CHUNK), F32)
            for lo, lanes in halves(c):
                upbufs[c][:, hist0:FFN_PAD, lanes] = st_ref[:, :, lo:lo + FFN_CHUNK]

    if bb >= FFN_ROW_SPLIT:
        sub_b, sub_t = bb // FFN_ROW_SPLIT, tm
        subs = [(slice(i * sub_b, (i + 1) * sub_b), 0) for i in range(FFN_ROW_SPLIT)]
    else:
        sub_b, sub_t = bb, tm // FFN_ROW_SPLIT
        subs = [(slice(0, bb), i * sub_t) for i in range(FFN_ROW_SPLIT)]
    sub_rows = sub_b * sub_t
    n_tiles = sub_t // SUBLANES
    tile_row = lax.broadcasted_iota(jnp.int32, (sub_b * n_tiles, SUBLANES, FFN_CHUNK), 1)

    def up_proj(h, sub, c):
        sl, r0 = sub
        for lo, lanes in halves(c):
            up = _dot(h, wup_ref[:, lo:lo + FFN_CHUNK])
            upbufs[c][sl, FFN_PAD + r0:FFN_PAD + r0 + sub_t, lanes] = up.reshape(sub_b, sub_t, FFN_CHUNK)

    def delayed(cur, hist):
        cur_r = pltpu.roll(cur, 1, 1)
        hist_r = pltpu.roll(hist, 1, 1)
        cur_r4 = cur_r.reshape(sub_b, n_tiles, SUBLANES, FFN_CHUNK)
        prev_r = jnp.concatenate([hist_r[:, None], cur_r4[:, :n_tiles - 1]], axis=1)
        out = jnp.where(tile_row < 1, prev_r.reshape(sub_b * n_tiles, SUBLANES, FFN_CHUNK), cur_r)
        return out, hist_r

    def conv(sub, c, lo, lanes):
        sl, r0 = sub
        w = wdw_ref[:, lo:lo + FFN_CHUNK]
        cur = upbufs[c][sl, FFN_PAD + r0:FFN_PAD + r0 + sub_t, lanes].reshape(
            sub_b * n_tiles, SUBLANES, FFN_CHUNK)
        hist = upbufs[c][sl, r0:r0 + FFN_PAD, lanes]
        acc, acc_hist = None, None
        for kk in range(FFN_HIST):
            term, term_hist = cur * w[kk:kk + 1], hist * w[kk:kk + 1]
            if acc is not None:
                term, term_hist = term + acc, term_hist + acc_hist
            acc, acc_hist = delayed(term, term_hist)
        out = cur * w[FFN_HIST:FFN_HIST + 1] + bdw_ref[:, lo:lo + FFN_CHUNK] + acc
        return out.reshape(sub_rows, FFN_CHUNK)

    def gate(sub, c, actbuf, col):
        (glo, glanes), (ulo, ulanes) = halves(c)
        act = _gelu(conv(sub, c, glo, glanes)) * conv(sub, c, ulo, ulanes)
        actbuf[:, col:col + FFN_CHUNK] = act.astype(BF16)

    groups, start = [], 0
    for size in FFN_DOWN_GROUPS:
        groups.append((start, start + size))
        start += size
    for si, sub in enumerate(subs):
        sl, r0 = sub
        x = x_ref[sl, r0:r0 + sub_t, :].reshape(sub_rows, D_MODEL)
        h = _rms(x, gin_ref[...]).astype(BF16)
        for gi, (c0, c1) in enumerate(groups):
            for c in range(c0, c1):
                up_proj(h, sub, c)
                gate(sub, c, actbufs[si * len(groups) + gi], (c - c0) * FFN_CHUNK)
        part_rows = sub_rows // FFN_DOWN_ROW_PARTS
        o3 = o_ref.at[sl, r0:r0 + sub_t, :]
        for rp in range(FFN_DOWN_ROW_PARTS):
            rs = slice(rp * part_rows, (rp + 1) * part_rows)
            y = None
            for gi, (c0, c1) in enumerate(groups):
                part = _dot(actbufs[si * len(groups) + gi][rs, :], wdn_ref[c0 * FFN_CHUNK:c1 * FFN_CHUNK, :])
                y = part if y is None else y + part
            out = x[rs] + _rms(y, gout_ref[...])
            if sub_b == 1:
                o3[0, rs, :] = out
            else:
                nb = part_rows // sub_t
                o3[rp * nb:(rp + 1) * nb] = out.reshape(nb, sub_t, D_MODEL)

    for c in range(n_chunks):
        for lo, lanes in halves(c):
            last = upbufs[c][:, FFN_PAD + tm - FFN_HIST:FFN_PAD + tm, lanes]

            @pl.when(t == nt - 1)
            def _(last=last, lo=lo):
                nst_ref[:, :, lo:lo + FFN_CHUNK] = last

            if nt > 1:
                upbufs[c][:, hist0:FFN_PAD, lanes] = last


def _ffn_layer(x, st, gin, gout, wup, wdw, bdw, wdn, *, layer, bb, tm):
    B, T, _ = x.shape
    nt = T // tm
    assert tm >= FFN_HIST and sum(FFN_DOWN_GROUPS) * FFN_CHUNK == D_FF
    body = functools.partial(_ffn_body, bb=bb, tm=tm, nt=nt)
    return pl.pallas_call(
        body,
        grid=(B // bb, nt),
        out_shape=(jax.ShapeDtypeStruct(x.shape, F32),
                   jax.ShapeDtypeStruct((B, FFN_HIST, 2 * D_FF), F32)),
        in_specs=[
            _seq_spec(bb, tm, D_MODEL),
            _state_spec(bb, FFN_HIST, 2 * D_FF),
            _const_spec((1, D_MODEL)),
            _const_spec((1, D_MODEL)),
            _layer_spec(wup, layer),
            _const_spec(wdw.shape),
            _const_spec(bdw.shape),
            _layer_spec(wdn, layer),
        ],
        out_specs=(_seq_spec(bb, tm, D_MODEL), _state_spec(bb, FFN_HIST, 2 * D_FF)),
        scratch_shapes=(
            [pltpu.VMEM((bb, FFN_PAD + tm, 2 * FFN_CHUNK), F32)] * (D_FF // FFN_CHUNK)
            + [pltpu.VMEM((bb * tm // FFN_ROW_SPLIT, size * FFN_CHUNK), BF16)
               for _ in range(FFN_ROW_SPLIT) for size in FFN_DOWN_GROUPS]),
        compiler_params=_params(),
        name="conv_ffn",
    )(x, st, gin, gout, wup, wdw, bdw, wdn)


def _trunk(x, caches, p, bias, *, bb, tm):
    prompt = caches is None
    B = x.shape[0]
    new_k, new_v, new_conv, new_cv, new_ffn = [], [], [], [], []
    for i in range(DEPTH):
        kind, j = i % N_MIXERS, i // N_MIXERS
        g = p['norm_gain'][i]
        gains = [g[n][None, :] for n in range(4)]
        if kind == 0:
            if prompt:
                kc = jnp.zeros((B, WINDOW, KV_DIM), F32)
                vc = kc
            else:
                kc = caches[0][j].reshape(B, WINDOW, KV_DIM)
                vc = caches[1][j].reshape(B, WINDOW, KV_DIM)
            x, k_rows, v_rows = _attn_layer(
                x, kc, vc, gains[0], gains[1], p['attn_w_qkv'], p['attn_b_qkv'][j][None, :],
                p['attn_w_o'], p['attn_b_o'][j][None, :], p['attn_sinks'][j], bias, layer=j,
                bb=bb, tm=tm, mask_start=prompt)
            new_k.append(k_rows.reshape(B, -1, N_KV_HEADS, HEAD_DIM))
            new_v.append(v_rows.reshape(B, -1, N_KV_HEADS, HEAD_DIM))
        elif kind == 1:
            st = jnp.zeros((B, CONV_HIST, D_MODEL), F32) if prompt else caches[2][j]
            x, s = _conv_layer(
                x, st, gains[0], gains[1], p['conv_w_pw1'][j], p['conv_b_pw1'][j][None, :],
                p['conv_w_dw'][j], p['conv_b_dw'][j][None, :], p['conv_ln_g'][j][None, :],
                p['conv_ln_b'][j][None, :], p['conv_w_pw2'][j], p['conv_b_pw2'][j][None, :],
                bb=bb, tm=tm)
            new_conv.append(s)
        else:
            res = _cmlp_layer(
                x, gains[0], gains[1], p['cmlp_w_in'][j], p['cmlp_b_in'][j][None, :],
                p['cmlp_ln_g'][j][None, :], p['cmlp_ln_b'][j][None, :], p['cmlp_w_s'][j],
                p['cmlp_b_s'][j], p['cmlp_w_out'][j], p['cmlp_b_out'][j][None, :],
                bb=bb, tm=tm, emit_v=not prompt)
            x = res[0]
            if not prompt:
                new_cv.append(res[1])
        st = jnp.zeros((B, FFN_HIST, 2 * D_FF), F32) if prompt else caches[3][i]
        x, s = _ffn_layer(x, st, gains[2], gains[3], p['ffn_w_up'], p['ffn_w_dw'][i],
                          p['ffn_b_dw'][i][None, :], p['ffn_w_down'], layer=i, bb=bb, tm=tm)
        new_ffn.append(s)
    cv = jnp.stack(new_cv) if new_cv else None
    return x, jnp.stack(new_k), jnp.stack(new_v), jnp.stack(new_conv), cv, jnp.stack(new_ffn)


def kernel(x_prompt, x_sample, cache_attn_k, cache_attn_v, state_conv, state_ffn_conv, rel_bias_table, norm_gain, attn_w_qkv, attn_b_qkv, attn_w_o, attn_b_o, attn_sinks, conv_w_pw1, conv_b_pw1, conv_w_dw, conv_b_dw, conv_ln_g, conv_ln_b, conv_w_pw2, conv_b_pw2, cmlp_w_in, cmlp_b_in, cmlp_ln_g, cmlp_ln_b, cmlp_w_s, cmlp_b_s, cmlp_w_out, cmlp_b_out, ffn_w_up, ffn_w_dw, ffn_b_dw, ffn_w_down):
    p = {
        'norm_gain': norm_gain,
        'attn_w_qkv': attn_w_qkv.astype(BF16), 'attn_b_qkv': attn_b_qkv,
        'attn_w_o': attn_w_o.astype(BF16), 'attn_b_o': attn_b_o, 'attn_sinks': attn_sinks,
        'conv_w_pw1': conv_w_pw1.astype(BF16), 'conv_b_pw1': conv_b_pw1, 'conv_w_dw': conv_w_dw,
        'conv_b_dw': conv_b_dw, 'conv_ln_g': conv_ln_g, 'conv_ln_b': conv_ln_b,
        'conv_w_pw2': conv_w_pw2.astype(BF16), 'conv_b_pw2': conv_b_pw2,
        'cmlp_w_in': cmlp_w_in.astype(BF16), 'cmlp_b_in': cmlp_b_in, 'cmlp_ln_g': cmlp_ln_g,
        'cmlp_ln_b': cmlp_ln_b, 'cmlp_w_s': cmlp_w_s, 'cmlp_b_s': cmlp_b_s,
        'cmlp_w_out': cmlp_w_out.astype(BF16), 'cmlp_b_out': cmlp_b_out,
        'ffn_w_up': ffn_w_up.astype(BF16), 'ffn_w_dw': ffn_w_dw, 'ffn_b_dw': ffn_b_dw,
        'ffn_w_down': ffn_w_down.astype(BF16),
    }
    bias = _rel_bias(rel_bias_table)
    y_prompt, p_attn_k, p_attn_v, p_conv, _, p_ffn_conv = _trunk(
        x_prompt, None, p, bias, bb=1, tm=512)
    y_sample, s_attn_k, s_attn_v, s_conv, s_cmlp_v, s_ffn_conv = _trunk(
        x_sample, (cache_attn_k, cache_attn_v, state_conv, state_ffn_conv), p, bias, bb=8, tm=64)
    return (y_prompt, y_sample, p_attn_k, p_attn_v, p_conv, p_ffn_conv,
            s_attn_k, s_attn_v, s_conv, s_cmlp_v, s_ffn_conv)
```

```python
import functools
import math

import jax
import jax.numpy as jnp
import numpy as np
from jax import lax
from jax.experimental import pallas as pl
from jax.experimental.pallas import tpu as pltpu

D_MODEL = 1024
DEPTH = 4
CHUNK = 64
N_MIXERS = 3
N_HEADS = 16
N_KV_HEADS = 4
HEAD_DIM = 64
GQA_GROUP = N_HEADS // N_KV_HEADS
WINDOW = 128
KV_DIM = N_KV_HEADS * HEAD_DIM
Q_DIM = N_HEADS * HEAD_DIM
NUM_BUCKETS = 32
MAX_DISTANCE = 128
CONV_WIDTH = 31
CONV_HIST = CONV_WIDTH - 1
MIX_BLOCK = 128
CMLP_GROUPS = 4
CMLP_DIM = 2 * D_MODEL
CMLP_GROUP_DIM = CMLP_DIM // CMLP_GROUPS
D_FF = 2816
FFN_CONV_WIDTH = 3
FFN_HIST = FFN_CONV_WIDTH - 1
RMS_EPS = 1e-6
_GELU_C0 = math.sqrt(2.0 / math.pi)
_GELU_C1 = _GELU_C0 * 0.044715
LN_EPS = 1e-5

F32 = jnp.float32
BF16 = jnp.bfloat16

V7X_VMEM_BYTES = 64 * 1024 * 1024
VMEM_LIMIT_BYTES = V7X_VMEM_BYTES - 8 * 1024 * 1024
SUBLANES = 8
FFN_CHUNK = 256
FFN_DOWN_GROUPS = (4, 4, 3)
FFN_DOWN_ROW_PARTS = 2
FFN_ROW_SPLIT = 1
FFN_PAD = SUBLANES
CONV_PAD = 32
CONV_ROW_TILE = 64
CONV_LANE_TILE = 128
CONV_ROW_BLOCK = 256


def _rms(x, g):
    ms = jnp.mean(x * x, axis=-1, keepdims=True)
    return x * lax.rsqrt(ms + RMS_EPS) * g


def _layernorm(x, g, b):
    mu = jnp.mean(x, axis=-1, keepdims=True)
    xc = x - mu
    var = jnp.mean(xc * xc, axis=-1, keepdims=True)
    return xc * lax.rsqrt(var + LN_EPS) * g + b


def _gelu(x):
    hx = 0.5 * x
    return hx + hx * jnp.tanh(x * (_GELU_C0 + _GELU_C1 * (x * x)))


def _dot(a, b):
    return jnp.dot(a, b, preferred_element_type=F32)


def _const_spec(shape):
    zeros = (0,) * len(shape)
    return pl.BlockSpec(shape, lambda b, t: zeros, pipeline_mode=pl.Buffered(1))


def _layer_spec(stacked, layer):
    zeros = (0,) * (stacked.ndim - 1)
    return pl.BlockSpec((None,) + stacked.shape[1:], lambda b, t: (layer,) + zeros,
                        pipeline_mode=pl.Buffered(1))


def _seq_spec(bb, rows, width):
    return pl.BlockSpec((bb, rows, width), lambda b, t: (b, t, 0))


def _state_spec(bb, rows, width):
    return pl.BlockSpec((bb, rows, width), lambda b, t: (b, 0, 0))


def _params(flags=None):
    return pltpu.CompilerParams(
        dimension_semantics=("arbitrary", "arbitrary"),
        vmem_limit_bytes=VMEM_LIMIT_BYTES,
        flags=flags,
    )


HEAD_PAIR = 2 * HEAD_DIM
ATTN_LOOKAHEAD = 2
ATTN_CHUNKS_PER_UNIT = 2
ATTN_ROW_SPLIT = 2
PAIR_STEP = GQA_GROUP // 2


def _pair_heads(j, v):
    return GQA_GROUP * j + v, GQA_GROUP * j + PAIR_STEP + v


def _bias_body(bucket_ref, table_ref, o_ref):
    bk = bucket_ref[...]
    low = lax.broadcasted_iota(jnp.int32, (1, HEAD_PAIR), 1) < CHUNK
    for j in range(N_KV_HEADS):
        for v in range(2):
            ha, hb = _pair_heads(j, v)
            acc = jnp.zeros(bk.shape, F32)
            for b in range(NUM_BUCKETS):
                val = jnp.where(low, table_ref[b, ha], table_ref[b, hb])
                acc = jnp.where(bk == b, val, acc)
            o_ref[j, v] = acc


def _t5_bucket(rel):
    half = NUM_BUCKETS // 2
    max_exact = half // 2
    n = jnp.abs(rel)
    log_ratio = jnp.log(jnp.maximum(n, 1).astype(F32) / max_exact) / math.log(MAX_DISTANCE / max_exact)
    large = jnp.minimum(max_exact + (log_ratio * (half - max_exact)).astype(jnp.int32), half - 1)
    return jnp.where(rel > 0, half, 0) + jnp.where(n < max_exact, n, large)


def _rel_bias(table):
    q_pos = jnp.arange(CHUNK, dtype=jnp.int32)
    k_pos = jnp.arange(WINDOW + CHUNK, dtype=jnp.int32) - WINDOW
    bucket = _t5_bucket(k_pos[:, None] - q_pos[None, :]).astype(jnp.int32)
    bucket = jnp.concatenate([bucket, bucket], axis=1)
    return pl.pallas_call(
        _bias_body,
        out_shape=jax.ShapeDtypeStruct((N_KV_HEADS, 2, WINDOW + CHUNK, HEAD_PAIR), F32),
        in_specs=[pl.BlockSpec(memory_space=pltpu.VMEM), pl.BlockSpec(memory_space=pltpu.SMEM)],
        out_specs=pl.BlockSpec(memory_space=pltpu.VMEM),
        name="rel_bias",
    )(bucket, table)


def _attn_body(x_ref, kc_ref, vc_ref, gin_ref, gout_ref, wqkv_ref, bqkv_ref, wo_ref, bo_ref,
               sinks_ref, bias_ref, o_ref, kout_ref, vout_ref, qbuf, kzbuf, vzbuf, obuf, ybuf,
               *, bb, tm, nt, mask_start):
    t = pl.program_id(1)
    rows = bb * tm
    kv_rows = min(WINDOW, tm)
    span = WINDOW + CHUNK
    low = lax.broadcasted_iota(jnp.int32, (1, HEAD_PAIR), 1) < CHUNK

    def expand(buf, sl, r0, kv):
        nb, n = kv.shape[0], kv.shape[1]
        for pair in range(N_KV_HEADS // 2):
            tile = kv[:, :, pair * HEAD_PAIR:(pair + 1) * HEAD_PAIR]
            swapped = pltpu.roll(tile.reshape(nb * n, HEAD_PAIR), HEAD_DIM, 1).reshape(nb, n, HEAD_PAIR)
            for j, (lo_src, hi_src) in ((2 * pair, (tile, swapped)), (2 * pair + 1, (swapped, tile))):
                buf[sl, j, 0, r0:r0 + n, :] = jnp.where(low, lo_src, 0.0).astype(BF16)
                buf[sl, j, 1, r0:r0 + n, :] = jnp.where(low, 0.0, hi_src).astype(BF16)

    @pl.when(t == 0)
    def _():
        expand(kzbuf, slice(0, bb), 0, kc_ref[...])
        expand(vzbuf, slice(0, bb), 0, vc_ref[...])

    if bb >= ATTN_ROW_SPLIT:
        sub_b, sub_t = bb // ATTN_ROW_SPLIT, tm
        subs = [(slice(i * sub_b, (i + 1) * sub_b), 0) for i in range(ATTN_ROW_SPLIT)]
    else:
        sub_b, sub_t = bb, tm // ATTN_ROW_SPLIT
        subs = [(slice(0, bb), i * sub_t) for i in range(ATTN_ROW_SPLIT)]
    sub_rows = sub_b * sub_t
    assert sub_t % CHUNK == 0 and (sub_b > 1 or sub_t >= kv_rows or ATTN_ROW_SPLIT == 1)
    n_piece = GQA_GROUP * HEAD_DIM

    def flat0(sub):
        sl, r0 = sub
        return sl.start * tm + r0

    def load_x(sub):
        sl, r0 = sub
        return x_ref[sl, r0:r0 + sub_t, :].reshape(sub_rows, D_MODEL)

    def in_pieces(sub):
        sl, r0 = sub
        h = _rms(load_x(sub), gin_ref[...]).astype(BF16)
        f0 = flat0(sub)

        def kv_piece(col, buf, out_ref):
            def run():
                kv = (_dot(h, wqkv_ref[:, col:col + KV_DIM]) + bqkv_ref[:, col:col + KV_DIM]).reshape(
                    sub_b, sub_t, KV_DIM)
                expand(buf, sl, WINDOW + r0, kv)
                if sub_b > 1 or r0 + sub_t == tm:
                    out_ref[sl] = kv[:, sub_t - kv_rows:, :]
            return run

        def q_piece(j):
            def run():
                col = j * n_piece
                q = _dot(h, wqkv_ref[:, col:col + n_piece]) + bqkv_ref[:, col:col + n_piece]
                qbuf[f0:f0 + sub_rows, col:col + n_piece] = (q * (HEAD_DIM ** -0.5)).astype(BF16)
            return run

        return ([kv_piece(Q_DIM, kzbuf, kout_ref), kv_piece(Q_DIM + KV_DIM, vzbuf, vout_ref)]
                + [q_piece(j) for j in range(N_KV_HEADS)])

    def out_pieces(sub):
        sl, r0 = sub
        f0 = flat0(sub)

        def piece(n):
            def run():
                col = n * n_piece
                ybuf[f0:f0 + sub_rows, col:col + n_piece] = (
                    _dot(obuf[f0:f0 + sub_rows, :], wo_ref[:, col:col + n_piece]) + bo_ref[:, col:col + n_piece])
            return run

        def finish():
            y = ybuf[f0:f0 + sub_rows, :]
            o_ref[sl, r0:r0 + sub_t, :] = (load_x(sub) + _rms(y, gout_ref[...])).reshape(sub_b, sub_t, D_MODEL)

        return [piece(n) for n in range(D_MODEL // n_piece)] + [finish]

    key_row = lax.broadcasted_iota(jnp.int32, (span, HEAD_PAIR), 0)
    first_valid = jnp.where(t == 0, WINDOW, 0)
    contract_last = (((1,), (1,)), ((), ()))
    contract_first = (((0,), (0,)), ((), ()))

    def scores(s, chunks, j):
        g0 = j * GQA_GROUP * HEAD_DIM
        r0 = chunks[0] * CHUNK
        n_keys = span + (len(chunks) - 1) * CHUNK
        q_parts = []
        for c in chunks:
            q_rows = slice(s * tm + c * CHUNK, s * tm + (c + 1) * CHUNK)
            q_parts += [qbuf[q_rows, g0:g0 + HEAD_PAIR], qbuf[q_rows, g0 + HEAD_PAIR:g0 + 2 * HEAD_PAIR]]
        keys = jnp.concatenate([kzbuf[s, j, v_idx, r0:r0 + n_keys, :] for v_idx in range(2)], axis=0)
        sc_all = lax.dot_general(keys, jnp.concatenate(q_parts, axis=0), contract_last,
                                 preferred_element_type=F32)
        out = []
        for n, c in enumerate(chunks):
            tiles = []
            for v_idx in range(2):
                k0 = v_idx * n_keys + n * CHUNK
                sc = sc_all[k0:k0 + span, n * HEAD_PAIR:(n + 1) * HEAD_PAIR] + bias_ref[j, v_idx]
                if mask_start and c * CHUNK < WINDOW:
                    sc = jnp.where(key_row + c * CHUNK >= first_valid, sc, -jnp.inf)
                tiles.append(sc)
            out.append((c, tiles))
        return out

    def attend(s, c, j, scs):
        r0 = c * CHUNK
        q_rows = slice(s * tm + r0, s * tm + r0 + CHUNK)
        g0 = j * GQA_GROUP * HEAD_DIM
        probs = []
        for v_idx, sc in enumerate(scs):
            ha, hb = _pair_heads(j, v_idx)
            sink = jnp.where(low, sinks_ref[ha], sinks_ref[hb])
            m = jnp.maximum(jnp.max(sc, axis=0, keepdims=True), sink)
            p = jnp.exp(sc - m)
            denom = jnp.sum(p, axis=0, keepdims=True) + jnp.exp(sink - m)
            probs.append((p * (1.0 / denom)).astype(BF16))
        p_kn = jnp.concatenate(probs, axis=0)
        v_kd = jnp.concatenate([vzbuf[s, j, 0, r0:r0 + span, :],
                                vzbuf[s, j, 1, r0:r0 + span, :]], axis=0)
        o = lax.dot_general(p_kn, v_kd, contract_first, preferred_element_type=F32)
        obuf[q_rows, g0:g0 + HEAD_PAIR] = o[:CHUNK].astype(BF16)
        obuf[q_rows, g0 + HEAD_PAIR:g0 + 2 * HEAD_PAIR] = o[CHUNK:].astype(BF16)

    def units_of(sub):
        sl, r0 = sub
        c0, c1 = r0 // CHUNK, (r0 + sub_t) // CHUNK
        step = ATTN_CHUNKS_PER_UNIT if (c1 - c0) % ATTN_CHUNKS_PER_UNIT == 0 else 1
        return [(s, tuple(range(c, c + step)), j) for s in range(sl.start, sl.stop)
                for c in range(c0, c1, step) for j in range(N_KV_HEADS)]

    def run_units(units, fillers):
        every = max(1, len(units) // max(1, len(fillers)))
        pending = []

        def retire():
            (s, _, j), per_chunk = pending.pop(0)
            for c, tiles in per_chunk:
                attend(s, c, j, tiles)

        for n, unit in enumerate(units):
            pending.append((unit, scores(*unit)))
            if len(pending) > ATTN_LOOKAHEAD:
                retire()
            if fillers and (n + 1) % every == 0:
                fillers.pop(0)()
        while pending:
            retire()
        for filler in fillers:
            filler()

    for piece in in_pieces(subs[0]):
        piece()
    for i, sub in enumerate(subs):
        fillers = []
        if i > 0:
            fillers += out_pieces(subs[i - 1])
        if i + 1 < len(subs):
            fillers += in_pieces(subs[i + 1])
        run_units(units_of(sub), fillers)
    for piece in out_pieces(subs[-1]):
        piece()

    if nt > 1:
        kzbuf[:, :, :, 0:WINDOW, :] = kzbuf[:, :, :, tm:tm + WINDOW, :]
        vzbuf[:, :, :, 0:WINDOW, :] = vzbuf[:, :, :, tm:tm + WINDOW, :]


def _attn_layer(x, kc, vc, gin, gout, wqkv, bqkv, wo, bo, sinks, bias, *, layer, bb, tm, mask_start):
    B, T, _ = x.shape
    nt = T // tm
    kv_rows = min(WINDOW, tm)
    body = functools.partial(_attn_body, bb=bb, tm=tm, nt=nt, mask_start=mask_start)
    return pl.pallas_call(
        body,
        grid=(B // bb, nt),
        out_shape=(jax.ShapeDtypeStruct(x.shape, F32),
                   jax.ShapeDtypeStruct((B, kv_rows, KV_DIM), F32),
                   jax.ShapeDtypeStruct((B, kv_rows, KV_DIM), F32)),
        in_specs=[
            _seq_spec(bb, tm, D_MODEL),
            _state_spec(bb, WINDOW, KV_DIM),
            _state_spec(bb, WINDOW, KV_DIM),
            _const_spec((1, D_MODEL)),
            _const_spec((1, D_MODEL)),
            _layer_spec(wqkv, layer),
            _const_spec(bqkv.shape),
            _layer_spec(wo, layer),
            _const_spec(bo.shape),
            pl.BlockSpec(memory_space=pltpu.SMEM),
            _const_spec(bias.shape),
        ],
        out_specs=(_seq_spec(bb, tm, D_MODEL),
                   _state_spec(bb, kv_rows, KV_DIM),
                   _state_spec(bb, kv_rows, KV_DIM)),
        scratch_shapes=[
            pltpu.VMEM((bb * tm, Q_DIM), BF16),
            pltpu.VMEM((bb, N_KV_HEADS, 2, WINDOW + tm, HEAD_PAIR), BF16),
            pltpu.VMEM((bb, N_KV_HEADS, 2, WINDOW + tm, HEAD_PAIR), BF16),
            pltpu.VMEM((bb * tm, Q_DIM), BF16),
            pltpu.VMEM((bb * tm, D_MODEL), F32),
        ],
        compiler_params=_params(),
        name="attn_mixer",
    )(x, kc, vc, gin, gout, wqkv, bqkv, wo, bo, sinks, bias)


def _conv_body(x_ref, st_ref, gin_ref, gout_ref, w1_ref, b1_ref, wdw_ref, bdw_ref, lng_ref, lnb_ref,
               w2_ref, b2_ref, shift_ref, o_ref, nst_ref, gbuf, ybuf, cbuf, *zbufs, bb, tm, nt):
    t = pl.program_id(1)
    rows = bb * tm
    hist0 = CONV_PAD - CONV_HIST

    @pl.when(t == 0)
    def _():
        gbuf[:, hist0:CONV_PAD, :] = st_ref[...]

    n_tiles = CONV_ROW_TILE // SUBLANES
    max_a = (hist0 + CONV_WIDTH - 1) // SUBLANES
    tiles_per_block = CONV_ROW_BLOCK // CONV_ROW_TILE

    def row_tiles(rb):
        out = []
        for i in range(tiles_per_block):
            flat = rb * CONV_ROW_BLOCK + i * CONV_ROW_TILE
            out.append((flat // tm, flat % tm, flat))
        return out

    def block_rows(ref, rb):
        flat = rb * CONV_ROW_BLOCK
        if tm >= CONV_ROW_BLOCK:
            return ref.at[flat // tm, flat % tm:flat % tm + CONV_ROW_BLOCK, :]
        return ref.at[flat // tm:(flat + CONV_ROW_BLOCK) // tm, :, :]

    def pointwise_in(rb):
        x = block_rows(x_ref, rb)[...].reshape(CONV_ROW_BLOCK, D_MODEL)
        h = _rms(x, gin_ref[...]).astype(BF16)
        ag = _dot(h, w1_ref[...]) + b1_ref[...]
        glu = ag[:, :D_MODEL] * jax.nn.sigmoid(ag[:, D_MODEL:])
        for i, (s, r0, _) in enumerate(row_tiles(rb)):
            gbuf[s, CONV_PAD + r0:CONV_PAD + r0 + CONV_ROW_TILE, :] = (
                glu[i * CONV_ROW_TILE:(i + 1) * CONV_ROW_TILE])

    def depthwise(rb):
        for l0 in range(0, D_MODEL, CONV_LANE_TILE):
            lanes = slice(l0, l0 + CONV_LANE_TILE)
            carried, carried_for = {}, None
            for ti, (s, r0, flat) in enumerate(row_tiles(rb)):
                if carried_for != (s, r0):
                    carried = {}
                tiles = [gbuf[s, r0 + SUBLANES * j:r0 + SUBLANES * (j + 1), lanes]
                         for j in range(n_tiles + max_a)]
                stored, half = 0, None
                for b in range(SUBLANES):
                    taps = [a for a in range(max_a + 1)
                            if 0 <= SUBLANES * a + b - hist0 < CONV_WIDTH]
                    wts = [wdw_ref[SUBLANES * a + b - hist0:SUBLANES * a + b - hist0 + 1, lanes]
                           for a in taps]
                    for m in range(n_tiles + (1 if b else 0)):
                        if m == 0 and b in carried:
                            acc = carried[b]
                        else:
                            acc = None
                            for a, wt in zip(taps, wts):
                                term = tiles[m + a] * wt
                                acc = term if acc is None else acc + term
                        if b == 0:
                            lo = flat + SUBLANES * m
                            ybuf[lo:lo + SUBLANES, lanes] = acc
                            continue
                        if m == n_tiles:
                            carried[b] = acc
                        if half is None:
                            half = acc
                        else:
                            zbufs[ti][stored:stored + 2 * SUBLANES, lanes] = (
                                jnp.concatenate([half, acc], axis=0).astype(BF16))
                            stored, half = stored + 2 * SUBLANES, None
                zbufs[ti][stored:stored + 2 * SUBLANES, lanes] = (
                    jnp.concatenate([half, jnp.zeros_like(half)], axis=0).astype(BF16))
                carried_for = (s, r0 + CONV_ROW_TILE)
        for ti, (s, r0, flat) in enumerate(row_tiles(rb)):
            rs = slice(flat, flat + CONV_ROW_TILE)
            conv = ybuf[rs, :] + _dot(shift_ref[...], zbufs[ti][...]) + bdw_ref[...]
            y = _layernorm(conv, lng_ref[...], lnb_ref[...])
            cbuf[rs, :] = (y * jax.nn.sigmoid(y)).astype(BF16)

    def pointwise_out(rb):
        flat = rb * CONV_ROW_BLOCK
        y = _dot(cbuf[flat:flat + CONV_ROW_BLOCK, :], w2_ref[...]) + b2_ref[...]
        x = block_rows(x_ref, rb)[...]
        block_rows(o_ref, rb)[...] = x + _rms(y, gout_ref[...]).reshape(x.shape)

    n_blocks = rows // CONV_ROW_BLOCK
    pointwise_in(0)
    for rb in range(1, n_blocks):
        depthwise(rb - 1)
        pointwise_in(rb)
        pointwise_out(rb - 1)
    depthwise(n_blocks - 1)
    pointwise_out(n_blocks - 1)

    last = gbuf[:, CONV_PAD + tm - CONV_HIST:CONV_PAD + tm, :]

    @pl.when(t == nt - 1)
    def _():
        nst_ref[...] = last

    if nt > 1:
        gbuf[:, hist0:CONV_PAD, :] = last


def _conv_shift_matrix():
    n_tiles = CONV_ROW_TILE // SUBLANES
    n_shift = SUBLANES - 1
    k_rows = -(-(n_shift * (n_tiles + 1) * SUBLANES) // (2 * SUBLANES)) * 2 * SUBLANES
    shift = np.zeros((CONV_ROW_TILE, k_rows), np.float32)
    for b in range(1, SUBLANES):
        for m in range(n_tiles + 1):
            for r in range(SUBLANES):
                i = SUBLANES * m + r - b
                if 0 <= i < CONV_ROW_TILE:
                    shift[i, ((b - 1) * (n_tiles + 1) + m) * SUBLANES + r] = 1.0
    return jnp.asarray(shift, BF16)


def _conv_layer(x, st, gin, gout, w1, b1, wdw, bdw, lng, lnb, w2, b2, *, bb, tm):
    B, T, _ = x.shape
    nt = T // tm
    assert tm >= CONV_HIST and tm % CONV_ROW_TILE == 0
    shift = _conv_shift_matrix()
    body = functools.partial(_conv_body, bb=bb, tm=tm, nt=nt)
    return pl.pallas_call(
        body,
        grid=(B // bb, nt),
        out_shape=(jax.ShapeDtypeStruct(x.shape, F32),
                   jax.ShapeDtypeStruct((B, CONV_HIST, D_MODEL), F32)),
        in_specs=[
            _seq_spec(bb, tm, D_MODEL),
            _state_spec(bb, CONV_HIST, D_MODEL),
            _const_spec((1, D_MODEL)),
            _const_spec((1, D_MODEL)),
            _const_spec(w1.shape),
            _const_spec(b1.shape),
            _const_spec(wdw.shape),
            _const_spec(bdw.shape),
            _const_spec(lng.shape),
            _const_spec(lnb.shape),
            _const_spec(w2.shape),
            _const_spec(b2.shape),
            _const_spec(shift.shape),
        ],
        out_specs=(_seq_spec(bb, tm, D_MODEL), _state_spec(bb, CONV_HIST, D_MODEL)),
        scratch_shapes=[
            pltpu.VMEM((bb, CONV_PAD + tm, D_MODEL), F32),
            pltpu.VMEM((bb * tm, D_MODEL), F32),
            pltpu.VMEM((bb * tm, D_MODEL), BF16),
        ] + [pltpu.VMEM((shift.shape[1], D_MODEL), BF16)] * (CONV_ROW_BLOCK // CONV_ROW_TILE),
        compiler_params=_params(),
        name="conv_mixer",
    )(x, st, gin, gout, w1, b1, wdw, bdw, lng, lnb, w2, b2, shift)


def _cmlp_body(x_ref, gin_ref, gout_ref, win_ref, bin_ref, lng_ref, lnb_ref, ws_ref, bs_ref,
               wout_ref, bout_ref, *rest, bb, tm, blk, emit_v):
    if emit_v:
        o_ref, v_ref, vbuf, ybuf = rest
    else:
        o_ref, vbuf, ybuf = rest
    rows = bb * tm
    x = x_ref[...].reshape(rows, D_MODEL)
    h = _rms(x, gin_ref[...]).astype(BF16)
    hv = _gelu(_dot(h, win_ref[:, CMLP_DIM:]) + bin_ref[:, CMLP_DIM:])
    vln = _layernorm(hv, lng_ref[...], lnb_ref[...])
    if emit_v:
        v_ref[...] = vln.reshape(bb, tm, CMLP_DIM)
    vbuf[...] = vln.astype(BF16)

    pos_r = lax.broadcasted_iota(jnp.int32, (blk, blk), 0) // CHUNK
    pos_c = lax.broadcasted_iota(jnp.int32, (blk, blk), 1) // CHUNK
    for g in range(CMLP_GROUPS):
        lanes = slice(g * CMLP_GROUP_DIM, (g + 1) * CMLP_GROUP_DIM)
        ug = _gelu(_dot(h, win_ref[:, lanes]) + bin_ref[:, lanes])
        wsm = jnp.where(pos_c <= pos_r, ws_ref[g], 0.0).astype(BF16)
        for n in range(rows // blk):
            rs = slice(n * blk, (n + 1) * blk)
            gate = _dot(wsm, vbuf[rs, lanes]) + bs_ref[g]
            ybuf[rs, lanes] = (ug[rs] * gate).astype(BF16)

    y = _dot(ybuf[...], wout_ref[...]) + bout_ref[...]
    o_ref[...] = (x + _rms(y, gout_ref[...])).reshape(bb, tm, D_MODEL)


def _cmlp_layer(x, gin, gout, win, bin_, lng, lnb, ws, bs, wout, bout, *, bb, tm, emit_v):
    B, T, _ = x.shape
    blk = min(T, MIX_BLOCK)
    assert tm % blk == 0
    ws = ws[:, :blk, :blk]
    bs = bs[:, :blk, None]
    body = functools.partial(_cmlp_body, bb=bb, tm=tm, blk=blk, emit_v=emit_v)
    out_shape = [jax.ShapeDtypeStruct(x.shape, F32)]
    out_specs = [_seq_spec(bb, tm, D_MODEL)]
    if emit_v:
        out_shape.append(jax.ShapeDtypeStruct((B, T, CMLP_DIM), F32))
        out_specs.append(_seq_spec(bb, tm, CMLP_DIM))
    return pl.pallas_call(
        body,
        grid=(B // bb, T // tm),
        out_shape=tuple(out_shape),
        in_specs=[
            _seq_spec(bb, tm, D_MODEL),
            _const_spec((1, D_MODEL)),
            _const_spec((1, D_MODEL)),
            _const_spec(win.shape),
            _const_spec(bin_.shape),
            _const_spec(lng.shape),
            _const_spec(lnb.shape),
            _const_spec(ws.shape),
            _const_spec(bs.shape),
            _const_spec(wout.shape),
            _const_spec(bout.shape),
        ],
        out_specs=tuple(out_specs),
        scratch_shapes=[
            pltpu.VMEM((bb * tm, CMLP_DIM), BF16),
            pltpu.VMEM((bb * tm, CMLP_DIM), BF16),
        ],
        compiler_params=_params(),
        name="cmlp_mixer",
    )(x, gin, gout, win, bin_, lng, lnb, ws, bs, wout, bout)


def _ffn_body(x_ref, st_ref, gin_ref, gout_ref, wup_ref, wdw_ref, bdw_ref, wdn_ref,
              o_ref, nst_ref, *scratch, bb, tm, nt):
    n_chunks = D_FF // FFN_CHUNK
    upbufs = scratch[:n_chunks]
    actbufs = scratch[n_chunks:]
    t = pl.program_id(1)
    rows = bb * tm
    hist0 = FFN_PAD - FFN_HIST

    def halves(c):
        return ((c * FFN_CHUNK, slice(0, FFN_CHUNK)),
                (D_FF + c * FFN_CHUNK, slice(FFN_CHUNK, 2 * FFN_CHUNK)))

    @pl.when(t == 0)
    def _():
        for c in range(n_chunks):
            upbufs[c][:, 0:hist0, :] = jnp.zeros((bb, hist0, 2 * FFN_CHUNK), F32)
            for lo, lanes in halves(c):
                upbufs[c][:, hist0:FFN_PAD, lanes] = st_ref[:, :, lo:lo + FFN_CHUNK]

    if bb >= FFN_ROW_SPLIT:
        sub_b, sub_t = bb // FFN_ROW_SPLIT, tm
        subs = [(slice(i * sub_b, (i + 1) * sub_b), 0) for i in range(FFN_ROW_SPLIT)]
    else:
        sub_b, sub_t = bb, tm // FFN_ROW_SPLIT
        subs = [(slice(0, bb), i * sub_t) for i in range(FFN_ROW_SPLIT)]
    sub_rows = sub_b * sub_t
    n_tiles = sub_t // SUBLANES
    tile_row = lax.broadcasted_iota(jnp.int32, (sub_b * n_tiles, SUBLANES, FFN_CHUNK), 1)

    def up_proj(h, sub, c):
        sl, r0 = sub
        for lo, lanes in halves(c):
            up = _dot(h, wup_ref[:, lo:lo + FFN_CHUNK])
            upbufs[c][sl, FFN_PAD + r0:FFN_PAD + r0 + sub_t, lanes] = up.reshape(sub_b, sub_t, FFN_CHUNK)

    def delayed(cur, hist):
        cur_r = pltpu.roll(cur, 1, 1)
        hist_r = pltpu.roll(hist, 1, 1)
        cur_r4 = cur_r.reshape(sub_b, n_tiles, SUBLANES, FFN_CHUNK)
        prev_r = jnp.concatenate([hist_r[:, None], cur_r4[:, :n_tiles - 1]], axis=1)
        out = jnp.where(tile_row < 1, prev_r.reshape(sub_b * n_tiles, SUBLANES, FFN_CHUNK), cur_r)
        return out, hist_r

    def conv(sub, c, lo, lanes):
        sl, r0 = sub
        w = wdw_ref[:, lo:lo + FFN_CHUNK]
        cur = upbufs[c][sl, FFN_PAD + r0:FFN_PAD + r0 + sub_t, lanes].reshape(
            sub_b * n_tiles, SUBLANES, FFN_CHUNK)
        hist = upbufs[c][sl, r0:r0 + FFN_PAD, lanes]
        acc, acc_hist = None, None
        for kk in range(FFN_HIST):
            term, term_hist = cur * w[kk:kk + 1], hist * w[kk:kk + 1]
            if acc is not None:
                term, term_hist = term + acc, term_hist + acc_hist
            acc, acc_hist = delayed(term, term_hist)
        out = cur * w[FFN_HIST:FFN_HIST + 1] + bdw_ref[:, lo:lo + FFN_CHUNK] + acc
        return out.reshape(sub_rows, FFN_CHUNK)

    def gate(sub, c, actbuf, col):
        (glo, glanes), (ulo, ulanes) = halves(c)
        act = _gelu(conv(sub, c, glo, glanes)) * conv(sub, c, ulo, ulanes)
        actbuf[:, col:col + FFN_CHUNK] = act.astype(BF16)

    groups, start = [], 0
    for size in FFN_DOWN_GROUPS:
        groups.append((start, start + size))
        start += size
    for si, sub in enumerate(subs):
        sl, r0 = sub
        x = x_ref[sl, r0:r0 + sub_t, :].reshape(sub_rows, D_MODEL)
        h = _rms(x, gin_ref[...]).astype(BF16)
        for gi, (c0, c1) in enumerate(groups):
            for c in range(c0, c1):
                up_proj(h, sub, c)
                gate(sub, c, actbufs[si * len(groups) + gi], (c - c0) * FFN_CHUNK)
        part_rows = sub_rows // FFN_DOWN_ROW_PARTS
        o3 = o_ref.at[sl, r0:r0 + sub_t, :]
        for rp in range(FFN_DOWN_ROW_PARTS):
            rs = slice(rp * part_rows, (rp + 1) * part_rows)
            y = None
            for gi, (c0, c1) in enumerate(groups):
                part = _dot(actbufs[si * len(groups) + gi][rs, :], wdn_ref[c0 * FFN_CHUNK:c1 * FFN_CHUNK, :])
                y = part if y is None else y + part
            out = x[rs] + _rms(y, gout_ref[...])
            if sub_b == 1:
                o3[0, rs, :] = out
            else:
                nb = part_rows // sub_t
                o3[rp * nb:(rp + 1) * nb] = out.reshape(nb, sub_t, D_MODEL)

    for c in range(n_chunks):
        for lo, lanes in halves(c):
            last = upbufs[c][:, FFN_PAD + tm - FFN_HIST:FFN_PAD + tm, lanes]

            @pl.when(t == nt - 1)
            def _(last=last, lo=lo):
                nst_ref[:, :, lo:lo + FFN_CHUNK] = last

            if nt > 1:
                upbufs[c][:, hist0:FFN_PAD, lanes] = last


def _ffn_layer(x, st, gin, gout, wup, wdw, bdw, wdn, *, layer, bb, tm):
    B, T, _ = x.shape
    nt = T // tm
    assert tm >= FFN_HIST and sum(FFN_DOWN_GROUPS) * FFN_CHUNK == D_FF
    body = functools.partial(_ffn_body, bb=bb, tm=tm, nt=nt)
    return pl.pallas_call(
        body,
        grid=(B // bb, nt),
        out_shape=(jax.ShapeDtypeStruct(x.shape, F32),
                   jax.ShapeDtypeStruct((B, FFN_HIST, 2 * D_FF), F32)),
        in_specs=[
            _seq_spec(bb, tm, D_MODEL),
            _state_spec(bb, FFN_HIST, 2 * D_FF),
            _const_spec((1, D_MODEL)),
            _const_spec((1, D_MODEL)),
            _layer_spec(wup, layer),
            _const_spec(wdw.shape),
            _const_spec(bdw.shape),
            _layer_spec(wdn, layer),
        ],
        out_specs=(_seq_spec(bb, tm, D_MODEL), _state_spec(bb, FFN_HIST, 2 * D_FF)),
        scratch_shapes=(
            [pltpu.VMEM((bb, FFN_PAD + tm, 2 * FFN_CHUNK), F32)] * (D_FF // FFN_CHUNK)
            + [pltpu.VMEM((bb * tm // FFN_ROW_SPLIT, size * FFN_CHUNK), BF16)
               for _ in range(FFN_ROW_SPLIT) for size in FFN_DOWN_GROUPS]),
        compiler_params=_params(),
        name="conv_ffn",
    )(x, st, gin, gout, wup, wdw, bdw, wdn)


def _trunk(x, caches, p, bias, *, bb, tm):
    prompt = caches is None
    B = x.shape[0]
    new_k, new_v, new_conv, new_cv, new_ffn = [], [], [], [], []
    for i in range(DEPTH):
        kind, j = i % N_MIXERS, i // N_MIXERS
        g = p['norm_gain'][i]
        gains = [g[n][None, :] for n in range(4)]
        if kind == 0:
            if prompt:
                kc = jnp.zeros((B, WINDOW, KV_DIM), F32)
                vc = kc
            else:
                kc = caches[0][j].reshape(B, WINDOW, KV_DIM)
                vc = caches[1][j].reshape(B, WINDOW, KV_DIM)
            x, k_rows, v_rows = _attn_layer(
                x, kc, vc, gains[0], gains[1], p['attn_w_qkv'], p['attn_b_qkv'][j][None, :],
                p['attn_w_o'], p['attn_b_o'][j][None, :], p['attn_sinks'][j], bias, layer=j,
                bb=bb, tm=tm, mask_start=prompt)
            new_k.append(k_rows.reshape(B, -1, N_KV_HEADS, HEAD_DIM))
            new_v.append(v_rows.reshape(B, -1, N_KV_HEADS, HEAD_DIM))
        elif kind == 1:
            st = jnp.zeros((B, CONV_HIST, D_MODEL), F32) if prompt else caches[2][j]
            x, s = _conv_layer(
                x, st, gains[0], gains[1], p['conv_w_pw1'][j], p['conv_b_pw1'][j][None, :],
                p['conv_w_dw'][j], p['conv_b_dw'][j][None, :], p['conv_ln_g'][j][None, :],
                p['conv_ln_b'][j][None, :], p['conv_w_pw2'][j], p['conv_b_pw2'][j][None, :],
                bb=bb, tm=tm)
            new_conv.append(s)
        else:
            res = _cmlp_layer(
                x, gains[0], gains[1], p['cmlp_w_in'][j], p['cmlp_b_in'][j][None, :],
                p['cmlp_ln_g'][j][None, :], p['cmlp_ln_b'][j][None, :], p['cmlp_w_s'][j],
                p['cmlp_b_s'][j], p['cmlp_w_out'][j], p['cmlp_b_out'][j][None, :],
                bb=bb, tm=tm, emit_v=not prompt)
            x = res[0]
            if not prompt:
                new_cv.append(res[1])
        st = jnp.zeros((B, FFN_HIST, 2 * D_FF), F32) if prompt else caches[3][i]
        x, s = _ffn_layer(x, st, gains[2], gains[3], p['ffn_w_up'], p['ffn_w_dw'][i],
                          p['ffn_b_dw'][i][None, :], p['ffn_w_down'], layer=i, bb=bb, tm=tm)
        new_ffn.append(s)
    cv = jnp.stack(new_cv) if new_cv else None
    return x, jnp.stack(new_k), jnp.stack(new_v), jnp.stack(new_conv), cv, jnp.stack(new_ffn)


def kernel(x_prompt, x_sample, cache_attn_k, cache_attn_v, state_conv, state_ffn_conv, rel_bias_table, norm_gain, attn_w_qkv, attn_b_qkv, attn_w_o, attn_b_o, attn_sinks, conv_w_pw1, conv_b_pw1, conv_w_dw, conv_b_dw, conv_ln_g, conv_ln_b, conv_w_pw2, conv_b_pw2, cmlp_w_in, cmlp_b_in, cmlp_ln_g, cmlp_ln_b, cmlp_w_s, cmlp_b_s, cmlp_w_out, cmlp_b_out, ffn_w_up, ffn_w_dw, ffn_b_dw, ffn_w_down):
    p = {
        'norm_gain': norm_gain,
        'attn_w_qkv': attn_w_qkv.astype(BF16), 'attn_b_qkv': attn_b_qkv,
        'attn_w_o': attn_w_o.astype(BF16), 'attn_b_o': attn_b_o, 'attn_sinks': attn_sinks,
        'conv_w_pw1': conv_w_pw1.astype(BF16), 'conv_b_pw1': conv_b_pw1, 'conv_w_dw': conv_w_dw,
        'conv_b_dw': conv_b_dw, 'conv_ln_g': conv_ln_g, 'conv_ln_b': conv_ln_b,
        'conv_w_pw2': conv_w_pw2.astype(BF16), 'conv_b_pw2': conv_b_pw2,
        'cmlp_w_in': cmlp_w_in.astype(BF16), 'cmlp_b_in': cmlp_b_in, 'cmlp_ln_g': cmlp_ln_g,
        'cmlp_ln_b': cmlp_ln_b, 'cmlp_w_s': cmlp_w_s, 'cmlp_b_s': cmlp_b_s,
        'cmlp_w_out': cmlp_w_out.astype(BF16), 'cmlp_b_out': cmlp_b_out,
        'ffn_w_up': ffn_w_up.astype(BF16), 'ffn_w_dw': ffn_w_dw, 'ffn_b_dw': ffn_b_dw,
        'ffn_w_down': ffn_w_down.astype(BF16),
    }
    bias = _rel_bias(rel_bias_table)
    y_prompt, p_attn_k, p_attn_v, p_conv, _, p_ffn_conv = _trunk(
        x_prompt, None, p, bias, bb=1, tm=512)
    y_sample, s_attn_k, s_attn_v, s_conv, s_cmlp_v, s_ffn_conv = _trunk(
        x_sample, (cache_attn_k, cache_attn_v, state_conv, state_ffn_conv), p, bias, bb=8, tm=64)
    return (y_prompt, y_sample, p_attn_k, p_attn_v, p_conv, p_ffn_conv,
            s_attn_k, s_attn_v, s_conv, s_cmlp_v, s_ffn_conv)
```

```python
import functools
import math

import jax
import jax.numpy as jnp
from jax import lax
from jax.experimental import pallas as pl
from jax.experimental.pallas import tpu as pltpu

D_MODEL = 1024
DEPTH = 4
CHUNK = 64
N_MIXERS = 3
N_HEADS = 16
N_KV_HEADS = 4
HEAD_DIM = 64
GQA_GROUP = N_HEADS // N_KV_HEADS
WINDOW = 128
KV_DIM = N_KV_HEADS * HEAD_DIM
Q_DIM = N_HEADS * HEAD_DIM
NUM_BUCKETS = 32
MAX_DISTANCE = 128
CONV_WIDTH = 31
CONV_HIST = CONV_WIDTH - 1
MIX_BLOCK = 128
CMLP_GROUPS = 4
CMLP_DIM = 2 * D_MODEL
CMLP_GROUP_DIM = CMLP_DIM // CMLP_GROUPS
D_FF = 2816
FFN_CONV_WIDTH = 3
FFN_HIST = FFN_CONV_WIDTH - 1
RMS_EPS = 1e-6
_GELU_C0 = math.sqrt(2.0 / math.pi)
_GELU_C1 = _GELU_C0 * 0.044715
LN_EPS = 1e-5

F32 = jnp.float32
BF16 = jnp.bfloat16

V7X_VMEM_BYTES = 64 * 1024 * 1024
VMEM_LIMIT_BYTES = V7X_VMEM_BYTES - 8 * 1024 * 1024
SUBLANES = 8
FFN_CHUNK = 256
FFN_DOWN_ROW_PARTS = 2
FFN_PAD = SUBLANES
CONV_PAD = 32
CONV_ROW_TILE = 64
CONV_LANE_TILE = 128
CONV_ROW_BLOCK = 256


def _rms(x, g):
    ms = jnp.mean(x * x, axis=-1, keepdims=True)
    return x * lax.rsqrt(ms + RMS_EPS) * g


def _layernorm(x, g, b):
    mu = jnp.mean(x, axis=-1, keepdims=True)
    xc = x - mu
    var = jnp.mean(xc * xc, axis=-1, keepdims=True)
    return xc * lax.rsqrt(var + LN_EPS) * g + b


def _gelu(x):
    hx = 0.5 * x
    return hx + hx * jnp.tanh(x * (_GELU_C0 + _GELU_C1 * (x * x)))


def _dot(a, b):
    return jnp.dot(a, b, preferred_element_type=F32)


def _const_spec(shape):
    zeros = (0,) * len(shape)
    return pl.BlockSpec(shape, lambda b, t: zeros, pipeline_mode=pl.Buffered(1))


def _layer_spec(stacked, layer):
    zeros = (0,) * (stacked.ndim - 1)
    return pl.BlockSpec((None,) + stacked.shape[1:], lambda b, t: (layer,) + zeros,
                        pipeline_mode=pl.Buffered(1))


def _seq_spec(bb, rows, width):
    return pl.BlockSpec((bb, rows, width), lambda b, t: (b, t, 0))


def _state_spec(bb, rows, width):
    return pl.BlockSpec((bb, rows, width), lambda b, t: (b, 0, 0))


def _params(flags=None):
    return pltpu.CompilerParams(
        dimension_semantics=("arbitrary", "arbitrary"),
        vmem_limit_bytes=VMEM_LIMIT_BYTES,
        flags=flags,
    )


HEAD_PAIR = 2 * HEAD_DIM
ATTN_LOOKAHEAD = 2
ATTN_CHUNKS_PER_UNIT = 2
ATTN_ROW_SPLIT = 2
PAIR_STEP = GQA_GROUP // 2


def _pair_heads(j, v):
    return GQA_GROUP * j + v, GQA_GROUP * j + PAIR_STEP + v


def _bias_body(bucket_ref, table_ref, o_ref):
    bk = bucket_ref[...]
    low = lax.broadcasted_iota(jnp.int32, (1, HEAD_PAIR), 1) < CHUNK
    for j in range(N_KV_HEADS):
        for v in range(2):
            ha, hb = _pair_heads(j, v)
            acc = jnp.zeros(bk.shape, F32)
            for b in range(NUM_BUCKETS):
                val = jnp.where(low, table_ref[b, ha], table_ref[b, hb])
                acc = jnp.where(bk == b, val, acc)
            o_ref[j, v] = acc


def _t5_bucket(rel):
    half = NUM_BUCKETS // 2
    max_exact = half // 2
    n = jnp.abs(rel)
    log_ratio = jnp.log(jnp.maximum(n, 1).astype(F32) / max_exact) / math.log(MAX_DISTANCE / max_exact)
    large = jnp.minimum(max_exact + (log_ratio * (half - max_exact)).astype(jnp.int32), half - 1)
    return jnp.where(rel > 0, half, 0) + jnp.where(n < max_exact, n, large)


def _rel_bias(table):
    q_pos = jnp.arange(CHUNK, dtype=jnp.int32)
    k_pos = jnp.arange(WINDOW + CHUNK, dtype=jnp.int32) - WINDOW
    bucket = _t5_bucket(k_pos[:, None] - q_pos[None, :]).astype(jnp.int32)
    bucket = jnp.concatenate([bucket, bucket], axis=1)
    return pl.pallas_call(
        _bias_body,
        out_shape=jax.ShapeDtypeStruct((N_KV_HEADS, 2, WINDOW + CHUNK, HEAD_PAIR), F32),
        in_specs=[pl.BlockSpec(memory_space=pltpu.VMEM), pl.BlockSpec(memory_space=pltpu.SMEM)],
        out_specs=pl.BlockSpec(memory_space=pltpu.VMEM),
        name="rel_bias",
    )(bucket, table)


def _attn_body(x_ref, kc_ref, vc_ref, gin_ref, gout_ref, wqkv_ref, bqkv_ref, wo_ref, bo_ref,
               sinks_ref, bias_ref, o_ref, kout_ref, vout_ref, qbuf, kzbuf, vzbuf, obuf, ybuf,
               *, bb, tm, nt, mask_start):
    t = pl.program_id(1)
    rows = bb * tm
    kv_rows = min(WINDOW, tm)
    span = WINDOW + CHUNK
    low = lax.broadcasted_iota(jnp.int32, (1, HEAD_PAIR), 1) < CHUNK

    def expand(buf, sl, r0, kv):
        nb, n = kv.shape[0], kv.shape[1]
        for pair in range(N_KV_HEADS // 2):
            tile = kv[:, :, pair * HEAD_PAIR:(pair + 1) * HEAD_PAIR]
            swapped = pltpu.roll(tile.reshape(nb * n, HEAD_PAIR), HEAD_DIM, 1).reshape(nb, n, HEAD_PAIR)
            for j, (lo_src, hi_src) in ((2 * pair, (tile, swapped)), (2 * pair + 1, (swapped, tile))):
                buf[sl, j, 0, r0:r0 + n, :] = jnp.where(low, lo_src, 0.0).astype(BF16)
                buf[sl, j, 1, r0:r0 + n, :] = jnp.where(low, 0.0, hi_src).astype(BF16)

    @pl.when(t == 0)
    def _():
        expand(kzbuf, slice(0, bb), 0, kc_ref[...])
        expand(vzbuf, slice(0, bb), 0, vc_ref[...])

    if bb >= ATTN_ROW_SPLIT:
        sub_b, sub_t = bb // ATTN_ROW_SPLIT, tm
        subs = [(slice(i * sub_b, (i + 1) * sub_b), 0) for i in range(ATTN_ROW_SPLIT)]
    else:
        sub_b, sub_t = bb, tm // ATTN_ROW_SPLIT
        subs = [(slice(0, bb), i * sub_t) for i in range(ATTN_ROW_SPLIT)]
    sub_rows = sub_b * sub_t
    assert sub_t % CHUNK == 0 and (sub_b > 1 or sub_t >= kv_rows or ATTN_ROW_SPLIT == 1)
    n_piece = GQA_GROUP * HEAD_DIM

    def flat0(sub):
        sl, r0 = sub
        return sl.start * tm + r0

    def load_x(sub):
        sl, r0 = sub
        return x_ref[sl, r0:r0 + sub_t, :].reshape(sub_rows, D_MODEL)

    def in_pieces(sub):
        sl, r0 = sub
        h = _rms(load_x(sub), gin_ref[...]).astype(BF16)
        f0 = flat0(sub)

        def kv_piece(col, buf, out_ref):
            def run():
                kv = (_dot(h, wqkv_ref[:, col:col + KV_DIM]) + bqkv_ref[:, col:col + KV_DIM]).reshape(
                    sub_b, sub_t, KV_DIM)
                expand(buf, sl, WINDOW + r0, kv)
                if sub_b > 1 or r0 + sub_t == tm:
                    out_ref[sl] = kv[:, sub_t - kv_rows:, :]
            return run

        def q_piece(j):
            def run():
                col = j * n_piece
                q = _dot(h, wqkv_ref[:, col:col + n_piece]) + bqkv_ref[:, col:col + n_piece]
                qbuf[f0:f0 + sub_rows, col:col + n_piece] = (q * (HEAD_DIM ** -0.5)).astype(BF16)
            return run

        return ([kv_piece(Q_DIM, kzbuf, kout_ref), kv_piece(Q_DIM + KV_DIM, vzbuf, vout_ref)]
                + [q_piece(j) for j in range(N_KV_HEADS)])

    def out_pieces(sub):
        sl, r0 = sub
        f0 = flat0(sub)

        def piece(n):
            def run():
                col = n * n_piece
                ybuf[f0:f0 + sub_rows, col:col + n_piece] = (
                    _dot(obuf[f0:f0 + sub_rows, :], wo_ref[:, col:col + n_piece]) + bo_ref[:, col:col + n_piece])
            return run

        def finish():
            y = ybuf[f0:f0 + sub_rows, :]
            o_ref[sl, r0:r0 + sub_t, :] = (load_x(sub) + _rms(y, gout_ref[...])).reshape(sub_b, sub_t, D_MODEL)

        return [piece(n) for n in range(D_MODEL // n_piece)] + [finish]

    key_row = lax.broadcasted_iota(jnp.int32, (span, HEAD_PAIR), 0)
    first_valid = jnp.where(t == 0, WINDOW, 0)
    contract_last = (((1,), (1,)), ((), ()))
    contract_first = (((0,), (0,)), ((), ()))

    def scores(s, chunks, j):
        g0 = j * GQA_GROUP * HEAD_DIM
        r0 = chunks[0] * CHUNK
        n_keys = span + (len(chunks) - 1) * CHUNK
        q_parts = []
        for c in chunks:
            q_rows = slice(s * tm + c * CHUNK, s * tm + (c + 1) * CHUNK)
            q_parts += [qbuf[q_rows, g0:g0 + HEAD_PAIR], qbuf[q_rows, g0 + HEAD_PAIR:g0 + 2 * HEAD_PAIR]]
        keys = jnp.concatenate([kzbuf[s, j, v_idx, r0:r0 + n_keys, :] for v_idx in range(2)], axis=0)
        sc_all = lax.dot_general(keys, jnp.concatenate(q_parts, axis=0), contract_last,
                                 preferred_element_type=F32)
        out = []
        for n, c in enumerate(chunks):
            tiles = []
            for v_idx in range(2):
                k0 = v_idx * n_keys + n * CHUNK
                sc = sc_all[k0:k0 + span, n * HEAD_PAIR:(n + 1) * HEAD_PAIR] + bias_ref[j, v_idx]
                if mask_start and c * CHUNK < WINDOW:
                    sc = jnp.where(key_row + c * CHUNK >= first_valid, sc, -jnp.inf)
                tiles.append(sc)
            out.append((c, tiles))
        return out

    def attend(s, c, j, scs):
        r0 = c * CHUNK
        q_rows = slice(s * tm + r0, s * tm + r0 + CHUNK)
        g0 = j * GQA_GROUP * HEAD_DIM
        probs = []
        for v_idx, sc in enumerate(scs):
            ha, hb = _pair_heads(j, v_idx)
            sink = jnp.where(low, sinks_ref[ha], sinks_ref[hb])
            m = jnp.maximum(jnp.max(sc, axis=0, keepdims=True), sink)
            p = jnp.exp(sc - m)
            denom = jnp.sum(p, axis=0, keepdims=True) + jnp.exp(sink - m)
            probs.append((p * (1.0 / denom)).astype(BF16))
        p_kn = jnp.concatenate(probs, axis=0)
        v_kd = jnp.concatenate([vzbuf[s, j, 0, r0:r0 + span, :],
                                vzbuf[s, j, 1, r0:r0 + span, :]], axis=0)
        o = lax.dot_general(p_kn, v_kd, contract_first, preferred_element_type=F32)
        obuf[q_rows, g0:g0 + HEAD_PAIR] = o[:CHUNK].astype(BF16)
        obuf[q_rows, g0 + HEAD_PAIR:g0 + 2 * HEAD_PAIR] = o[CHUNK:].astype(BF16)

    def units_of(sub):
        sl, r0 = sub
        c0, c1 = r0 // CHUNK, (r0 + sub_t) // CHUNK
        step = ATTN_CHUNKS_PER_UNIT if (c1 - c0) % ATTN_CHUNKS_PER_UNIT == 0 else 1
        return [(s, tuple(range(c, c + step)), j) for s in range(sl.start, sl.stop)
                for c in range(c0, c1, step) for j in range(N_KV_HEADS)]

    def run_units(units, fillers):
        every = max(1, len(units) // max(1, len(fillers)))
        pending = []

        def retire():
            (s, _, j), per_chunk = pending.pop(0)
            for c, tiles in per_chunk:
                attend(s, c, j, tiles)

        for n, unit in enumerate(units):
            pending.append((unit, scores(*unit)))
            if len(pending) > ATTN_LOOKAHEAD:
                retire()
            if fillers and (n + 1) % every == 0:
                fillers.pop(0)()
        while pending:
            retire()
        for filler in fillers:
            filler()

    for piece in in_pieces(subs[0]):
        piece()
    for i, sub in enumerate(subs):
        fillers = []
        if i > 0:
            fillers += out_pieces(subs[i - 1])
        if i + 1 < len(subs):
            fillers += in_pieces(subs[i + 1])
        run_units(units_of(sub), fillers)
    for piece in out_pieces(subs[-1]):
        piece()

    if nt > 1:
        kzbuf[:, :, :, 0:WINDOW, :] = kzbuf[:, :, :, tm:tm + WINDOW, :]
        vzbuf[:, :, :, 0:WINDOW, :] = vzbuf[:, :, :, tm:tm + WINDOW, :]


def _attn_layer(x, kc, vc, gin, gout, wqkv, bqkv, wo, bo, sinks, bias, *, layer, bb, tm, mask_start):
    B, T, _ = x.shape
    nt = T // tm
    kv_rows = min(WINDOW, tm)
    body = functools.partial(_attn_body, bb=bb, tm=tm, nt=nt, mask_start=mask_start)
    return pl.pallas_call(
        body,
        grid=(B // bb, nt),
        out_shape=(jax.ShapeDtypeStruct(x.shape, F32),
                   jax.ShapeDtypeStruct((B, kv_rows, KV_DIM), F32),
                   jax.ShapeDtypeStruct((B, kv_rows, KV_DIM), F32)),
        in_specs=[
            _seq_spec(bb, tm, D_MODEL),
            _state_spec(bb, WINDOW, KV_DIM),
            _state_spec(bb, WINDOW, KV_DIM),
            _const_spec((1, D_MODEL)),
            _const_spec((1, D_MODEL)),
            _layer_spec(wqkv, layer),
            _const_spec(bqkv.shape),
            _layer_spec(wo, layer),
            _const_spec(bo.shape),
            pl.BlockSpec(memory_space=pltpu.SMEM),
            _const_spec(bias.shape),
        ],
        out_specs=(_seq_spec(bb, tm, D_MODEL),
                   _state_spec(bb, kv_rows, KV_DIM),
                   _state_spec(bb, kv_rows, KV_DIM)),
        scratch_shapes=[
            pltpu.VMEM((bb * tm, Q_DIM), BF16),
            pltpu.VMEM((bb, N_KV_HEADS, 2, WINDOW + tm, HEAD_PAIR), BF16),
            pltpu.VMEM((bb, N_KV_HEADS, 2, WINDOW + tm, HEAD_PAIR), BF16),
            pltpu.VMEM((bb * tm, Q_DIM), BF16),
            pltpu.VMEM((bb * tm, D_MODEL), F32),
        ],
        compiler_params=_params(),
        name="attn_mixer",
    )(x, kc, vc, gin, gout, wqkv, bqkv, wo, bo, sinks, bias)


def _conv_body(x_ref, st_ref, gin_ref, gout_ref, w1_ref, b1_ref, wdw_ref, bdw_ref, lng_ref, lnb_ref,
               w2_ref, b2_ref, o_ref, nst_ref, gbuf, ybuf, cbuf, *, bb, tm, nt):
    t = pl.program_id(1)
    rows = bb * tm
    hist0 = CONV_PAD - CONV_HIST

    @pl.when(t == 0)
    def _():
        gbuf[:, hist0:CONV_PAD, :] = st_ref[...]

    row = lax.broadcasted_iota(jnp.int32, (SUBLANES, CONV_LANE_TILE), 0)
    n_tiles = CONV_ROW_TILE // SUBLANES
    max_a = (hist0 + CONV_WIDTH - 1) // SUBLANES
    tiles_per_block = CONV_ROW_BLOCK // CONV_ROW_TILE

    def row_tiles(rb):
        out = []
        for i in range(tiles_per_block):
            flat = rb * CONV_ROW_BLOCK + i * CONV_ROW_TILE
            out.append((flat // tm, flat % tm, flat))
        return out

    def block_rows(ref, rb):
        flat = rb * CONV_ROW_BLOCK
        if tm >= CONV_ROW_BLOCK:
            return ref.at[flat // tm, flat % tm:flat % tm + CONV_ROW_BLOCK, :]
        return ref.at[flat // tm:(flat + CONV_ROW_BLOCK) // tm, :, :]

    def pointwise_in(rb):
        x = block_rows(x_ref, rb)[...].reshape(CONV_ROW_BLOCK, D_MODEL)
        h = _rms(x, gin_ref[...]).astype(BF16)
        ag = _dot(h, w1_ref[...]) + b1_ref[...]
        glu = ag[:, :D_MODEL] * jax.nn.sigmoid(ag[:, D_MODEL:])
        for i, (s, r0, _) in enumerate(row_tiles(rb)):
            gbuf[s, CONV_PAD + r0:CONV_PAD + r0 + CONV_ROW_TILE, :] = (
                glu[i * CONV_ROW_TILE:(i + 1) * CONV_ROW_TILE])

    def depthwise(rb):
        for l0 in range(0, D_MODEL, CONV_LANE_TILE):
            lanes = slice(l0, l0 + CONV_LANE_TILE)
            carried, carried_for = {}, None
            for s, r0, flat in row_tiles(rb):
                if carried_for != (s, r0):
                    carried = {}
                tiles = [gbuf[s, r0 + SUBLANES * j:r0 + SUBLANES * (j + 1), lanes]
                         for j in range(n_tiles + max_a)]
                out = [None] * n_tiles
                for b in range(SUBLANES):
                    taps = [a for a in range(max_a + 1)
                            if 0 <= SUBLANES * a + b - hist0 < CONV_WIDTH]
                    wts = [wdw_ref[SUBLANES * a + b - hist0:SUBLANES * a + b - hist0 + 1, lanes]
                           for a in taps]
                    z = []
                    for m in range(n_tiles + (1 if b else 0)):
                        if m == 0 and b in carried:
                            z.append(carried[b])
                            continue
                        acc = None
                        for a, wt in zip(taps, wts):
                            term = tiles[m + a] * wt
                            acc = term if acc is None else acc + term
                        z.append(acc)
                    if b:
                        carried[b] = z[n_tiles]
                    for m in range(n_tiles):
                        if b == 0:
                            part = z[m]
                        else:
                            part = pltpu.roll(jnp.where(row >= b, z[m], z[m + 1]), SUBLANES - b, 0)
                        out[m] = part if out[m] is None else out[m] + part
                carried_for = (s, r0 + CONV_ROW_TILE)
                for m in range(n_tiles):
                    lo = flat + SUBLANES * m
                    ybuf[lo:lo + SUBLANES, lanes] = out[m]
        for s, r0, flat in row_tiles(rb):
            rs = slice(flat, flat + CONV_ROW_TILE)
            y = _layernorm(ybuf[rs, :] + bdw_ref[...], lng_ref[...], lnb_ref[...])
            cbuf[rs, :] = (y * jax.nn.sigmoid(y)).astype(BF16)

    def pointwise_out(rb):
        flat = rb * CONV_ROW_BLOCK
        y = _dot(cbuf[flat:flat + CONV_ROW_BLOCK, :], w2_ref[...]) + b2_ref[...]
        x = block_rows(x_ref, rb)[...]
        block_rows(o_ref, rb)[...] = x + _rms(y, gout_ref[...]).reshape(x.shape)

    n_blocks = rows // CONV_ROW_BLOCK
    pointwise_in(0)
    for rb in range(1, n_blocks):
        depthwise(rb - 1)
        pointwise_in(rb)
        pointwise_out(rb - 1)
    depthwise(n_blocks - 1)
    pointwise_out(n_blocks - 1)

    last = gbuf[:, CONV_PAD + tm - CONV_HIST:CONV_PAD + tm, :]

    @pl.when(t == nt - 1)
    def _():
        nst_ref[...] = last

    if nt > 1:
        gbuf[:, hist0:CONV_PAD, :] = last


def _conv_layer(x, st, gin, gout, w1, b1, wdw, bdw, lng, lnb, w2, b2, *, bb, tm):
    B, T, _ = x.shape
    nt = T // tm
    assert tm >= CONV_HIST and tm % CONV_ROW_TILE == 0
    body = functools.partial(_conv_body, bb=bb, tm=tm, nt=nt)
    return pl.pallas_call(
        body,
        grid=(B // bb, nt),
        out_shape=(jax.ShapeDtypeStruct(x.shape, F32),
                   jax.ShapeDtypeStruct((B, CONV_HIST, D_MODEL), F32)),
        in_specs=[
            _seq_spec(bb, tm, D_MODEL),
            _state_spec(bb, CONV_HIST, D_MODEL),
            _const_spec((1, D_MODEL)),
            _const_spec((1, D_MODEL)),
            _const_spec(w1.shape),
            _const_spec(b1.shape),
            _const_spec(wdw.shape),
            _const_spec(bdw.shape),
            _const_spec(lng.shape),
            _const_spec(lnb.shape),
            _const_spec(w2.shape),
            _const_spec(b2.shape),
        ],
        out_specs=(_seq_spec(bb, tm, D_MODEL), _state_spec(bb, CONV_HIST, D_MODEL)),
        scratch_shapes=[
            pltpu.VMEM((bb, CONV_PAD + tm, D_MODEL), F32),
            pltpu.VMEM((bb * tm, D_MODEL), F32),
            pltpu.VMEM((bb * tm, D_MODEL), BF16),
        ],
        compiler_params=_params(),
        name="conv_mixer",
    )(x, st, gin, gout, w1, b1, wdw, bdw, lng, lnb, w2, b2)


def _cmlp_body(x_ref, gin_ref, gout_ref, win_ref, bin_ref, lng_ref, lnb_ref, ws_ref, bs_ref,
               wout_ref, bout_ref, *rest, bb, tm, blk, emit_v):
    if emit_v:
        o_ref, v_ref, vbuf, ybuf = rest
    else:
        o_ref, vbuf, ybuf = rest
    rows = bb * tm
    x = x_ref[...].reshape(rows, D_MODEL)
    h = _rms(x, gin_ref[...]).astype(BF16)
    hv = _gelu(_dot(h, win_ref[:, CMLP_DIM:]) + bin_ref[:, CMLP_DIM:])
    vln = _layernorm(hv, lng_ref[...], lnb_ref[...])
    if emit_v:
        v_ref[...] = vln.reshape(bb, tm, CMLP_DIM)
    vbuf[...] = vln.astype(BF16)

    pos_r = lax.broadcasted_iota(jnp.int32, (blk, blk), 0) // CHUNK
    pos_c = lax.broadcasted_iota(jnp.int32, (blk, blk), 1) // CHUNK
    for g in range(CMLP_GROUPS):
        lanes = slice(g * CMLP_GROUP_DIM, (g + 1) * CMLP_GROUP_DIM)
        ug = _gelu(_dot(h, win_ref[:, lanes]) + bin_ref[:, lanes])
        wsm = jnp.where(pos_c <= pos_r, ws_ref[g], 0.0).astype(BF16)
        for n in range(rows // blk):
            rs = slice(n * blk, (n + 1) * blk)
            gate = _dot(wsm, vbuf[rs, lanes]) + bs_ref[g]
            ybuf[rs, lanes] = (ug[rs] * gate).astype(BF16)

    y = _dot(ybuf[...], wout_ref[...]) + bout_ref[...]
    o_ref[...] = (x + _rms(y, gout_ref[...])).reshape(bb, tm, D_MODEL)


def _cmlp_layer(x, gin, gout, win, bin_, lng, lnb, ws, bs, wout, bout, *, bb, tm, emit_v):
    B, T, _ = x.shape
    blk = min(T, MIX_BLOCK)
    assert tm % blk == 0
    ws = ws[:, :blk, :blk]
    bs = bs[:, :blk, None]
    body = functools.partial(_cmlp_body, bb=bb, tm=tm, blk=blk, emit_v=emit_v)
    out_shape = [jax.ShapeDtypeStruct(x.shape, F32)]
    out_specs = [_seq_spec(bb, tm, D_MODEL)]
    if emit_v:
        out_shape.append(jax.ShapeDtypeStruct((B, T, CMLP_DIM), F32))
        out_specs.append(_seq_spec(bb, tm, CMLP_DIM))
    return pl.pallas_call(
        body,
        grid=(B // bb, T // tm),
        out_shape=tuple(out_shape),
        in_specs=[
            _seq_spec(bb, tm, D_MODEL),
            _const_spec((1, D_MODEL)),
            _const_spec((1, D_MODEL)),
            _const_spec(win.shape),
            _const_spec(bin_.shape),
            _const_spec(lng.shape),
            _const_spec(lnb.shape),
            _const_spec(ws.shape),
            _const_spec(bs.shape),
            _const_spec(wout.shape),
            _const_spec(bout.shape),
        ],
        out_specs=tuple(out_specs),
        scratch_shapes=[
            pltpu.VMEM((bb * tm, CMLP_DIM), BF16),
            pltpu.VMEM((bb * tm, CMLP_DIM), BF16),
        ],
        compiler_params=_params(),
        name="cmlp_mixer",
    )(x, gin, gout, win, bin_, lng, lnb, ws, bs, wout, bout)


def _ffn_body(x_ref, st_ref, gin_ref, gout_ref, wup_ref, wdw_ref, bdw_ref, wdn_ref,
              o_ref, nst_ref, actbuf, *upbufs, bb, tm, nt):
    n_chunks = D_FF // FFN_CHUNK
    t = pl.program_id(1)
    rows = bb * tm
    hist0 = FFN_PAD - FFN_HIST
    n_tiles = tm // SUBLANES
    tile_row = lax.broadcasted_iota(jnp.int32, (bb * n_tiles, SUBLANES, FFN_CHUNK), 1)

    def halves(c):
        return ((c * FFN_CHUNK, slice(0, FFN_CHUNK)),
                (D_FF + c * FFN_CHUNK, slice(FFN_CHUNK, 2 * FFN_CHUNK)))

    @pl.when(t == 0)
    def _():
        for c in range(n_chunks):
            upbufs[c][:, 0:hist0, :] = jnp.zeros((bb, hist0, 2 * FFN_CHUNK), F32)
            for lo, lanes in halves(c):
                upbufs[c][:, hist0:FFN_PAD, lanes] = st_ref[:, :, lo:lo + FFN_CHUNK]

    def up_proj(h, c):
        for lo, lanes in halves(c):
            up = _dot(h, wup_ref[:, lo:lo + FFN_CHUNK])
            upbufs[c][:, FFN_PAD:FFN_PAD + tm, lanes] = up.reshape(bb, tm, FFN_CHUNK)

    def delayed(cur, hist):
        cur_r = pltpu.roll(cur, 1, 1)
        hist_r = pltpu.roll(hist, 1, 1)
        cur_r4 = cur_r.reshape(bb, n_tiles, SUBLANES, FFN_CHUNK)
        prev_r = jnp.concatenate([hist_r[:, None], cur_r4[:, :n_tiles - 1]], axis=1)
        out = jnp.where(tile_row < 1, prev_r.reshape(bb * n_tiles, SUBLANES, FFN_CHUNK), cur_r)
        return out, hist_r

    def conv(c, lo, lanes):
        w = wdw_ref[:, lo:lo + FFN_CHUNK]
        cur = upbufs[c][:, FFN_PAD:FFN_PAD + tm, lanes].reshape(bb * n_tiles, SUBLANES, FFN_CHUNK)
        hist = upbufs[c][:, 0:FFN_PAD, lanes]
        acc, acc_hist = None, None
        for kk in range(FFN_HIST):
            term, term_hist = cur * w[kk:kk + 1], hist * w[kk:kk + 1]
            if acc is not None:
                term, term_hist = term + acc, term_hist + acc_hist
            acc, acc_hist = delayed(term, term_hist)
        out = cur * w[FFN_HIST:FFN_HIST + 1] + bdw_ref[:, lo:lo + FFN_CHUNK] + acc
        return out.reshape(rows, FFN_CHUNK)

    x = x_ref[...].reshape(rows, D_MODEL)
    h = _rms(x, gin_ref[...]).astype(BF16)
    for c in range(n_chunks):
        up_proj(h, c)
        (glo, glanes), (ulo, ulanes) = halves(c)
        act = _gelu(conv(c, glo, glanes)) * conv(c, ulo, ulanes)
        actbuf[:, c * FFN_CHUNK:(c + 1) * FFN_CHUNK] = act.astype(BF16)

    part_rows = rows // FFN_DOWN_ROW_PARTS
    for rp in range(FFN_DOWN_ROW_PARTS):
        rs = slice(rp * part_rows, (rp + 1) * part_rows)
        out = x[rs] + _rms(_dot(actbuf[rs, :], wdn_ref[...]), gout_ref[...])
        if bb == 1:
            o_ref[0, rs, :] = out
        else:
            nb = part_rows // tm
            o_ref[rp * nb:(rp + 1) * nb] = out.reshape(nb, tm, D_MODEL)

    lasts = [(lo, lanes, upbufs[c][:, FFN_PAD + tm - FFN_HIST:FFN_PAD + tm, lanes])
             for c in range(n_chunks) for lo, lanes in halves(c)]
    if nt > 1:
        for i, (lo, lanes, last) in enumerate(lasts):
            upbufs[i // 2][:, hist0:FFN_PAD, lanes] = last

    @pl.when(t == nt - 1)
    def _():
        for lo, lanes, last in lasts:
            nst_ref[:, :, lo:lo + FFN_CHUNK] = last


def _ffn_layer(x, st, gin, gout, wup, wdw, bdw, wdn, *, layer, bb, tm):
    B, T, _ = x.shape
    nt = T // tm
    assert tm >= FFN_HIST and D_FF % FFN_CHUNK == 0
    body = functools.partial(_ffn_body, bb=bb, tm=tm, nt=nt)
    return pl.pallas_call(
        body,
        grid=(B // bb, nt),
        out_shape=(jax.ShapeDtypeStruct(x.shape, F32),
                   jax.ShapeDtypeStruct((B, FFN_HIST, 2 * D_FF), F32)),
        in_specs=[
            _seq_spec(bb, tm, D_MODEL),
            _state_spec(bb, FFN_HIST, 2 * D_FF),
            _const_spec((1, D_MODEL)),
            _const_spec((1, D_MODEL)),
            _layer_spec(wup, layer),
            _const_spec(wdw.shape),
            _const_spec(bdw.shape),
            _layer_spec(wdn, layer),
        ],
        out_specs=(_seq_spec(bb, tm, D_MODEL), _state_spec(bb, FFN_HIST, 2 * D_FF)),
        scratch_shapes=(
            [pltpu.VMEM((bb * tm, D_FF), BF16)]
            + [pltpu.VMEM((bb, FFN_PAD + tm, 2 * FFN_CHUNK), F32)] * (D_FF // FFN_CHUNK)),
        compiler_params=_params(),
        name="conv_ffn",
    )(x, st, gin, gout, wup, wdw, bdw, wdn)


def _trunk(x, caches, p, bias, *, bb, tm):
    prompt = caches is None
    B = x.shape[0]
    new_k, new_v, new_conv, new_cv, new_ffn = [], [], [], [], []
    for i in range(DEPTH):
        kind, j = i % N_MIXERS, i // N_MIXERS
        g = p['norm_gain'][i]
        gains = [g[n][None, :] for n in range(4)]
        if kind == 0:
            if prompt:
                kc = jnp.zeros((B, WINDOW, KV_DIM), F32)
                vc = kc
            else:
                kc = caches[0][j].reshape(B, WINDOW, KV_DIM)
                vc = caches[1][j].reshape(B, WINDOW, KV_DIM)
            x, k_rows, v_rows = _attn_layer(
                x, kc, vc, gains[0], gains[1], p['attn_w_qkv'], p['attn_b_qkv'][j][None, :],
                p['attn_w_o'], p['attn_b_o'][j][None, :], p['attn_sinks'][j], bias, layer=j,
                bb=bb, tm=tm, mask_start=prompt)
            new_k.append(k_rows.reshape(B, -1, N_KV_HEADS, HEAD_DIM))
            new_v.append(v_rows.reshape(B, -1, N_KV_HEADS, HEAD_DIM))
        elif kind == 1:
            st = jnp.zeros((B, CONV_HIST, D_MODEL), F32) if prompt else caches[2][j]
            x, s = _conv_layer(
                x, st, gains[0], gains[1], p['conv_w_pw1'][j], p['conv_b_pw1'][j][None, :],
                p['conv_w_dw'][j], p['conv_b_dw'][j][None, :], p['conv_ln_g'][j][None, :],
                p['conv_ln_b'][j][None, :], p['conv_w_pw2'][j], p['conv_b_pw2'][j][None, :],
                bb=bb, tm=tm)
            new_conv.append(s)
        else:
            res = _cmlp_layer(
                x, gains[0], gains[1], p['cmlp_w_in'][j], p['cmlp_b_in'][j][None, :],
                p['cmlp_ln_g'][j][None, :], p['cmlp_ln_b'][j][None, :], p['cmlp_w_s'][j],
                p['cmlp_b_s'][j], p['cmlp_w_out'][j], p['cmlp_b_out'][j][None, :],
                bb=bb, tm=tm, emit_v=not prompt)
            x = res[0]
            if not prompt:
                new_cv.append(res[1])
        st = jnp.zeros((B, FFN_HIST, 2 * D_FF), F32) if prompt else caches[3][i]
        x, s = _ffn_layer(x, st, gains[2], gains[3], p['ffn_w_up'], p['ffn_w_dw'][i],
                          p['ffn_b_dw'][i][None, :], p['ffn_w_down'], layer=i, bb=bb, tm=tm)
        new_ffn.append(s)
    cv = jnp.stack(new_cv) if new_cv else None
    return x, jnp.stack(new_k), jnp.stack(new_v), jnp.stack(new_conv), cv, jnp.stack(new_ffn)


def kernel(x_prompt, x_sample, cache_attn_k, cache_attn_v, state_conv, state_ffn_conv, rel_bias_table, norm_gain, attn_w_qkv, attn_b_qkv, attn_w_o, attn_b_o, attn_sinks, conv_w_pw1, conv_b_pw1, conv_w_dw, conv_b_dw, conv_ln_g, conv_ln_b, conv_w_pw2, conv_b_pw2, cmlp_w_in, cmlp_b_in, cmlp_ln_g, cmlp_ln_b, cmlp_w_s, cmlp_b_s, cmlp_w_out, cmlp_b_out, ffn_w_up, ffn_w_dw, ffn_b_dw, ffn_w_down):
    p = {
        'norm_gain': norm_gain,
        'attn_w_qkv': attn_w_qkv.astype(BF16), 'attn_b_qkv': attn_b_qkv,
        'attn_w_o': attn_w_o.astype(BF16), 'attn_b_o': attn_b_o, 'attn_sinks': attn_sinks,
        'conv_w_pw1': conv_w_pw1.astype(BF16), 'conv_b_pw1': conv_b_pw1, 'conv_w_dw': conv_w_dw,
        'conv_b_dw': conv_b_dw, 'conv_ln_g': conv_ln_g, 'conv_ln_b': conv_ln_b,
        'conv_w_pw2': conv_w_pw2.astype(BF16), 'conv_b_pw2': conv_b_pw2,
        'cmlp_w_in': cmlp_w_in.astype(BF16), 'cmlp_b_in': cmlp_b_in, 'cmlp_ln_g': cmlp_ln_g,
        'cmlp_ln_b': cmlp_ln_b, 'cmlp_w_s': cmlp_w_s, 'cmlp_b_s': cmlp_b_s,
        'cmlp_w_out': cmlp_w_out.astype(BF16), 'cmlp_b_out': cmlp_b_out,
        'ffn_w_up': ffn_w_up.astype(BF16), 'ffn_w_dw': ffn_w_dw, 'ffn_b_dw': ffn_b_dw,
        'ffn_w_down': ffn_w_down.astype(BF16),
    }
    bias = _rel_bias(rel_bias_table)
    y_prompt, p_attn_k, p_attn_v, p_conv, _, p_ffn_conv = _trunk(
        x_prompt, None, p, bias, bb=1, tm=512)
    y_sample, s_attn_k, s_attn_v, s_conv, s_cmlp_v, s_ffn_conv = _trunk(
        x_sample, (cache_attn_k, cache_attn_v, state_conv, state_ffn_conv), p, bias, bb=8, tm=64)
    return (y_prompt, y_sample, p_attn_k, p_attn_v, p_conv, p_ffn_conv,
            s_attn_k, s_attn_v, s_conv, s_cmlp_v, s_ffn_conv)
```

```python
import functools
import math

import jax
import jax.numpy as jnp
from jax import lax
from jax.experimental import pallas as pl
from jax.experimental.pallas import tpu as pltpu

D_MODEL = 1024
DEPTH = 4
CHUNK = 64
N_MIXERS = 3
N_HEADS = 16
N_KV_HEADS = 4
HEAD_DIM = 64
GQA_GROUP = N_HEADS // N_KV_HEADS
WINDOW = 128
KV_DIM = N_KV_HEADS * HEAD_DIM
Q_DIM = N_HEADS * HEAD_DIM
NUM_BUCKETS = 32
MAX_DISTANCE = 128
CONV_WIDTH = 31
CONV_HIST = CONV_WIDTH - 1
MIX_BLOCK = 128
CMLP_GROUPS = 4
CMLP_DIM = 2 * D_MODEL
CMLP_GROUP_DIM = CMLP_DIM // CMLP_GROUPS
D_FF = 2816
FFN_CONV_WIDTH = 3
FFN_HIST = FFN_CONV_WIDTH - 1
RMS_EPS = 1e-6
_GELU_C0 = math.sqrt(2.0 / math.pi)
_GELU_C1 = _GELU_C0 * 0.044715
LN_EPS = 1e-5

F32 = jnp.float32
BF16 = jnp.bfloat16

V7X_VMEM_BYTES = 64 * 1024 * 1024
VMEM_LIMIT_BYTES = V7X_VMEM_BYTES - 8 * 1024 * 1024
SUBLANES = 8
BF16_SUBLANES = 2 * SUBLANES
FFN_CHUNK = 256
FFN_DOWN_ROW_PARTS = 2
FFN_PAD = SUBLANES
CONV_PAD = 32
CONV_ROW_TILE = 64
CONV_LANE_TILE = 128
CONV_ROW_BLOCK = 256


def _rms(x, g):
    ms = jnp.mean(x * x, axis=-1, keepdims=True)
    return x * lax.rsqrt(ms + RMS_EPS) * g


def _layernorm(x, g, b):
    mu = jnp.mean(x, axis=-1, keepdims=True)
    xc = x - mu
    var = jnp.mean(xc * xc, axis=-1, keepdims=True)
    return xc * lax.rsqrt(var + LN_EPS) * g + b


def _gelu(x):
    hx = 0.5 * x
    return hx + hx * jnp.tanh(x * (_GELU_C0 + _GELU_C1 * (x * x)))


def _dot(a, b):
    return jnp.dot(a, b, preferred_element_type=F32)


def _const_spec(shape):
    zeros = (0,) * len(shape)
    return pl.BlockSpec(shape, lambda b, t: zeros, pipeline_mode=pl.Buffered(1))


def _layer_spec(stacked, layer):
    zeros = (0,) * (stacked.ndim - 1)
    return pl.BlockSpec((None,) + stacked.shape[1:], lambda b, t: (layer,) + zeros,
                        pipeline_mode=pl.Buffered(1))


def _seq_spec(bb, rows, width):
    return pl.BlockSpec((bb, rows, width), lambda b, t: (b, t, 0))


def _state_spec(bb, rows, width):
    return pl.BlockSpec((bb, rows, width), lambda b, t: (b, 0, 0))


def _params(flags=None):
    return pltpu.CompilerParams(
        dimension_semantics=("arbitrary", "arbitrary"),
        vmem_limit_bytes=VMEM_LIMIT_BYTES,
        flags=flags,
    )


def _mixer_call(body, *, grid, out_shape, in_specs, out_specs, scratch_shapes, name, ffn_cast=None):
    out_shape, out_specs, in_specs = tuple(out_shape), tuple(out_specs), list(in_specs)
    extra_args = ()
    if ffn_cast is not None:
        wup, wdn, layer = ffn_cast
        nt, steps = grid[1], grid[0] * grid[1]
        up_rows, dn_rows = D_MODEL // steps, D_FF // (steps // 2)
        assert up_rows * steps == D_MODEL and dn_rows * (steps // 2) == D_FF
        assert up_rows % BF16_SUBLANES == 0 and dn_rows % BF16_SUBLANES == 0
        n_in, n_out = len(in_specs), len(out_specs)
        mixer_body = body

        def body(*refs):
            cast_in = refs[n_in:n_in + 2]
            cast_out = refs[n_in + 2 + n_out:n_in + 4 + n_out]
            for src, dst in zip(cast_in, cast_out):
                dst[...] = src[...].astype(BF16)
            mixer_body(*refs[:n_in], *refs[n_in + 2:n_in + 2 + n_out], *refs[n_in + 4 + n_out:])

        in_specs += [pl.BlockSpec((None, up_rows, 2 * D_FF), lambda b, t: (layer, b * nt + t, 0)),
                     pl.BlockSpec((None, dn_rows, D_MODEL), lambda b, t: (layer, (b * nt + t) // 2, 0))]
        out_specs += (pl.BlockSpec((up_rows, 2 * D_FF), lambda b, t: (b * nt + t, 0)),
                      pl.BlockSpec((dn_rows, D_MODEL), lambda b, t: ((b * nt + t) // 2, 0)))
        out_shape += (jax.ShapeDtypeStruct((D_MODEL, 2 * D_FF), BF16),
                      jax.ShapeDtypeStruct((D_FF, D_MODEL), BF16))
        extra_args = (wup, wdn)
    call = pl.pallas_call(body, grid=grid, out_shape=out_shape, in_specs=in_specs, out_specs=out_specs,
                          scratch_shapes=scratch_shapes, compiler_params=_params(), name=name)
    return lambda *args: call(*args, *extra_args)


HEAD_PAIR = 2 * HEAD_DIM
ATTN_LOOKAHEAD = 2
ATTN_CHUNKS_PER_UNIT = 2
ATTN_ROW_SPLIT = 2
PAIR_STEP = GQA_GROUP // 2


def _pair_heads(j, v):
    return GQA_GROUP * j + v, GQA_GROUP * j + PAIR_STEP + v


def _bias_body(bucket_ref, table_ref, o_ref):
    bk = bucket_ref[...]
    low = lax.broadcasted_iota(jnp.int32, (1, HEAD_PAIR), 1) < CHUNK
    for j in range(N_KV_HEADS):
        for v in range(2):
            ha, hb = _pair_heads(j, v)
            acc = jnp.zeros(bk.shape, F32)
            for b in range(NUM_BUCKETS):
                val = jnp.where(low, table_ref[b, ha], table_ref[b, hb])
                acc = jnp.where(bk == b, val, acc)
            o_ref[j, v] = acc


def _t5_bucket(rel):
    half = NUM_BUCKETS // 2
    max_exact = half // 2
    n = jnp.abs(rel)
    log_ratio = jnp.log(jnp.maximum(n, 1).astype(F32) / max_exact) / math.log(MAX_DISTANCE / max_exact)
    large = jnp.minimum(max_exact + (log_ratio * (half - max_exact)).astype(jnp.int32), half - 1)
    return jnp.where(rel > 0, half, 0) + jnp.where(n < max_exact, n, large)


def _rel_bias(table):
    q_pos = jnp.arange(CHUNK, dtype=jnp.int32)
    k_pos = jnp.arange(WINDOW + CHUNK, dtype=jnp.int32) - WINDOW
    bucket = _t5_bucket(k_pos[:, None] - q_pos[None, :]).astype(jnp.int32)
    bucket = jnp.concatenate([bucket, bucket], axis=1)
    return pl.pallas_call(
        _bias_body,
        out_shape=jax.ShapeDtypeStruct((N_KV_HEADS, 2, WINDOW + CHUNK, HEAD_PAIR), F32),
        in_specs=[pl.BlockSpec(memory_space=pltpu.VMEM), pl.BlockSpec(memory_space=pltpu.SMEM)],
        out_specs=pl.BlockSpec(memory_space=pltpu.VMEM),
        name="rel_bias",
    )(bucket, table)


def _attn_body(x_ref, kc_ref, vc_ref, gin_ref, gout_ref, wqkv_ref, bqkv_ref, wo_ref, bo_ref,
               sinks_ref, bias_ref, o_ref, kout_ref, vout_ref, qbuf, kzbuf, vzbuf, obuf, ybuf,
               *, bb, tm, nt, mask_start):
    t = pl.program_id(1)
    rows = bb * tm
    kv_rows = min(WINDOW, tm)
    span = WINDOW + CHUNK
    low = lax.broadcasted_iota(jnp.int32, (1, HEAD_PAIR), 1) < CHUNK

    def expand(buf, sl, r0, kv):
        nb, n = kv.shape[0], kv.shape[1]
        for pair in range(N_KV_HEADS // 2):
            tile = kv[:, :, pair * HEAD_PAIR:(pair + 1) * HEAD_PAIR]
            swapped = pltpu.roll(tile.reshape(nb * n, HEAD_PAIR), HEAD_DIM, 1).reshape(nb, n, HEAD_PAIR)
            for j, (lo_src, hi_src) in ((2 * pair, (tile, swapped)), (2 * pair + 1, (swapped, tile))):
                buf[sl, j, 0, r0:r0 + n, :] = jnp.where(low, lo_src, 0.0).astype(BF16)
                buf[sl, j, 1, r0:r0 + n, :] = jnp.where(low, 0.0, hi_src).astype(BF16)

    @pl.when(t == 0)
    def _():
        expand(kzbuf, slice(0, bb), 0, kc_ref[...])
        expand(vzbuf, slice(0, bb), 0, vc_ref[...])

    if bb >= ATTN_ROW_SPLIT:
        sub_b, sub_t = bb // ATTN_ROW_SPLIT, tm
        subs = [(slice(i * sub_b, (i + 1) * sub_b), 0) for i in range(ATTN_ROW_SPLIT)]
    else:
        sub_b, sub_t = bb, tm // ATTN_ROW_SPLIT
        subs = [(slice(0, bb), i * sub_t) for i in range(ATTN_ROW_SPLIT)]
    sub_rows = sub_b * sub_t
    assert sub_t % CHUNK == 0 and (sub_b > 1 or sub_t >= kv_rows or ATTN_ROW_SPLIT == 1)
    n_piece = GQA_GROUP * HEAD_DIM

    def flat0(sub):
        sl, r0 = sub
        return sl.start * tm + r0

    def load_x(sub):
        sl, r0 = sub
        return x_ref[sl, r0:r0 + sub_t, :].reshape(sub_rows, D_MODEL)

    def in_pieces(sub):
        sl, r0 = sub
        h = _rms(load_x(sub), gin_ref[...]).astype(BF16)
        f0 = flat0(sub)

        def kv_piece(col, buf, out_ref):
            def run():
                kv = (_dot(h, wqkv_ref[:, col:col + KV_DIM]) + bqkv_ref[:, col:col + KV_DIM]).reshape(
                    sub_b, sub_t, KV_DIM)
                expand(buf, sl, WINDOW + r0, kv)
                if sub_b > 1 or r0 + sub_t == tm:
                    out_ref[sl] = kv[:, sub_t - kv_rows:, :]
            return run

        def q_piece(j):
            def run():
                col = j * n_piece
                q = _dot(h, wqkv_ref[:, col:col + n_piece]) + bqkv_ref[:, col:col + n_piece]
                qbuf[f0:f0 + sub_rows, col:col + n_piece] = (q * (HEAD_DIM ** -0.5)).astype(BF16)
            return run

        return ([kv_piece(Q_DIM, kzbuf, kout_ref), kv_piece(Q_DIM + KV_DIM, vzbuf, vout_ref)]
                + [q_piece(j) for j in range(N_KV_HEADS)])

    def out_pieces(sub):
        sl, r0 = sub
        f0 = flat0(sub)

        def piece(n):
            def run():
                col = n * n_piece
                ybuf[f0:f0 + sub_rows, col:col + n_piece] = (
                    _dot(obuf[f0:f0 + sub_rows, :], wo_ref[:, col:col + n_piece]) + bo_ref[:, col:col + n_piece])
            return run

        def finish():
            y = ybuf[f0:f0 + sub_rows, :]
            o_ref[sl, r0:r0 + sub_t, :] = (load_x(sub) + _rms(y, gout_ref[...])).reshape(sub_b, sub_t, D_MODEL)

        return [piece(n) for n in range(D_MODEL // n_piece)] + [finish]

    key_row = lax.broadcasted_iota(jnp.int32, (span, HEAD_PAIR), 0)
    first_valid = jnp.where(t == 0, WINDOW, 0)
    contract_last = (((1,), (1,)), ((), ()))
    contract_first = (((0,), (0,)), ((), ()))

    def scores(s, chunks, j):
        g0 = j * GQA_GROUP * HEAD_DIM
        r0 = chunks[0] * CHUNK
        n_keys = span + (len(chunks) - 1) * CHUNK
        q_parts = []
        for c in chunks:
            q_rows = slice(s * tm + c * CHUNK, s * tm + (c + 1) * CHUNK)
            q_parts += [qbuf[q_rows, g0:g0 + HEAD_PAIR], qbuf[q_rows, g0 + HEAD_PAIR:g0 + 2 * HEAD_PAIR]]
        keys = jnp.concatenate([kzbuf[s, j, v_idx, r0:r0 + n_keys, :] for v_idx in range(2)], axis=0)
        sc_all = lax.dot_general(keys, jnp.concatenate(q_parts, axis=0), contract_last,
                                 preferred_element_type=F32)
        out = []
        for n, c in enumerate(chunks):
            tiles = []
            for v_idx in range(2):
                k0 = v_idx * n_keys + n * CHUNK
                sc = sc_all[k0:k0 + span, n * HEAD_PAIR:(n + 1) * HEAD_PAIR] + bias_ref[j, v_idx]
                if mask_start and c * CHUNK < WINDOW:
                    sc = jnp.where(key_row + c * CHUNK >= first_valid, sc, -jnp.inf)
                tiles.append(sc)
            out.append((c, tiles))
        return out

    def attend(s, c, j, scs):
        r0 = c * CHUNK
        q_rows = slice(s * tm + r0, s * tm + r0 + CHUNK)
        g0 = j * GQA_GROUP * HEAD_DIM
        probs = []
        for v_idx, sc in enumerate(scs):
            ha, hb = _pair_heads(j, v_idx)
            sink = jnp.where(low, sinks_ref[ha], sinks_ref[hb])
            m = jnp.maximum(jnp.max(sc, axis=0, keepdims=True), sink)
            p = jnp.exp(sc - m)
            denom = jnp.sum(p, axis=0, keepdims=True) + jnp.exp(sink - m)
            probs.append((p * (1.0 / denom)).astype(BF16))
        p_kn = jnp.concatenate(probs, axis=0)
        v_kd = jnp.concatenate([vzbuf[s, j, 0, r0:r0 + span, :],
                                vzbuf[s, j, 1, r0:r0 + span, :]], axis=0)
        o = lax.dot_general(p_kn, v_kd, contract_first, preferred_element_type=F32)
        obuf[q_rows, g0:g0 + HEAD_PAIR] = o[:CHUNK].astype(BF16)
        obuf[q_rows, g0 + HEAD_PAIR:g0 + 2 * HEAD_PAIR] = o[CHUNK:].astype(BF16)

    def units_of(sub):
        sl, r0 = sub
        c0, c1 = r0 // CHUNK, (r0 + sub_t) // CHUNK
        step = ATTN_CHUNKS_PER_UNIT if (c1 - c0) % ATTN_CHUNKS_PER_UNIT == 0 else 1
        return [(s, tuple(range(c, c + step)), j) for s in range(sl.start, sl.stop)
                for c in range(c0, c1, step) for j in range(N_KV_HEADS)]

    def run_units(units, fillers):
        every = max(1, len(units) // max(1, len(fillers)))
        pending = []

        def retire():
            (s, _, j), per_chunk = pending.pop(0)
            for c, tiles in per_chunk:
                attend(s, c, j, tiles)

        for n, unit in enumerate(units):
            pending.append((unit, scores(*unit)))
            if len(pending) > ATTN_LOOKAHEAD:
                retire()
            if fillers and (n + 1) % every == 0:
                fillers.pop(0)()
        while pending:
            retire()
        for filler in fillers:
            filler()

    for piece in in_pieces(subs[0]):
        piece()
    for i, sub in enumerate(subs):
        fillers = []
        if i > 0:
            fillers += out_pieces(subs[i - 1])
        if i + 1 < len(subs):
            fillers += in_pieces(subs[i + 1])
        run_units(units_of(sub), fillers)
    for piece in out_pieces(subs[-1]):
        piece()

    if nt > 1:
        kzbuf[:, :, :, 0:WINDOW, :] = kzbuf[:, :, :, tm:tm + WINDOW, :]
        vzbuf[:, :, :, 0:WINDOW, :] = vzbuf[:, :, :, tm:tm + WINDOW, :]


def _attn_layer(x, kc, vc, gin, gout, wqkv, bqkv, wo, bo, sinks, bias, *, layer, bb, tm, mask_start,
                ffn_cast=None):
    B, T, _ = x.shape
    nt = T // tm
    kv_rows = min(WINDOW, tm)
    body = functools.partial(_attn_body, bb=bb, tm=tm, nt=nt, mask_start=mask_start)
    return _mixer_call(
        body,
        grid=(B // bb, nt),
        out_shape=(jax.ShapeDtypeStruct(x.shape, F32),
                   jax.ShapeDtypeStruct((B, kv_rows, KV_DIM), F32),
                   jax.ShapeDtypeStruct((B, kv_rows, KV_DIM), F32)),
        in_specs=[
            _seq_spec(bb, tm, D_MODEL),
            _state_spec(bb, WINDOW, KV_DIM),
            _state_spec(bb, WINDOW, KV_DIM),
            _const_spec((1, D_MODEL)),
            _const_spec((1, D_MODEL)),
            _layer_spec(wqkv, layer),
            _const_spec(bqkv.shape),
            _layer_spec(wo, layer),
            _const_spec(bo.shape),
            pl.BlockSpec(memory_space=pltpu.SMEM),
            _const_spec(bias.shape),
        ],
        out_specs=(_seq_spec(bb, tm, D_MODEL),
                   _state_spec(bb, kv_rows, KV_DIM),
                   _state_spec(bb, kv_rows, KV_DIM)),
        scratch_shapes=[
            pltpu.VMEM((bb * tm, Q_DIM), BF16),
            pltpu.VMEM((bb, N_KV_HEADS, 2, WINDOW + tm, HEAD_PAIR), BF16),
            pltpu.VMEM((bb, N_KV_HEADS, 2, WINDOW + tm, HEAD_PAIR), BF16),
            pltpu.VMEM((bb * tm, Q_DIM), BF16),
            pltpu.VMEM((bb * tm, D_MODEL), F32),
        ],
        name="attn_mixer",
        ffn_cast=ffn_cast,
    )(x, kc, vc, gin, gout, wqkv, bqkv, wo, bo, sinks, bias)


def _conv_body(x_ref, st_ref, gin_ref, gout_ref, w1_ref, b1_ref, wdw_ref, bdw_ref, lng_ref, lnb_ref,
               w2_ref, b2_ref, o_ref, nst_ref, gbuf, ybuf, cbuf, *, bb, tm, nt):
    t = pl.program_id(1)
    rows = bb * tm
    hist0 = CONV_PAD - CONV_HIST

    @pl.when(t == 0)
    def _():
        gbuf[:, hist0:CONV_PAD, :] = st_ref[...]

    row = lax.broadcasted_iota(jnp.int32, (SUBLANES, CONV_LANE_TILE), 0)
    n_tiles = CONV_ROW_TILE // SUBLANES
    max_a = (hist0 + CONV_WIDTH - 1) // SUBLANES
    tiles_per_block = CONV_ROW_BLOCK // CONV_ROW_TILE

    def row_tiles(rb):
        out = []
        for i in range(tiles_per_block):
            flat = rb * CONV_ROW_BLOCK + i * CONV_ROW_TILE
            out.append((flat // tm, flat % tm, flat))
        return out

    def block_rows(ref, rb):
        flat = rb * CONV_ROW_BLOCK
        if tm >= CONV_ROW_BLOCK:
            return ref.at[flat // tm, flat % tm:flat % tm + CONV_ROW_BLOCK, :]
        return ref.at[flat // tm:(flat + CONV_ROW_BLOCK) // tm, :, :]

    def pointwise_in(rb):
        x = block_rows(x_ref, rb)[...].reshape(CONV_ROW_BLOCK, D_MODEL)
        h = _rms(x, gin_ref[...]).astype(BF16)
        ag = _dot(h, w1_ref[...]) + b1_ref[...]
        glu = ag[:, :D_MODEL] * jax.nn.sigmoid(ag[:, D_MODEL:])
        for i, (s, r0, _) in enumerate(row_tiles(rb)):
            gbuf[s, CONV_PAD + r0:CONV_PAD + r0 + CONV_ROW_TILE, :] = (
                glu[i * CONV_ROW_TILE:(i + 1) * CONV_ROW_TILE])

    def depthwise(rb):
        for l0 in range(0, D_MODEL, CONV_LANE_TILE):
            lanes = slice(l0, l0 + CONV_LANE_TILE)
            carried, carried_for = {}, None
            for s, r0, flat in row_tiles(rb):
                if carried_for != (s, r0):
                    carried = {}
                tiles = [gbuf[s, r0 + SUBLANES * j:r0 + SUBLANES * (j + 1), lanes]
                         for j in range(n_tiles + max_a)]
                out = [None] * n_tiles
                for b in range(SUBLANES):
                    taps = [a for a in range(max_a + 1)
                            if 0 <= SUBLANES * a + b - hist0 < CONV_WIDTH]
                    wts = [wdw_ref[SUBLANES * a + b - hist0:SUBLANES * a + b - hist0 + 1, lanes]
                           for a in taps]
                    z = []
                    for m in range(n_tiles + (1 if b else 0)):
                        if m == 0 and b in carried:
                            z.append(carried[b])
                            continue
                        acc = None
                        for a, wt in zip(taps, wts):
                            term = tiles[m + a] * wt
                            acc = term if acc is None else acc + term
                        z.append(acc)
                    if b:
                        carried[b] = z[n_tiles]
                    for m in range(n_tiles):
                        if b == 0:
                            part = z[m]
                        else:
                            part = pltpu.roll(jnp.where(row >= b, z[m], z[m + 1]), SUBLANES - b, 0)
                        out[m] = part if out[m] is None else out[m] + part
                carried_for = (s, r0 + CONV_ROW_TILE)
                for m in range(n_tiles):
                    lo = flat + SUBLANES * m
                    ybuf[lo:lo + SUBLANES, lanes] = out[m]
        for s, r0, flat in row_tiles(rb):
            rs = slice(flat, flat + CONV_ROW_TILE)
            y = _layernorm(ybuf[rs, :] + bdw_ref[...], lng_ref[...], lnb_ref[...])
            cbuf[rs, :] = (y * jax.nn.sigmoid(y)).astype(BF16)

    def pointwise_out(rb):
        flat = rb * CONV_ROW_BLOCK
        y = _dot(cbuf[flat:flat + CONV_ROW_BLOCK, :], w2_ref[...]) + b2_ref[...]
        x = block_rows(x_ref, rb)[...]
        block_rows(o_ref, rb)[...] = x + _rms(y, gout_ref[...]).reshape(x.shape)

    n_blocks = rows // CONV_ROW_BLOCK
    pointwise_in(0)
    for rb in range(1, n_blocks):
        depthwise(rb - 1)
        pointwise_in(rb)
        pointwise_out(rb - 1)
    depthwise(n_blocks - 1)
    pointwise_out(n_blocks - 1)

    last = gbuf[:, CONV_PAD + tm - CONV_HIST:CONV_PAD + tm, :]

    @pl.when(t == nt - 1)
    def _():
        nst_ref[...] = last

    if nt > 1:
        gbuf[:, hist0:CONV_PAD, :] = last


def _conv_layer(x, st, gin, gout, w1, b1, wdw, bdw, lng, lnb, w2, b2, *, bb, tm, ffn_cast=None):
    B, T, _ = x.shape
    nt = T // tm
    assert tm >= CONV_HIST and tm % CONV_ROW_TILE == 0
    body = functools.partial(_conv_body, bb=bb, tm=tm, nt=nt)
    return _mixer_call(
        body,
        grid=(B // bb, nt),
        out_shape=(jax.ShapeDtypeStruct(x.shape, F32),
                   jax.ShapeDtypeStruct((B, CONV_HIST, D_MODEL), F32)),
        in_specs=[
            _seq_spec(bb, tm, D_MODEL),
            _state_spec(bb, CONV_HIST, D_MODEL),
            _const_spec((1, D_MODEL)),
            _const_spec((1, D_MODEL)),
            _const_spec(w1.shape),
            _const_spec(b1.shape),
            _const_spec(wdw.shape),
            _const_spec(bdw.shape),
            _const_spec(lng.shape),
            _const_spec(lnb.shape),
            _const_spec(w2.shape),
            _const_spec(b2.shape),
        ],
        out_specs=(_seq_spec(bb, tm, D_MODEL), _state_spec(bb, CONV_HIST, D_MODEL)),
        scratch_shapes=[
            pltpu.VMEM((bb, CONV_PAD + tm, D_MODEL), F32),
            pltpu.VMEM((bb * tm, D_MODEL), F32),
            pltpu.VMEM((bb * tm, D_MODEL), BF16),
        ],
        name="conv_mixer",
        ffn_cast=ffn_cast,
    )(x, st, gin, gout, w1, b1, wdw, bdw, lng, lnb, w2, b2)


def _cmlp_body(x_ref, gin_ref, gout_ref, win_ref, bin_ref, lng_ref, lnb_ref, ws_ref, bs_ref,
               wout_ref, bout_ref, *rest, bb, tm, blk, emit_v):
    if emit_v:
        o_ref, v_ref, vbuf, ybuf = rest
    else:
        o_ref, vbuf, ybuf = rest
    rows = bb * tm
    x = x_ref[...].reshape(rows, D_MODEL)
    h = _rms(x, gin_ref[...]).astype(BF16)
    hv = _gelu(_dot(h, win_ref[:, CMLP_DIM:]) + bin_ref[:, CMLP_DIM:])
    vln = _layernorm(hv, lng_ref[...], lnb_ref[...])
    if emit_v:
        v_ref[...] = vln.reshape(bb, tm, CMLP_DIM)
    vbuf[...] = vln.astype(BF16)

    pos_r = lax.broadcasted_iota(jnp.int32, (blk, blk), 0) // CHUNK
    pos_c = lax.broadcasted_iota(jnp.int32, (blk, blk), 1) // CHUNK
    for g in range(CMLP_GROUPS):
        lanes = slice(g * CMLP_GROUP_DIM, (g + 1) * CMLP_GROUP_DIM)
        ug = _gelu(_dot(h, win_ref[:, lanes]) + bin_ref[:, lanes])
        wsm = jnp.where(pos_c <= pos_r, ws_ref[g], 0.0).astype(BF16)
        for n in range(rows // blk):
            rs = slice(n * blk, (n + 1) * blk)
            gate = _dot(wsm, vbuf[rs, lanes]) + bs_ref[g]
            ybuf[rs, lanes] = (ug[rs] * gate).astype(BF16)

    y = _dot(ybuf[...], wout_ref[...]) + bout_ref[...]
    o_ref[...] = (x + _rms(y, gout_ref[...])).reshape(bb, tm, D_MODEL)


def _cmlp_layer(x, gin, gout, win, bin_, lng, lnb, ws, bs, wout, bout, *, bb, tm, emit_v, ffn_cast=None):
    B, T, _ = x.shape
    blk = min(T, MIX_BLOCK)
    assert tm % blk == 0
    ws = ws[:, :blk, :blk]
    bs = bs[:, :blk, None]
    body = functools.partial(_cmlp_body, bb=bb, tm=tm, blk=blk, emit_v=emit_v)
    out_shape = [jax.ShapeDtypeStruct(x.shape, F32)]
    out_specs = [_seq_spec(bb, tm, D_MODEL)]
    if emit_v:
        out_shape.append(jax.ShapeDtypeStruct((B, T, CMLP_DIM), F32))
        out_specs.append(_seq_spec(bb, tm, CMLP_DIM))
    return _mixer_call(
        body,
        grid=(B // bb, T // tm),
        out_shape=tuple(out_shape),
        in_specs=[
            _seq_spec(bb, tm, D_MODEL),
            _const_spec((1, D_MODEL)),
            _const_spec((1, D_MODEL)),
            _const_spec(win.shape),
            _const_spec(bin_.shape),
            _const_spec(lng.shape),
            _const_spec(lnb.shape),
            _const_spec(ws.shape),
            _const_spec(bs.shape),
            _const_spec(wout.shape),
            _const_spec(bout.shape),
        ],
        out_specs=tuple(out_specs),
        scratch_shapes=[
            pltpu.VMEM((bb * tm, CMLP_DIM), BF16),
            pltpu.VMEM((bb * tm, CMLP_DIM), BF16),
        ],
        name="cmlp_mixer",
        ffn_cast=ffn_cast,
    )(x, gin, gout, win, bin_, lng, lnb, ws, bs, wout, bout)


def _ffn_body(x_ref, st_ref, gin_ref, gout_ref, wup_ref, wdw_ref, bdw_ref, wdn_ref,
              o_ref, nst_ref, actbuf, *upbufs, bb, tm, nt):
    n_chunks = D_FF // FFN_CHUNK
    t = pl.program_id(1)
    rows = bb * tm
    hist0 = FFN_PAD - FFN_HIST
    n_tiles = tm // SUBLANES
    tile_row = lax.broadcasted_iota(jnp.int32, (bb * n_tiles, SUBLANES, FFN_CHUNK), 1)

    def halves(c):
        return ((c * FFN_CHUNK, slice(0, FFN_CHUNK)),
                (D_FF + c * FFN_CHUNK, slice(FFN_CHUNK, 2 * FFN_CHUNK)))

    @pl.when(t == 0)
    def _():
        for c in range(n_chunks):
            upbufs[c][:, 0:hist0, :] = jnp.zeros((bb, hist0, 2 * FFN_CHUNK), F32)
            for lo, lanes in halves(c):
                upbufs[c][:, hist0:FFN_PAD, lanes] = st_ref[:, :, lo:lo + FFN_CHUNK]

    def up_proj(h, c):
        for lo, lanes in halves(c):
            up = _dot(h, wup_ref[:, lo:lo + FFN_CHUNK])
            upbufs[c][:, FFN_PAD:FFN_PAD + tm, lanes] = up.reshape(bb, tm, FFN_CHUNK)

    def delayed(cur, hist):
        cur_r = pltpu.roll(cur, 1, 1)
        hist_r = pltpu.roll(hist, 1, 1)
        cur_r4 = cur_r.reshape(bb, n_tiles, SUBLANES, FFN_CHUNK)
        prev_r = jnp.concatenate([hist_r[:, None], cur_r4[:, :n_tiles - 1]], axis=1)
        out = jnp.where(tile_row < 1, prev_r.reshape(bb * n_tiles, SUBLANES, FFN_CHUNK), cur_r)
        return out, hist_r

    def conv(c, lo, lanes):
        w = wdw_ref[:, lo:lo + FFN_CHUNK]
        cur = upbufs[c][:, FFN_PAD:FFN_PAD + tm, lanes].reshape(bb * n_tiles, SUBLANES, FFN_CHUNK)
        hist = upbufs[c][:, 0:FFN_PAD, lanes]
        acc, acc_hist = None, None
        for kk in range(FFN_HIST):
            term, term_hist = cur * w[kk:kk + 1], hist * w[kk:kk + 1]
            if acc is not None:
                term, term_hist = term + acc, term_hist + acc_hist
            acc, acc_hist = delayed(term, term_hist)
        out = cur * w[FFN_HIST:FFN_HIST + 1] + bdw_ref[:, lo:lo + FFN_CHUNK] + acc
        return out.reshape(rows, FFN_CHUNK)

    x = x_ref[...].reshape(rows, D_MODEL)
    h = _rms(x, gin_ref[...]).astype(BF16)
    for c in range(n_chunks):
        up_proj(h, c)
        (glo, glanes), (ulo, ulanes) = halves(c)
        act = _gelu(conv(c, glo, glanes)) * conv(c, ulo, ulanes)
        actbuf[:, c * FFN_CHUNK:(c + 1) * FFN_CHUNK] = act.astype(BF16)

    part_rows = rows // FFN_DOWN_ROW_PARTS
    for rp in range(FFN_DOWN_ROW_PARTS):
        rs = slice(rp * part_rows, (rp + 1) * part_rows)
        out = x[rs] + _rms(_dot(actbuf[rs, :], wdn_ref[...]), gout_ref[...])
        if bb == 1:
            o_ref[0, rs, :] = out
        else:
            nb = part_rows // tm
            o_ref[rp * nb:(rp + 1) * nb] = out.reshape(nb, tm, D_MODEL)

    lasts = [(lo, lanes, upbufs[c][:, FFN_PAD + tm - FFN_HIST:FFN_PAD + tm, lanes])
             for c in range(n_chunks) for lo, lanes in halves(c)]
    if nt > 1:
        for i, (lo, lanes, last) in enumerate(lasts):
            upbufs[i // 2][:, hist0:FFN_PAD, lanes] = last

    @pl.when(t == nt - 1)
    def _():
        for lo, lanes, last in lasts:
            nst_ref[:, :, lo:lo + FFN_CHUNK] = last


def _ffn_layer(x, st, gin, gout, wup, wdw, bdw, wdn, *, bb, tm):
    B, T, _ = x.shape
    nt = T // tm
    assert tm >= FFN_HIST and D_FF % FFN_CHUNK == 0
    body = functools.partial(_ffn_body, bb=bb, tm=tm, nt=nt)
    return pl.pallas_call(
        body,
        grid=(B // bb, nt),
        out_shape=(jax.ShapeDtypeStruct(x.shape, F32),
                   jax.ShapeDtypeStruct((B, FFN_HIST, 2 * D_FF), F32)),
        in_specs=[
            _seq_spec(bb, tm, D_MODEL),
            _state_spec(bb, FFN_HIST, 2 * D_FF),
            _const_spec((1, D_MODEL)),
            _const_spec((1, D_MODEL)),
            _const_spec(wup.shape),
            _const_spec(wdw.shape),
            _const_spec(bdw.shape),
            _const_spec(wdn.shape),
        ],
        out_specs=(_seq_spec(bb, tm, D_MODEL), _state_spec(bb, FFN_HIST, 2 * D_FF)),
        scratch_shapes=(
            [pltpu.VMEM((bb * tm, D_FF), BF16)]
            + [pltpu.VMEM((bb, FFN_PAD + tm, 2 * FFN_CHUNK), F32)] * (D_FF // FFN_CHUNK)),
        compiler_params=_params(),
        name="conv_ffn",
    )(x, st, gin, gout, wup, wdw, bdw, wdn)


def _trunk(x, caches, p, bias, ffn_bf16, *, bb, tm):
    prompt = caches is None
    cast_here = ffn_bf16 is None
    ffn_bf16 = [] if cast_here else ffn_bf16
    B = x.shape[0]
    new_k, new_v, new_conv, new_cv, new_ffn = [], [], [], [], []
    for i in range(DEPTH):
        kind, j = i % N_MIXERS, i // N_MIXERS
        g = p['norm_gain'][i]
        gains = [g[n][None, :] for n in range(4)]
        ffn_cast = (p['ffn_w_up'], p['ffn_w_down'], i) if cast_here else None
        if kind == 0:
            if prompt:
                kc = jnp.zeros((B, WINDOW, KV_DIM), F32)
                vc = kc
            else:
                kc = caches[0][j].reshape(B, WINDOW, KV_DIM)
                vc = caches[1][j].reshape(B, WINDOW, KV_DIM)
            x, k_rows, v_rows, *cast = _attn_layer(
                x, kc, vc, gains[0], gains[1], p['attn_w_qkv'], p['attn_b_qkv'][j][None, :],
                p['attn_w_o'], p['attn_b_o'][j][None, :], p['attn_sinks'][j], bias, layer=j,
                bb=bb, tm=tm, mask_start=prompt, ffn_cast=ffn_cast)
            new_k.append(k_rows.reshape(B, -1, N_KV_HEADS, HEAD_DIM))
            new_v.append(v_rows.reshape(B, -1, N_KV_HEADS, HEAD_DIM))
        elif kind == 1:
            st = jnp.zeros((B, CONV_HIST, D_MODEL), F32) if prompt else caches[2][j]
            x, s, *cast = _conv_layer(
                x, st, gains[0], gains[1], p['conv_w_pw1'][j], p['conv_b_pw1'][j][None, :],
                p['conv_w_dw'][j], p['conv_b_dw'][j][None, :], p['conv_ln_g'][j][None, :],
                p['conv_ln_b'][j][None, :], p['conv_w_pw2'][j], p['conv_b_pw2'][j][None, :],
                bb=bb, tm=tm, ffn_cast=ffn_cast)
            new_conv.append(s)
        else:
            res = _cmlp_layer(
                x, gains[0], gains[1], p['cmlp_w_in'][j], p['cmlp_b_in'][j][None, :],
                p['cmlp_ln_g'][j][None, :], p['cmlp_ln_b'][j][None, :], p['cmlp_w_s'][j],
                p['cmlp_b_s'][j], p['cmlp_w_out'][j], p['cmlp_b_out'][j][None, :],
                bb=bb, tm=tm, emit_v=not prompt, ffn_cast=ffn_cast)
            x = res[0]
            if not prompt:
                new_cv.append(res[1])
            cast = res[-2:] if cast_here else []
        st = jnp.zeros((B, FFN_HIST, 2 * D_FF), F32) if prompt else caches[3][i]
        if cast_here:
            ffn_bf16.append(tuple(cast))
        wup, wdn = ffn_bf16[i]
        x, s = _ffn_layer(x, st, gains[2], gains[3], wup, p['ffn_w_dw'][i],
                          p['ffn_b_dw'][i][None, :], wdn, bb=bb, tm=tm)
        new_ffn.append(s)
    cv = jnp.stack(new_cv) if new_cv else None
    return (x, jnp.stack(new_k), jnp.stack(new_v), jnp.stack(new_conv), cv, jnp.stack(new_ffn)), ffn_bf16


def kernel(x_prompt, x_sample, cache_attn_k, cache_attn_v, state_conv, state_ffn_conv, rel_bias_table, norm_gain, attn_w_qkv, attn_b_qkv, attn_w_o, attn_b_o, attn_sinks, conv_w_pw1, conv_b_pw1, conv_w_dw, conv_b_dw, conv_ln_g, conv_ln_b, conv_w_pw2, conv_b_pw2, cmlp_w_in, cmlp_b_in, cmlp_ln_g, cmlp_ln_b, cmlp_w_s, cmlp_b_s, cmlp_w_out, cmlp_b_out, ffn_w_up, ffn_w_dw, ffn_b_dw, ffn_w_down):
    p = {
        'norm_gain': norm_gain,
        'attn_w_qkv': attn_w_qkv.astype(BF16), 'attn_b_qkv': attn_b_qkv,
        'attn_w_o': attn_w_o.astype(BF16), 'attn_b_o': attn_b_o, 'attn_sinks': attn_sinks,
        'conv_w_pw1': conv_w_pw1.astype(BF16), 'conv_b_pw1': conv_b_pw1, 'conv_w_dw': conv_w_dw,
        'conv_b_dw': conv_b_dw, 'conv_ln_g': conv_ln_g, 'conv_ln_b': conv_ln_b,
        'conv_w_pw2': conv_w_pw2.astype(BF16), 'conv_b_pw2': conv_b_pw2,
        'cmlp_w_in': cmlp_w_in.astype(BF16), 'cmlp_b_in': cmlp_b_in, 'cmlp_ln_g': cmlp_ln_g,
        'cmlp_ln_b': cmlp_ln_b, 'cmlp_w_s': cmlp_w_s, 'cmlp_b_s': cmlp_b_s,
        'cmlp_w_out': cmlp_w_out.astype(BF16), 'cmlp_b_out': cmlp_b_out,
        'ffn_w_up': ffn_w_up, 'ffn_w_dw': ffn_w_dw, 'ffn_b_dw': ffn_b_dw, 'ffn_w_down': ffn_w_down,
    }
    bias = _rel_bias(rel_bias_table)
    (y_prompt, p_attn_k, p_attn_v, p_conv, _, p_ffn_conv), ffn_bf16 = _trunk(
        x_prompt, None, p, bias, None, bb=1, tm=512)
    (y_sample, s_attn_k, s_attn_v, s_conv, s_cmlp_v, s_ffn_conv), _ = _trunk(
        x_sample, (cache_attn_k, cache_attn_v, state_conv, state_ffn_conv), p, bias, ffn_bf16, bb=8, tm=64)
    return (y_prompt, y_sample, p_attn_k, p_attn_v, p_conv, p_ffn_conv,
            s_attn_k, s_attn_v, s_conv, s_cmlp_v, s_ffn_conv)
```

```python
import functools
import math

import jax
import jax.numpy as jnp
from jax import lax
from jax.experimental import pallas as pl
from jax.experimental.pallas import tpu as pltpu

D_MODEL = 1024
DEPTH = 4
CHUNK = 64
N_MIXERS = 3
N_HEADS = 16
N_KV_HEADS = 4
HEAD_DIM = 64
GQA_GROUP = N_HEADS // N_KV_HEADS
WINDOW = 128
KV_DIM = N_KV_HEADS * HEAD_DIM
Q_DIM = N_HEADS * HEAD_DIM
NUM_BUCKETS = 32
MAX_DISTANCE = 128
CONV_WIDTH = 31
CONV_HIST = CONV_WIDTH - 1
MIX_BLOCK = 128
CMLP_GROUPS = 4
CMLP_DIM = 2 * D_MODEL
CMLP_GROUP_DIM = CMLP_DIM // CMLP_GROUPS
D_FF = 2816
FFN_CONV_WIDTH = 3
FFN_HIST = FFN_CONV_WIDTH - 1
RMS_EPS = 1e-6
_GELU_C0 = math.sqrt(2.0 / math.pi)
_GELU_C1 = _GELU_C0 * 0.044715
LN_EPS = 1e-5

F32 = jnp.float32
BF16 = jnp.bfloat16

V7X_VMEM_BYTES = 64 * 1024 * 1024
VMEM_LIMIT_BYTES = V7X_VMEM_BYTES - 8 * 1024 * 1024
SUBLANES = 8
BF16_SUBLANES = 2 * SUBLANES
FFN_CHUNK = 256
FFN_DOWN_ROW_PARTS = 4
FFN_PAD = SUBLANES
CONV_PAD = 32
CONV_ROW_TILE = 64
CONV_LANE_TILE = 128
CONV_ROW_BLOCK = 256


def _rms(x, g):
    ms = jnp.mean(x * x, axis=-1, keepdims=True)
    return x * lax.rsqrt(ms + RMS_EPS) * g


def _layernorm(x, g, b):
    mu = jnp.mean(x, axis=-1, keepdims=True)
    xc = x - mu
    var = jnp.mean(xc * xc, axis=-1, keepdims=True)
    return xc * lax.rsqrt(var + LN_EPS) * g + b


def _gelu(x):
    hx = 0.5 * x
    return hx + hx * jnp.tanh(x * (_GELU_C0 + _GELU_C1 * (x * x)))


def _dot(a, b):
    return jnp.dot(a, b, preferred_element_type=F32)


def _const_spec(shape):
    zeros = (0,) * len(shape)
    return pl.BlockSpec(shape, lambda b, t: zeros, pipeline_mode=pl.Buffered(1))


def _layer_spec(stacked, layer):
    zeros = (0,) * (stacked.ndim - 1)
    return pl.BlockSpec((None,) + stacked.shape[1:], lambda b, t: (layer,) + zeros,
                        pipeline_mode=pl.Buffered(1))


def _seq_spec(bb, rows, width):
    return pl.BlockSpec((bb, rows, width), lambda b, t: (b, t, 0))


def _state_spec(bb, rows, width):
    return pl.BlockSpec((bb, rows, width), lambda b, t: (b, 0, 0))


def _params(flags=None):
    return pltpu.CompilerParams(
        dimension_semantics=("arbitrary", "arbitrary"),
        vmem_limit_bytes=VMEM_LIMIT_BYTES,
        flags=flags,
    )


def _mixer_call(body, *, grid, out_shape, in_specs, out_specs, scratch_shapes, name, ffn_cast=None):
    out_shape, out_specs, in_specs = tuple(out_shape), tuple(out_specs), list(in_specs)
    extra_args = ()
    if ffn_cast is not None:
        wup, wdn, layer = ffn_cast
        nt, steps = grid[1], grid[0] * grid[1]
        up_rows, dn_rows = D_MODEL // steps, D_FF // (steps // 2)
        assert up_rows * steps == D_MODEL and dn_rows * (steps // 2) == D_FF
        assert up_rows % BF16_SUBLANES == 0 and dn_rows % BF16_SUBLANES == 0
        n_in, n_out = len(in_specs), len(out_specs)
        mixer_body = body

        def body(*refs):
            cast_in = refs[n_in:n_in + 2]
            cast_out = refs[n_in + 2 + n_out:n_in + 4 + n_out]
            for src, dst in zip(cast_in, cast_out):
                dst[...] = src[...].astype(BF16)
            mixer_body(*refs[:n_in], *refs[n_in + 2:n_in + 2 + n_out], *refs[n_in + 4 + n_out:])

        in_specs += [pl.BlockSpec((None, up_rows, 2 * D_FF), lambda b, t: (layer, b * nt + t, 0)),
                     pl.BlockSpec((None, dn_rows, D_MODEL), lambda b, t: (layer, (b * nt + t) // 2, 0))]
        out_specs += (pl.BlockSpec((up_rows, 2 * D_FF), lambda b, t: (b * nt + t, 0)),
                      pl.BlockSpec((dn_rows, D_MODEL), lambda b, t: ((b * nt + t) // 2, 0)))
        out_shape += (jax.ShapeDtypeStruct((D_MODEL, 2 * D_FF), BF16),
                      jax.ShapeDtypeStruct((D_FF, D_MODEL), BF16))
        extra_args = (wup, wdn)
    call = pl.pallas_call(body, grid=grid, out_shape=out_shape, in_specs=in_specs, out_specs=out_specs,
                          scratch_shapes=scratch_shapes, compiler_params=_params(), name=name)
    return lambda *args: call(*args, *extra_args)


HEAD_PAIR = 2 * HEAD_DIM
ATTN_LOOKAHEAD = 2
ATTN_CHUNKS_PER_UNIT = 2
ATTN_ROW_SPLIT = 2
PAIR_STEP = GQA_GROUP // 2


def _pair_heads(j, v):
    return GQA_GROUP * j + v, GQA_GROUP * j + PAIR_STEP + v


def _bias_body(bucket_ref, table_ref, o_ref):
    bk = bucket_ref[...]
    low = lax.broadcasted_iota(jnp.int32, (1, HEAD_PAIR), 1) < CHUNK
    for j in range(N_KV_HEADS):
        for v in range(2):
            ha, hb = _pair_heads(j, v)
            acc = jnp.zeros(bk.shape, F32)
            for b in range(NUM_BUCKETS):
                val = jnp.where(low, table_ref[b, ha], table_ref[b, hb])
                acc = jnp.where(bk == b, val, acc)
            o_ref[j, v] = acc


def _t5_bucket(rel):
    half = NUM_BUCKETS // 2
    max_exact = half // 2
    n = jnp.abs(rel)
    log_ratio = jnp.log(jnp.maximum(n, 1).astype(F32) / max_exact) / math.log(MAX_DISTANCE / max_exact)
    large = jnp.minimum(max_exact + (log_ratio * (half - max_exact)).astype(jnp.int32), half - 1)
    return jnp.where(rel > 0, half, 0) + jnp.where(n < max_exact, n, large)


def _rel_bias(table):
    q_pos = jnp.arange(CHUNK, dtype=jnp.int32)
    k_pos = jnp.arange(WINDOW + CHUNK, dtype=jnp.int32) - WINDOW
    bucket = _t5_bucket(k_pos[:, None] - q_pos[None, :]).astype(jnp.int32)
    bucket = jnp.concatenate([bucket, bucket], axis=1)
    return pl.pallas_call(
        _bias_body,
        out_shape=jax.ShapeDtypeStruct((N_KV_HEADS, 2, WINDOW + CHUNK, HEAD_PAIR), F32),
        in_specs=[pl.BlockSpec(memory_space=pltpu.VMEM), pl.BlockSpec(memory_space=pltpu.SMEM)],
        out_specs=pl.BlockSpec(memory_space=pltpu.VMEM),
        name="rel_bias",
    )(bucket, table)


def _attn_body(x_ref, kc_ref, vc_ref, gin_ref, gout_ref, wqkv_ref, bqkv_ref, wo_ref, bo_ref,
               sinks_ref, bias_ref, o_ref, kout_ref, vout_ref, qbuf, kzbuf, vzbuf, obuf, ybuf,
               *, bb, tm, nt, mask_start):
    t = pl.program_id(1)
    rows = bb * tm
    kv_rows = min(WINDOW, tm)
    span = WINDOW + CHUNK
    low = lax.broadcasted_iota(jnp.int32, (1, HEAD_PAIR), 1) < CHUNK

    def expand(buf, sl, r0, kv):
        nb, n = kv.shape[0], kv.shape[1]
        for pair in range(N_KV_HEADS // 2):
            tile = kv[:, :, pair * HEAD_PAIR:(pair + 1) * HEAD_PAIR]
            swapped = pltpu.roll(tile.reshape(nb * n, HEAD_PAIR), HEAD_DIM, 1).reshape(nb, n, HEAD_PAIR)
            for j, (lo_src, hi_src) in ((2 * pair, (tile, swapped)), (2 * pair + 1, (swapped, tile))):
                buf[sl, j, 0, r0:r0 + n, :] = jnp.where(low, lo_src, 0.0).astype(BF16)
                buf[sl, j, 1, r0:r0 + n, :] = jnp.where(low, 0.0, hi_src).astype(BF16)

    @pl.when(t == 0)
    def _():
        expand(kzbuf, slice(0, bb), 0, kc_ref[...])
        expand(vzbuf, slice(0, bb), 0, vc_ref[...])

    if bb >= ATTN_ROW_SPLIT:
        sub_b, sub_t = bb // ATTN_ROW_SPLIT, tm
        subs = [(slice(i * sub_b, (i + 1) * sub_b), 0) for i in range(ATTN_ROW_SPLIT)]
    else:
        sub_b, sub_t = bb, tm // ATTN_ROW_SPLIT
        subs = [(slice(0, bb), i * sub_t) for i in range(ATTN_ROW_SPLIT)]
    sub_rows = sub_b * sub_t
    assert sub_t % CHUNK == 0 and (sub_b > 1 or sub_t >= kv_rows or ATTN_ROW_SPLIT == 1)
    n_piece = GQA_GROUP * HEAD_DIM

    def flat0(sub):
        sl, r0 = sub
        return sl.start * tm + r0

    def load_x(sub):
        sl, r0 = sub
        return x_ref[sl, r0:r0 + sub_t, :].reshape(sub_rows, D_MODEL)

    def in_pieces(sub):
        sl, r0 = sub
        h = _rms(load_x(sub), gin_ref[...]).astype(BF16)
        f0 = flat0(sub)

        def kv_piece(col, buf, out_ref):
            def run():
                kv = (_dot(h, wqkv_ref[:, col:col + KV_DIM]) + bqkv_ref[:, col:col + KV_DIM]).reshape(
                    sub_b, sub_t, KV_DIM)
                expand(buf, sl, WINDOW + r0, kv)
                if sub_b > 1 or r0 + sub_t == tm:
                    out_ref[sl] = kv[:, sub_t - kv_rows:, :]
            return run

        def q_piece(j):
            def run():
                col = j * n_piece
                q = _dot(h, wqkv_ref[:, col:col + n_piece]) + bqkv_ref[:, col:col + n_piece]
                qbuf[f0:f0 + sub_rows, col:col + n_piece] = (q * (HEAD_DIM ** -0.5)).astype(BF16)
            return run

        return ([kv_piece(Q_DIM, kzbuf, kout_ref), kv_piece(Q_DIM + KV_DIM, vzbuf, vout_ref)]
                + [q_piece(j) for j in range(N_KV_HEADS)])

    def out_pieces(sub):
        sl, r0 = sub
        f0 = flat0(sub)

        def piece(n):
            def run():
                col = n * n_piece
                ybuf[f0:f0 + sub_rows, col:col + n_piece] = (
                    _dot(obuf[f0:f0 + sub_rows, :], wo_ref[:, col:col + n_piece]) + bo_ref[:, col:col + n_piece])
            return run

        def finish():
            y = ybuf[f0:f0 + sub_rows, :]
            o_ref[sl, r0:r0 + sub_t, :] = (load_x(sub) + _rms(y, gout_ref[...])).reshape(sub_b, sub_t, D_MODEL)

        return [piece(n) for n in range(D_MODEL // n_piece)] + [finish]

    key_row = lax.broadcasted_iota(jnp.int32, (span, HEAD_PAIR), 0)
    first_valid = jnp.where(t == 0, WINDOW, 0)
    contract_last = (((1,), (1,)), ((), ()))
    contract_first = (((0,), (0,)), ((), ()))

    def scores(s, chunks, j):
        g0 = j * GQA_GROUP * HEAD_DIM
        r0 = chunks[0] * CHUNK
        n_keys = span + (len(chunks) - 1) * CHUNK
        q_parts = []
        for c in chunks:
            q_rows = slice(s * tm + c * CHUNK, s * tm + (c + 1) * CHUNK)
            q_parts += [qbuf[q_rows, g0:g0 + HEAD_PAIR], qbuf[q_rows, g0 + HEAD_PAIR:g0 + 2 * HEAD_PAIR]]
        keys = jnp.concatenate([kzbuf[s, j, v_idx, r0:r0 + n_keys, :] for v_idx in range(2)], axis=0)
        sc_all = lax.dot_general(keys, jnp.concatenate(q_parts, axis=0), contract_last,
                                 preferred_element_type=F32)
        out = []
        for n, c in enumerate(chunks):
            tiles = []
            for v_idx in range(2):
                k0 = v_idx * n_keys + n * CHUNK
                sc = sc_all[k0:k0 + span, n * HEAD_PAIR:(n + 1) * HEAD_PAIR] + bias_ref[j, v_idx]
                if mask_start and c * CHUNK < WINDOW:
                    sc = jnp.where(key_row + c * CHUNK >= first_valid, sc, -jnp.inf)
                tiles.append(sc)
            out.append((c, tiles))
        return out

    def attend(s, c, j, scs):
        r0 = c * CHUNK
        q_rows = slice(s * tm + r0, s * tm + r0 + CHUNK)
        g0 = j * GQA_GROUP * HEAD_DIM
        probs = []
        for v_idx, sc in enumerate(scs):
            ha, hb = _pair_heads(j, v_idx)
            sink = jnp.where(low, sinks_ref[ha], sinks_ref[hb])
            m = jnp.maximum(jnp.max(sc, axis=0, keepdims=True), sink)
            p = jnp.exp(sc - m)
            denom = jnp.sum(p, axis=0, keepdims=True) + jnp.exp(sink - m)
            probs.append((p * (1.0 / denom)).astype(BF16))
        p_kn = jnp.concatenate(probs, axis=0)
        v_kd = jnp.concatenate([vzbuf[s, j, 0, r0:r0 + span, :],
                                vzbuf[s, j, 1, r0:r0 + span, :]], axis=0)
        o = lax.dot_general(p_kn, v_kd, contract_first, preferred_element_type=F32)
        obuf[q_rows, g0:g0 + HEAD_PAIR] = o[:CHUNK].astype(BF16)
        obuf[q_rows, g0 + HEAD_PAIR:g0 + 2 * HEAD_PAIR] = o[CHUNK:].astype(BF16)

    def units_of(sub):
        sl, r0 = sub
        c0, c1 = r0 // CHUNK, (r0 + sub_t) // CHUNK
        step = ATTN_CHUNKS_PER_UNIT if (c1 - c0) % ATTN_CHUNKS_PER_UNIT == 0 else 1
        return [(s, tuple(range(c, c + step)), j) for s in range(sl.start, sl.stop)
                for c in range(c0, c1, step) for j in range(N_KV_HEADS)]

    def run_units(units, fillers):
        every = max(1, len(units) // max(1, len(fillers)))
        pending = []

        def retire():
            (s, _, j), per_chunk = pending.pop(0)
            for c, tiles in per_chunk:
                attend(s, c, j, tiles)

        for n, unit in enumerate(units):
            pending.append((unit, scores(*unit)))
            if len(pending) > ATTN_LOOKAHEAD:
                retire()
            if fillers and (n + 1) % every == 0:
                fillers.pop(0)()
        while pending:
            retire()
        for filler in fillers:
            filler()

    for piece in in_pieces(subs[0]):
        piece()
    for i, sub in enumerate(subs):
        fillers = []
        if i > 0:
            fillers += out_pieces(subs[i - 1])
        if i + 1 < len(subs):
            fillers += in_pieces(subs[i + 1])
        run_units(units_of(sub), fillers)
    for piece in out_pieces(subs[-1]):
        piece()

    if nt > 1:
        kzbuf[:, :, :, 0:WINDOW, :] = kzbuf[:, :, :, tm:tm + WINDOW, :]
        vzbuf[:, :, :, 0:WINDOW, :] = vzbuf[:, :, :, tm:tm + WINDOW, :]


def _attn_layer(x, kc, vc, gin, gout, wqkv, bqkv, wo, bo, sinks, bias, *, layer, bb, tm, mask_start,
                ffn_cast=None):
    B, T, _ = x.shape
    nt = T // tm
    kv_rows = min(WINDOW, tm)
    body = functools.partial(_attn_body, bb=bb, tm=tm, nt=nt, mask_start=mask_start)
    return _mixer_call(
        body,
        grid=(B // bb, nt),
        out_shape=(jax.ShapeDtypeStruct(x.shape, F32),
                   jax.ShapeDtypeStruct((B, kv_rows, KV_DIM), F32),
                   jax.ShapeDtypeStruct((B, kv_rows, KV_DIM), F32)),
        in_specs=[
            _seq_spec(bb, tm, D_MODEL),
            _state_spec(bb, WINDOW, KV_DIM),
            _state_spec(bb, WINDOW, KV_DIM),
            _const_spec((1, D_MODEL)),
            _const_spec((1, D_MODEL)),
            _layer_spec(wqkv, layer),
            _const_spec(bqkv.shape),
            _layer_spec(wo, layer),
            _const_spec(bo.shape),
            pl.BlockSpec(memory_space=pltpu.SMEM),
            _const_spec(bias.shape),
        ],
        out_specs=(_seq_spec(bb, tm, D_MODEL),
                   _state_spec(bb, kv_rows, KV_DIM),
                   _state_spec(bb, kv_rows, KV_DIM)),
        scratch_shapes=[
            pltpu.VMEM((bb * tm, Q_DIM), BF16),
            pltpu.VMEM((bb, N_KV_HEADS, 2, WINDOW + tm, HEAD_PAIR), BF16),
            pltpu.VMEM((bb, N_KV_HEADS, 2, WINDOW + tm, HEAD_PAIR), BF16),
            pltpu.VMEM((bb * tm, Q_DIM), BF16),
            pltpu.VMEM((bb * tm, D_MODEL), F32),
        ],
        name="attn_mixer",
        ffn_cast=ffn_cast,
    )(x, kc, vc, gin, gout, wqkv, bqkv, wo, bo, sinks, bias)


def _conv_body(x_ref, st_ref, gin_ref, gout_ref, w1_ref, b1_ref, wdw_ref, bdw_ref, lng_ref, lnb_ref,
               w2_ref, b2_ref, o_ref, nst_ref, gbuf, ybuf, cbuf, *, bb, tm, nt):
    t = pl.program_id(1)
    rows = bb * tm
    hist0 = CONV_PAD - CONV_HIST

    @pl.when(t == 0)
    def _():
        gbuf[:, hist0:CONV_PAD, :] = st_ref[...]

    row = lax.broadcasted_iota(jnp.int32, (SUBLANES, CONV_LANE_TILE), 0)
    n_tiles = CONV_ROW_TILE // SUBLANES
    max_a = (hist0 + CONV_WIDTH - 1) // SUBLANES
    tiles_per_block = CONV_ROW_BLOCK // CONV_ROW_TILE

    def row_tiles(rb):
        out = []
        for i in range(tiles_per_block):
            flat = rb * CONV_ROW_BLOCK + i * CONV_ROW_TILE
            out.append((flat // tm, flat % tm, flat))
        return out

    def block_rows(ref, rb):
        flat = rb * CONV_ROW_BLOCK
        if tm >= CONV_ROW_BLOCK:
            return ref.at[flat // tm, flat % tm:flat % tm + CONV_ROW_BLOCK, :]
        return ref.at[flat // tm:(flat + CONV_ROW_BLOCK) // tm, :, :]

    def pointwise_in(rb):
        x = block_rows(x_ref, rb)[...].reshape(CONV_ROW_BLOCK, D_MODEL)
        h = _rms(x, gin_ref[...]).astype(BF16)
        ag = _dot(h, w1_ref[...]) + b1_ref[...]
        glu = ag[:, :D_MODEL] * jax.nn.sigmoid(ag[:, D_MODEL:])
        for i, (s, r0, _) in enumerate(row_tiles(rb)):
            gbuf[s, CONV_PAD + r0:CONV_PAD + r0 + CONV_ROW_TILE, :] = (
                glu[i * CONV_ROW_TILE:(i + 1) * CONV_ROW_TILE])

    def depthwise(rb):
        for l0 in range(0, D_MODEL, CONV_LANE_TILE):
            lanes = slice(l0, l0 + CONV_LANE_TILE)
            carried, carried_for = {}, None
            for s, r0, flat in row_tiles(rb):
                if carried_for != (s, r0):
                    carried = {}
                tiles = [gbuf[s, r0 + SUBLANES * j:r0 + SUBLANES * (j + 1), lanes]
                         for j in range(n_tiles + max_a)]
                out = [None] * n_tiles
                for b in range(SUBLANES):
                    taps = [a for a in range(max_a + 1)
                            if 0 <= SUBLANES * a + b - hist0 < CONV_WIDTH]
                    wts = [wdw_ref[SUBLANES * a + b - hist0:SUBLANES * a + b - hist0 + 1, lanes]
                           for a in taps]
                    z = []
                    for m in range(n_tiles + (1 if b else 0)):
                        if m == 0 and b in carried:
                            z.append(carried[b])
                            continue
                        acc = None
                        for a, wt in zip(taps, wts):
                            term = tiles[m + a] * wt
                            acc = term if acc is None else acc + term
                        z.append(acc)
                    if b:
                        carried[b] = z[n_tiles]
                    for m in range(n_tiles):
                        if b == 0:
                            part = z[m]
                        else:
                            part = pltpu.roll(jnp.where(row >= b, z[m], z[m + 1]), SUBLANES - b, 0)
                        out[m] = part if out[m] is None else out[m] + part
                carried_for = (s, r0 + CONV_ROW_TILE)
                for m in range(n_tiles):
                    lo = flat + SUBLANES * m
                    ybuf[lo:lo + SUBLANES, lanes] = out[m]
        for s, r0, flat in row_tiles(rb):
            rs = slice(flat, flat + CONV_ROW_TILE)
            y = _layernorm(ybuf[rs, :] + bdw_ref[...], lng_ref[...], lnb_ref[...])
            cbuf[rs, :] = (y * jax.nn.sigmoid(y)).astype(BF16)

    def pointwise_out(rb):
        flat = rb * CONV_ROW_BLOCK
        y = _dot(cbuf[flat:flat + CONV_ROW_BLOCK, :], w2_ref[...]) + b2_ref[...]
        x = block_rows(x_ref, rb)[...]
        block_rows(o_ref, rb)[...] = x + _rms(y, gout_ref[...]).reshape(x.shape)

    n_blocks = rows // CONV_ROW_BLOCK
    pointwise_in(0)
    for rb in range(1, n_blocks):
        depthwise(rb - 1)
        pointwise_in(rb)
        pointwise_out(rb - 1)
    depthwise(n_blocks - 1)
    pointwise_out(n_blocks - 1)

    last = gbuf[:, CONV_PAD + tm - CONV_HIST:CONV_PAD + tm, :]

    @pl.when(t == nt - 1)
    def _():
        nst_ref[...] = last

    if nt > 1:
        gbuf[:, hist0:CONV_PAD, :] = last


def _conv_layer(x, st, gin, gout, w1, b1, wdw, bdw, lng, lnb, w2, b2, *, bb, tm, ffn_cast=None):
    B, T, _ = x.shape
    nt = T // tm
    assert tm >= CONV_HIST and tm % CONV_ROW_TILE == 0
    body = functools.partial(_conv_body, bb=bb, tm=tm, nt=nt)
    return _mixer_call(
        body,
        grid=(B // bb, nt),
        out_shape=(jax.ShapeDtypeStruct(x.shape, F32),
                   jax.ShapeDtypeStruct((B, CONV_HIST, D_MODEL), F32)),
        in_specs=[
            _seq_spec(bb, tm, D_MODEL),
            _state_spec(bb, CONV_HIST, D_MODEL),
            _const_spec((1, D_MODEL)),
            _const_spec((1, D_MODEL)),
            _const_spec(w1.shape),
            _const_spec(b1.shape),
            _const_spec(wdw.shape),
            _const_spec(bdw.shape),
            _const_spec(lng.shape),
            _const_spec(lnb.shape),
            _const_spec(w2.shape),
            _const_spec(b2.shape),
        ],
        out_specs=(_seq_spec(bb, tm, D_MODEL), _state_spec(bb, CONV_HIST, D_MODEL)),
        scratch_shapes=[
            pltpu.VMEM((bb, CONV_PAD + tm, D_MODEL), F32),
            pltpu.VMEM((bb * tm, D_MODEL), F32),
            pltpu.VMEM((bb * tm, D_MODEL), BF16),
        ],
        name="conv_mixer",
        ffn_cast=ffn_cast,
    )(x, st, gin, gout, w1, b1, wdw, bdw, lng, lnb, w2, b2)


def _cmlp_body(x_ref, gin_ref, gout_ref, win_ref, bin_ref, lng_ref, lnb_ref, ws_ref, bs_ref,
               wout_ref, bout_ref, *rest, bb, tm, blk, emit_v):
    if emit_v:
        o_ref, v_ref, vbuf, ybuf = rest
    else:
        o_ref, vbuf, ybuf = rest
    rows = bb * tm
    x = x_ref[...].reshape(rows, D_MODEL)
    h = _rms(x, gin_ref[...]).astype(BF16)
    hv = _gelu(_dot(h, win_ref[:, CMLP_DIM:]) + bin_ref[:, CMLP_DIM:])
    vln = _layernorm(hv, lng_ref[...], lnb_ref[...])
    if emit_v:
        v_ref[...] = vln.reshape(bb, tm, CMLP_DIM)
    vbuf[...] = vln.astype(BF16)

    pos_r = lax.broadcasted_iota(jnp.int32, (blk, blk), 0) // CHUNK
    pos_c = lax.broadcasted_iota(jnp.int32, (blk, blk), 1) // CHUNK
    for g in range(CMLP_GROUPS):
        lanes = slice(g * CMLP_GROUP_DIM, (g + 1) * CMLP_GROUP_DIM)
        ug = _gelu(_dot(h, win_ref[:, lanes]) + bin_ref[:, lanes])
        wsm = jnp.where(pos_c <= pos_r, ws_ref[g], 0.0).astype(BF16)
        for n in range(rows // blk):
            rs = slice(n * blk, (n + 1) * blk)
            gate = _dot(wsm, vbuf[rs, lanes]) + bs_ref[g]
            ybuf[rs, lanes] = (ug[rs] * gate).astype(BF16)

    y = _dot(ybuf[...], wout_ref[...]) + bout_ref[...]
    o_ref[...] = (x + _rms(y, gout_ref[...])).reshape(bb, tm, D_MODEL)


def _cmlp_layer(x, gin, gout, win, bin_, lng, lnb, ws, bs, wout, bout, *, bb, tm, emit_v, ffn_cast=None):
    B, T, _ = x.shape
    blk = min(T, MIX_BLOCK)
    assert tm % blk == 0
    ws = ws[:, :blk, :blk]
    bs = bs[:, :blk, None]
    body = functools.partial(_cmlp_body, bb=bb, tm=tm, blk=blk, emit_v=emit_v)
    out_shape = [jax.ShapeDtypeStruct(x.shape, F32)]
    out_specs = [_seq_spec(bb, tm, D_MODEL)]
    if emit_v:
        out_shape.append(jax.ShapeDtypeStruct((B, T, CMLP_DIM), F32))
        out_specs.append(_seq_spec(bb, tm, CMLP_DIM))
    return _mixer_call(
        body,
        grid=(B // bb, T // tm),
        out_shape=tuple(out_shape),
        in_specs=[
            _seq_spec(bb, tm, D_MODEL),
            _const_spec((1, D_MODEL)),
            _const_spec((1, D_MODEL)),
            _const_spec(win.shape),
            _const_spec(bin_.shape),
            _const_spec(lng.shape),
            _const_spec(lnb.shape),
            _const_spec(ws.shape),
            _const_spec(bs.shape),
            _const_spec(wout.shape),
            _const_spec(bout.shape),
        ],
        out_specs=tuple(out_specs),
        scratch_shapes=[
            pltpu.VMEM((bb * tm, CMLP_DIM), BF16),
            pltpu.VMEM((bb * tm, CMLP_DIM), BF16),
        ],
        name="cmlp_mixer",
        ffn_cast=ffn_cast,
    )(x, gin, gout, win, bin_, lng, lnb, ws, bs, wout, bout)


def _ffn_body(x_ref, st_ref, gin_ref, gout_ref, wup_ref, wdw_ref, bdw_ref, wdn_ref,
              o_ref, nst_ref, actbuf, *upbufs, bb, tm, nt):
    n_chunks = D_FF // FFN_CHUNK
    t = pl.program_id(1)
    rows = bb * tm
    hist0 = FFN_PAD - FFN_HIST
    n_tiles = tm // SUBLANES
    tile_row = lax.broadcasted_iota(jnp.int32, (bb * n_tiles, SUBLANES, FFN_CHUNK), 1)

    def halves(c):
        return ((c * FFN_CHUNK, slice(0, FFN_CHUNK)),
                (D_FF + c * FFN_CHUNK, slice(FFN_CHUNK, 2 * FFN_CHUNK)))

    @pl.when(t == 0)
    def _():
        for c in range(n_chunks):
            upbufs[c][:, 0:hist0, :] = jnp.zeros((bb, hist0, 2 * FFN_CHUNK), F32)
            for lo, lanes in halves(c):
                upbufs[c][:, hist0:FFN_PAD, lanes] = st_ref[:, :, lo:lo + FFN_CHUNK]

    def up_proj(h, c):
        for lo, lanes in halves(c):
            up = _dot(h, wup_ref[:, lo:lo + FFN_CHUNK])
            upbufs[c][:, FFN_PAD:FFN_PAD + tm, lanes] = up.reshape(bb, tm, FFN_CHUNK)

    def delayed(cur, hist):
        cur_r = pltpu.roll(cur, 1, 1)
        hist_r = pltpu.roll(hist, 1, 1)
        cur_r4 = cur_r.reshape(bb, n_tiles, SUBLANES, FFN_CHUNK)
        prev_r = jnp.concatenate([hist_r[:, None], cur_r4[:, :n_tiles - 1]], axis=1)
        out = jnp.where(tile_row < 1, prev_r.reshape(bb * n_tiles, SUBLANES, FFN_CHUNK), cur_r)
        return out, hist_r

    def conv(c, lo, lanes):
        w = wdw_ref[:, lo:lo + FFN_CHUNK]
        cur = upbufs[c][:, FFN_PAD:FFN_PAD + tm, lanes].reshape(bb * n_tiles, SUBLANES, FFN_CHUNK)
        hist = upbufs[c][:, 0:FFN_PAD, lanes]
        acc, acc_hist = None, None
        for kk in range(FFN_HIST):
            term, term_hist = cur * w[kk:kk + 1], hist * w[kk:kk + 1]
            if acc is not None:
                term, term_hist = term + acc, term_hist + acc_hist
            acc, acc_hist = delayed(term, term_hist)
        out = cur * w[FFN_HIST:FFN_HIST + 1] + bdw_ref[:, lo:lo + FFN_CHUNK] + acc
        return out.reshape(rows, FFN_CHUNK)

    x = x_ref[...].reshape(rows, D_MODEL)
    h = _rms(x, gin_ref[...]).astype(BF16)
    for c in range(n_chunks):
        up_proj(h, c)
        (glo, glanes), (ulo, ulanes) = halves(c)
        act = _gelu(conv(c, glo, glanes)) * conv(c, ulo, ulanes)
        actbuf[:, c * FFN_CHUNK:(c + 1) * FFN_CHUNK] = act.astype(BF16)

    part_rows = rows // FFN_DOWN_ROW_PARTS
    for rp in range(FFN_DOWN_ROW_PARTS):
        rs = slice(rp * part_rows, (rp + 1) * part_rows)
        out = x[rs] + _rms(_dot(actbuf[rs, :], wdn_ref[...]), gout_ref[...])
        if bb == 1:
            o_ref[0, rs, :] = out
        else:
            nb = part_rows // tm
            o_ref[rp * nb:(rp + 1) * nb] = out.reshape(nb, tm, D_MODEL)

    lasts = [(lo, lanes, upbufs[c][:, FFN_PAD + tm - FFN_HIST:FFN_PAD + tm, lanes])
             for c in range(n_chunks) for lo, lanes in halves(c)]
    if nt > 1:
        for i, (lo, lanes, last) in enumerate(lasts):
            upbufs[i // 2][:, hist0:FFN_PAD, lanes] = last

    @pl.when(t == nt - 1)
    def _():
        for lo, lanes, last in lasts:
            nst_ref[:, :, lo:lo + FFN_CHUNK] = last


def _ffn_layer(x, st, gin, gout, wup, wdw, bdw, wdn, *, bb, tm):
    B, T, _ = x.shape
    nt = T // tm
    assert tm >= FFN_HIST and D_FF % FFN_CHUNK == 0
    body = functools.partial(_ffn_body, bb=bb, tm=tm, nt=nt)
    return pl.pallas_call(
        body,
        grid=(B // bb, nt),
        out_shape=(jax.ShapeDtypeStruct(x.shape, F32),
                   jax.ShapeDtypeStruct((B, FFN_HIST, 2 * D_FF), F32)),
        in_specs=[
            _seq_spec(bb, tm, D_MODEL),
            _state_spec(bb, FFN_HIST, 2 * D_FF),
            _const_spec((1, D_MODEL)),
            _const_spec((1, D_MODEL)),
            _const_spec(wup.shape),
            _const_spec(wdw.shape),
            _const_spec(bdw.shape),
            _const_spec(wdn.shape),
        ],
        out_specs=(_seq_spec(bb, tm, D_MODEL), _state_spec(bb, FFN_HIST, 2 * D_FF)),
        scratch_shapes=(
            [pltpu.VMEM((bb * tm, D_FF), BF16)]
            + [pltpu.VMEM((bb, FFN_PAD + tm, 2 * FFN_CHUNK), F32)] * (D_FF // FFN_CHUNK)),
        compiler_params=_params(),
        name="conv_ffn",
    )(x, st, gin, gout, wup, wdw, bdw, wdn)


def _trunk(x, caches, p, bias, ffn_bf16, *, bb, tm):
    prompt = caches is None
    cast_here = ffn_bf16 is None
    ffn_bf16 = [] if cast_here else ffn_bf16
    B = x.shape[0]
    new_k, new_v, new_conv, new_cv, new_ffn = [], [], [], [], []
    for i in range(DEPTH):
        kind, j = i % N_MIXERS, i // N_MIXERS
        g = p['norm_gain'][i]
        gains = [g[n][None, :] for n in range(4)]
        ffn_cast = (p['ffn_w_up'], p['ffn_w_down'], i) if cast_here else None
        if kind == 0:
            if prompt:
                kc = jnp.zeros((B, WINDOW, KV_DIM), F32)
                vc = kc
            else:
                kc = caches[0][j].reshape(B, WINDOW, KV_DIM)
                vc = caches[1][j].reshape(B, WINDOW, KV_DIM)
            x, k_rows, v_rows, *cast = _attn_layer(
                x, kc, vc, gains[0], gains[1], p['attn_w_qkv'], p['attn_b_qkv'][j][None, :],
                p['attn_w_o'], p['attn_b_o'][j][None, :], p['attn_sinks'][j], bias, layer=j,
                bb=bb, tm=tm, mask_start=prompt, ffn_cast=ffn_cast)
            new_k.append(k_rows.reshape(B, -1, N_KV_HEADS, HEAD_DIM))
            new_v.append(v_rows.reshape(B, -1, N_KV_HEADS, HEAD_DIM))
        elif kind == 1:
            st = jnp.zeros((B, CONV_HIST, D_MODEL), F32) if prompt else caches[2][j]
            x, s, *cast = _conv_layer(
                x, st, gains[0], gains[1], p['conv_w_pw1'][j], p['conv_b_pw1'][j][None, :],
                p['conv_w_dw'][j], p['conv_b_dw'][j][None, :], p['conv_ln_g'][j][None, :],
                p['conv_ln_b'][j][None, :], p['conv_w_pw2'][j], p['conv_b_pw2'][j][None, :],
                bb=bb, tm=tm, ffn_cast=ffn_cast)
            new_conv.append(s)
        else:
            res = _cmlp_layer(
                x, gains[0], gains[1], p['cmlp_w_in'][j], p['cmlp_b_in'][j][None, :],
                p['cmlp_ln_g'][j][None, :], p['cmlp_ln_b'][j][None, :], p['cmlp_w_s'][j],
                p['cmlp_b_s'][j], p['cmlp_w_out'][j], p['cmlp_b_out'][j][None, :],
                bb=bb, tm=tm, emit_v=not prompt, ffn_cast=ffn_cast)
            x = res[0]
            if not prompt:
                new_cv.append(res[1])
            cast = res[-2:] if cast_here else []
        st = jnp.zeros((B, FFN_HIST, 2 * D_FF), F32) if prompt else caches[3][i]
        if cast_here:
            ffn_bf16.append(tuple(cast))
        wup, wdn = ffn_bf16[i]
        x, s = _ffn_layer(x, st, gains[2], gains[3], wup, p['ffn_w_dw'][i],
                          p['ffn_b_dw'][i][None, :], wdn, bb=bb, tm=tm)
        new_ffn.append(s)
    cv = jnp.stack(new_cv) if new_cv else None
    return (x, jnp.stack(new_k), jnp.stack(new_v), jnp.stack(new_conv), cv, jnp.stack(new_ffn)), ffn_bf16


def kernel(x_prompt, x_sample, cache_attn_k, cache_attn_v, state_conv, state_ffn_conv, rel_bias_table, norm_gain, attn_w_qkv, attn_b_qkv, attn_w_o, attn_b_o, attn_sinks, conv_w_pw1, conv_b_pw1, conv_w_dw, conv_b_dw, conv_ln_g, conv_ln_b, conv_w_pw2, conv_b_pw2, cmlp_w_in, cmlp_b_in, cmlp_ln_g, cmlp_ln_b, cmlp_w_s, cmlp_b_s, cmlp_w_out, cmlp_b_out, ffn_w_up, ffn_w_dw, ffn_b_dw, ffn_w_down):
    p = {
        'norm_gain': norm_gain,
        'attn_w_qkv': attn_w_qkv.astype(BF16), 'attn_b_qkv': attn_b_qkv,
        'attn_w_o': attn_w_o.astype(BF16), 'attn_b_o': attn_b_o, 'attn_sinks': attn_sinks,
        'conv_w_pw1': conv_w_pw1.astype(BF16), 'conv_b_pw1': conv_b_pw1, 'conv_w_dw': conv_w_dw,
        'conv_b_dw': conv_b_dw, 'conv_ln_g': conv_ln_g, 'conv_ln_b': conv_ln_b,
        'conv_w_pw2': conv_w_pw2.astype(BF16), 'conv_b_pw2': conv_b_pw2,
        'cmlp_w_in': cmlp_w_in.astype(BF16), 'cmlp_b_in': cmlp_b_in, 'cmlp_ln_g': cmlp_ln_g,
        'cmlp_ln_b': cmlp_ln_b, 'cmlp_w_s': cmlp_w_s, 'cmlp_b_s': cmlp_b_s,
        'cmlp_w_out': cmlp_w_out.astype(BF16), 'cmlp_b_out': cmlp_b_out,
        'ffn_w_up': ffn_w_up, 'ffn_w_dw': ffn_w_dw, 'ffn_b_dw': ffn_b_dw, 'ffn_w_down': ffn_w_down,
    }
    bias = _rel_bias(rel_bias_table)
    (y_prompt, p_attn_k, p_attn_v, p_conv, _, p_ffn_conv), ffn_bf16 = _trunk(
        x_prompt, None, p, bias, None, bb=1, tm=512)
    (y_sample, s_attn_k, s_attn_v, s_conv, s_cmlp_v, s_ffn_conv), _ = _trunk(
        x_sample, (cache_attn_k, cache_attn_v, state_conv, state_ffn_conv), p, bias, ffn_bf16, bb=8, tm=64)
    return (y_prompt, y_sample, p_attn_k, p_attn_v, p_conv, p_ffn_conv,
            s_attn_k, s_attn_v, s_conv, s_cmlp_v, s_ffn_conv)
```

```python
import functools
import math

import jax
import jax.numpy as jnp
from jax import lax
from jax.experimental import pallas as pl
from jax.experimental.pallas import tpu as pltpu

D_MODEL = 1024
DEPTH = 4
CHUNK = 64
N_MIXERS = 3
N_HEADS = 16
N_KV_HEADS = 4
HEAD_DIM = 64
GQA_GROUP = N_HEADS // N_KV_HEADS
WINDOW = 128
KV_DIM = N_KV_HEADS * HEAD_DIM
Q_DIM = N_HEADS * HEAD_DIM
NUM_BUCKETS = 32
MAX_DISTANCE = 128
CONV_WIDTH = 31
CONV_HIST = CONV_WIDTH - 1
MIX_BLOCK = 128
CMLP_GROUPS = 4
CMLP_DIM = 2 * D_MODEL
CMLP_GROUP_DIM = CMLP_DIM // CMLP_GROUPS
D_FF = 2816
FFN_CONV_WIDTH = 3
FFN_HIST = FFN_CONV_WIDTH - 1
RMS_EPS = 1e-6
_GELU_C0 = math.sqrt(2.0 / math.pi)
_GELU_C1 = _GELU_C0 * 0.044715
LN_EPS = 1e-5

F32 = jnp.float32
BF16 = jnp.bfloat16

V7X_VMEM_BYTES = 64 * 1024 * 1024
VMEM_LIMIT_BYTES = V7X_VMEM_BYTES - 8 * 1024 * 1024
SUBLANES = 8
BF16_SUBLANES = 2 * SUBLANES
FFN_CHUNK = 256
FFN_DOWN_ROW_PARTS = 4
FFN_PAD = SUBLANES
CONV_PAD = 32
CONV_ROW_TILE = 64
CONV_LANE_TILE = 128
CONV_ROW_BLOCK = 256


def _rms(x, g):
    ms = jnp.mean(x * x, axis=-1, keepdims=True)
    return x * lax.rsqrt(ms + RMS_EPS) * g


def _layernorm(x, g, b):
    mu = jnp.mean(x, axis=-1, keepdims=True)
    xc = x - mu
    var = jnp.mean(xc * xc, axis=-1, keepdims=True)
    return xc * lax.rsqrt(var + LN_EPS) * g + b


def _gelu(x):
    hx = 0.5 * x
    return hx + hx * jnp.tanh(x * (_GELU_C0 + _GELU_C1 * (x * x)))


def _dot(a, b):
    return jnp.dot(a, b, preferred_element_type=F32)


def _const_spec(shape):
    zeros = (0,) * len(shape)
    return pl.BlockSpec(shape, lambda b, t: zeros, pipeline_mode=pl.Buffered(1))


def _layer_spec(stacked, layer):
    zeros = (0,) * (stacked.ndim - 1)
    return pl.BlockSpec((None,) + stacked.shape[1:], lambda b, t: (layer,) + zeros,
                        pipeline_mode=pl.Buffered(1))


def _seq_spec(bb, rows, width):
    return pl.BlockSpec((bb, rows, width), lambda b, t: (b, t, 0))


def _state_spec(bb, rows, width):
    return pl.BlockSpec((bb, rows, width), lambda b, t: (b, 0, 0))


def _params(flags=None):
    return pltpu.CompilerParams(
        dimension_semantics=("arbitrary", "arbitrary"),
        vmem_limit_bytes=VMEM_LIMIT_BYTES,
        flags=flags,
    )


def _mixer_call(body, *, grid, out_shape, in_specs, out_specs, scratch_shapes, name, ffn_cast=None):
    out_shape, out_specs, in_specs = tuple(out_shape), tuple(out_specs), list(in_specs)
    extra_args = ()
    if ffn_cast is not None:
        wup, wdn, layer = ffn_cast
        nt, steps = grid[1], grid[0] * grid[1]
        up_rows, dn_rows = D_MODEL // steps, D_FF // (steps // 2)
        assert up_rows * steps == D_MODEL and dn_rows * (steps // 2) == D_FF
        assert up_rows % BF16_SUBLANES == 0 and dn_rows % BF16_SUBLANES == 0
        n_in, n_out = len(in_specs), len(out_specs)
        mixer_body = body

        def body(*refs):
            cast_in = refs[n_in:n_in + 2]
            cast_out = refs[n_in + 2 + n_out:n_in + 4 + n_out]
            for src, dst in zip(cast_in, cast_out):
                dst[...] = src[...].astype(BF16)
            mixer_body(*refs[:n_in], *refs[n_in + 2:n_in + 2 + n_out], *refs[n_in + 4 + n_out:])

        in_specs += [pl.BlockSpec((None, up_rows, 2 * D_FF), lambda b, t: (layer, b * nt + t, 0)),
                     pl.BlockSpec((None, dn_rows, D_MODEL), lambda b, t: (layer, (b * nt + t) // 2, 0))]
        out_specs += (pl.BlockSpec((up_rows, 2 * D_FF), lambda b, t: (b * nt + t, 0)),
                      pl.BlockSpec((dn_rows, D_MODEL), lambda b, t: ((b * nt + t) // 2, 0)))
        out_shape += (jax.ShapeDtypeStruct((D_MODEL, 2 * D_FF), BF16),
                      jax.ShapeDtypeStruct((D_FF, D_MODEL), BF16))
        extra_args = (wup, wdn)
    call = pl.pallas_call(body, grid=grid, out_shape=out_shape, in_specs=in_specs, out_specs=out_specs,
                          scratch_shapes=scratch_shapes, compiler_params=_params(), name=name)
    return lambda *args: call(*args, *extra_args)


HEAD_PAIR = 2 * HEAD_DIM
ATTN_LOOKAHEAD = 3
ATTN_CHUNKS_PER_UNIT = 2
ATTN_ROW_SPLIT = 2
PAIR_STEP = GQA_GROUP // 2


def _pair_heads(j, v):
    return GQA_GROUP * j + v, GQA_GROUP * j + PAIR_STEP + v


def _bias_body(bucket_ref, table_ref, o_ref):
    bk = bucket_ref[...]
    low = lax.broadcasted_iota(jnp.int32, (1, HEAD_PAIR), 1) < CHUNK
    for j in range(N_KV_HEADS):
        for v in range(2):
            ha, hb = _pair_heads(j, v)
            acc = jnp.zeros(bk.shape, F32)
            for b in range(NUM_BUCKETS):
                val = jnp.where(low, table_ref[b, ha], table_ref[b, hb])
                acc = jnp.where(bk == b, val, acc)
            o_ref[j, v] = acc


def _t5_bucket(rel):
    half = NUM_BUCKETS // 2
    max_exact = half // 2
    n = jnp.abs(rel)
    log_ratio = jnp.log(jnp.maximum(n, 1).astype(F32) / max_exact) / math.log(MAX_DISTANCE / max_exact)
    large = jnp.minimum(max_exact + (log_ratio * (half - max_exact)).astype(jnp.int32), half - 1)
    return jnp.where(rel > 0, half, 0) + jnp.where(n < max_exact, n, large)


def _rel_bias(table):
    q_pos = jnp.arange(CHUNK, dtype=jnp.int32)
    k_pos = jnp.arange(WINDOW + CHUNK, dtype=jnp.int32) - WINDOW
    bucket = _t5_bucket(k_pos[:, None] - q_pos[None, :]).astype(jnp.int32)
    bucket = jnp.concatenate([bucket, bucket], axis=1)
    return pl.pallas_call(
        _bias_body,
        out_shape=jax.ShapeDtypeStruct((N_KV_HEADS, 2, WINDOW + CHUNK, HEAD_PAIR), F32),
        in_specs=[pl.BlockSpec(memory_space=pltpu.VMEM), pl.BlockSpec(memory_space=pltpu.SMEM)],
        out_specs=pl.BlockSpec(memory_space=pltpu.VMEM),
        name="rel_bias",
    )(bucket, table)


def _attn_body(x_ref, kc_ref, vc_ref, gin_ref, gout_ref, wqkv_ref, bqkv_ref, wo_ref, bo_ref,
               sinks_ref, bias_ref, o_ref, kout_ref, vout_ref, qbuf, kzbuf, vzbuf, obuf, ybuf,
               *, bb, tm, nt, mask_start):
    t = pl.program_id(1)
    rows = bb * tm
    kv_rows = min(WINDOW, tm)
    span = WINDOW + CHUNK
    low = lax.broadcasted_iota(jnp.int32, (1, HEAD_PAIR), 1) < CHUNK

    def expand(buf, sl, r0, kv):
        nb, n = kv.shape[0], kv.shape[1]
        for pair in range(N_KV_HEADS // 2):
            tile = kv[:, :, pair * HEAD_PAIR:(pair + 1) * HEAD_PAIR]
            swapped = pltpu.roll(tile.reshape(nb * n, HEAD_PAIR), HEAD_DIM, 1).reshape(nb, n, HEAD_PAIR)
            for j, (lo_src, hi_src) in ((2 * pair, (tile, swapped)), (2 * pair + 1, (swapped, tile))):
                buf[sl, j, 0, r0:r0 + n, :] = jnp.where(low, lo_src, 0.0).astype(BF16)
                buf[sl, j, 1, r0:r0 + n, :] = jnp.where(low, 0.0, hi_src).astype(BF16)

    @pl.when(t == 0)
    def _():
        expand(kzbuf, slice(0, bb), 0, kc_ref[...])
        expand(vzbuf, slice(0, bb), 0, vc_ref[...])

    if bb >= ATTN_ROW_SPLIT:
        sub_b, sub_t = bb // ATTN_ROW_SPLIT, tm
        subs = [(slice(i * sub_b, (i + 1) * sub_b), 0) for i in range(ATTN_ROW_SPLIT)]
    else:
        sub_b, sub_t = bb, tm // ATTN_ROW_SPLIT
        subs = [(slice(0, bb), i * sub_t) for i in range(ATTN_ROW_SPLIT)]
    sub_rows = sub_b * sub_t
    assert sub_t % CHUNK == 0 and (sub_b > 1 or sub_t >= kv_rows or ATTN_ROW_SPLIT == 1)
    n_piece = GQA_GROUP * HEAD_DIM

    def flat0(sub):
        sl, r0 = sub
        return sl.start * tm + r0

    def load_x(sub):
        sl, r0 = sub
        return x_ref[sl, r0:r0 + sub_t, :].reshape(sub_rows, D_MODEL)

    def in_pieces(sub):
        sl, r0 = sub
        h = _rms(load_x(sub), gin_ref[...]).astype(BF16)
        f0 = flat0(sub)

        def kv_piece(col, buf, out_ref):
            def run():
                kv = (_dot(h, wqkv_ref[:, col:col + KV_DIM]) + bqkv_ref[:, col:col + KV_DIM]).reshape(
                    sub_b, sub_t, KV_DIM)
                expand(buf, sl, WINDOW + r0, kv)
                if sub_b > 1 or r0 + sub_t == tm:
                    out_ref[sl] = kv[:, sub_t - kv_rows:, :]
            return run

        def q_piece(j):
            def run():
                col = j * n_piece
                q = _dot(h, wqkv_ref[:, col:col + n_piece]) + bqkv_ref[:, col:col + n_piece]
                qbuf[f0:f0 + sub_rows, col:col + n_piece] = (q * (HEAD_DIM ** -0.5)).astype(BF16)
            return run

        return ([kv_piece(Q_DIM, kzbuf, kout_ref), kv_piece(Q_DIM + KV_DIM, vzbuf, vout_ref)]
                + [q_piece(j) for j in range(N_KV_HEADS)])

    def out_pieces(sub):
        sl, r0 = sub
        f0 = flat0(sub)

        def piece(n):
            def run():
                col = n * n_piece
                ybuf[f0:f0 + sub_rows, col:col + n_piece] = (
                    _dot(obuf[f0:f0 + sub_rows, :], wo_ref[:, col:col + n_piece]) + bo_ref[:, col:col + n_piece])
            return run

        def finish():
            y = ybuf[f0:f0 + sub_rows, :]
            o_ref[sl, r0:r0 + sub_t, :] = (load_x(sub) + _rms(y, gout_ref[...])).reshape(sub_b, sub_t, D_MODEL)

        return [piece(n) for n in range(D_MODEL // n_piece)] + [finish]

    key_row = lax.broadcasted_iota(jnp.int32, (span, HEAD_PAIR), 0)
    first_valid = jnp.where(t == 0, WINDOW, 0)
    contract_last = (((1,), (1,)), ((), ()))
    contract_first = (((0,), (0,)), ((), ()))

    def scores(s, chunks, j):
        g0 = j * GQA_GROUP * HEAD_DIM
        r0 = chunks[0] * CHUNK
        n_keys = span + (len(chunks) - 1) * CHUNK
        q_parts = []
        for c in chunks:
            q_rows = slice(s * tm + c * CHUNK, s * tm + (c + 1) * CHUNK)
            q_parts += [qbuf[q_rows, g0:g0 + HEAD_PAIR], qbuf[q_rows, g0 + HEAD_PAIR:g0 + 2 * HEAD_PAIR]]
        keys = jnp.concatenate([kzbuf[s, j, v_idx, r0:r0 + n_keys, :] for v_idx in range(2)], axis=0)
        sc_all = lax.dot_general(keys, jnp.concatenate(q_parts, axis=0), contract_last,
                                 preferred_element_type=F32)
        out = []
        for n, c in enumerate(chunks):
            tiles = []
            for v_idx in range(2):
                k0 = v_idx * n_keys + n * CHUNK
                sc = sc_all[k0:k0 + span, n * HEAD_PAIR:(n + 1) * HEAD_PAIR] + bias_ref[j, v_idx]
                if mask_start and c * CHUNK < WINDOW:
                    sc = jnp.where(key_row + c * CHUNK >= first_valid, sc, -jnp.inf)
                tiles.append(sc)
            out.append((c, tiles))
        return out

    def attend(s, c, j, scs):
        r0 = c * CHUNK
        q_rows = slice(s * tm + r0, s * tm + r0 + CHUNK)
        g0 = j * GQA_GROUP * HEAD_DIM
        probs = []
        for v_idx, sc in enumerate(scs):
            ha, hb = _pair_heads(j, v_idx)
            sink = jnp.where(low, sinks_ref[ha], sinks_ref[hb])
            m = jnp.maximum(jnp.max(sc, axis=0, keepdims=True), sink)
            p = jnp.exp(sc - m)
            denom = jnp.sum(p, axis=0, keepdims=True) + jnp.exp(sink - m)
            probs.append((p * (1.0 / denom)).astype(BF16))
        p_kn = jnp.concatenate(probs, axis=0)
        v_kd = jnp.concatenate([vzbuf[s, j, 0, r0:r0 + span, :],
                                vzbuf[s, j, 1, r0:r0 + span, :]], axis=0)
        o = lax.dot_general(p_kn, v_kd, contract_first, preferred_element_type=F32)
        obuf[q_rows, g0:g0 + HEAD_PAIR] = o[:CHUNK].astype(BF16)
        obuf[q_rows, g0 + HEAD_PAIR:g0 + 2 * HEAD_PAIR] = o[CHUNK:].astype(BF16)

    def units_of(sub):
        sl, r0 = sub
        c0, c1 = r0 // CHUNK, (r0 + sub_t) // CHUNK
        step = ATTN_CHUNKS_PER_UNIT if (c1 - c0) % ATTN_CHUNKS_PER_UNIT == 0 else 1
        return [(s, tuple(range(c, c + step)), j) for s in range(sl.start, sl.stop)
                for c in range(c0, c1, step) for j in range(N_KV_HEADS)]

    def run_units(units, fillers):
        every = max(1, len(units) // max(1, len(fillers)))
        pending = []

        def retire():
            (s, _, j), per_chunk = pending.pop(0)
            for c, tiles in per_chunk:
                attend(s, c, j, tiles)

        for n, unit in enumerate(units):
            pending.append((unit, scores(*unit)))
            if len(pending) > ATTN_LOOKAHEAD:
                retire()
            if fillers and (n + 1) % every == 0:
                fillers.pop(0)()
        while pending:
            retire()
        for filler in fillers:
            filler()

    for piece in in_pieces(subs[0]):
        piece()
    for i, sub in enumerate(subs):
        fillers = []
        if i > 0:
            fillers += out_pieces(subs[i - 1])
        if i + 1 < len(subs):
            fillers += in_pieces(subs[i + 1])
        run_units(units_of(sub), fillers)
    for piece in out_pieces(subs[-1]):
        piece()

    if nt > 1:
        kzbuf[:, :, :, 0:WINDOW, :] = kzbuf[:, :, :, tm:tm + WINDOW, :]
        vzbuf[:, :, :, 0:WINDOW, :] = vzbuf[:, :, :, tm:tm + WINDOW, :]


def _attn_layer(x, kc, vc, gin, gout, wqkv, bqkv, wo, bo, sinks, bias, *, layer, bb, tm, mask_start,
                ffn_cast=None):
    B, T, _ = x.shape
    nt = T // tm
    kv_rows = min(WINDOW, tm)
    body = functools.partial(_attn_body, bb=bb, tm=tm, nt=nt, mask_start=mask_start)
    return _mixer_call(
        body,
        grid=(B // bb, nt),
        out_shape=(jax.ShapeDtypeStruct(x.shape, F32),
                   jax.ShapeDtypeStruct((B, kv_rows, KV_DIM), F32),
                   jax.ShapeDtypeStruct((B, kv_rows, KV_DIM), F32)),
        in_specs=[
            _seq_spec(bb, tm, D_MODEL),
            _state_spec(bb, WINDOW, KV_DIM),
            _state_spec(bb, WINDOW, KV_DIM),
            _const_spec((1, D_MODEL)),
            _const_spec((1, D_MODEL)),
            _layer_spec(wqkv, layer),
            _const_spec(bqkv.shape),
            _layer_spec(wo, layer),
            _const_spec(bo.shape),
            pl.BlockSpec(memory_space=pltpu.SMEM),
            _const_spec(bias.shape),
        ],
        out_specs=(_seq_spec(bb, tm, D_MODEL),
                   _state_spec(bb, kv_rows, KV_DIM),
                   _state_spec(bb, kv_rows, KV_DIM)),
        scratch_shapes=[
            pltpu.VMEM((bb * tm, Q_DIM), BF16),
            pltpu.VMEM((bb, N_KV_HEADS, 2, WINDOW + tm, HEAD_PAIR), BF16),
            pltpu.VMEM((bb, N_KV_HEADS, 2, WINDOW + tm, HEAD_PAIR), BF16),
            pltpu.VMEM((bb * tm, Q_DIM), BF16),
            pltpu.VMEM((bb * tm, D_MODEL), F32),
        ],
        name="attn_mixer",
        ffn_cast=ffn_cast,
    )(x, kc, vc, gin, gout, wqkv, bqkv, wo, bo, sinks, bias)


def _conv_body(x_ref, st_ref, gin_ref, gout_ref, w1_ref, b1_ref, wdw_ref, bdw_ref, lng_ref, lnb_ref,
               w2_ref, b2_ref, o_ref, nst_ref, gbuf, ybuf, cbuf, *, bb, tm, nt):
    t = pl.program_id(1)
    rows = bb * tm
    hist0 = CONV_PAD - CONV_HIST

    @pl.when(t == 0)
    def _():
        gbuf[:, hist0:CONV_PAD, :] = st_ref[...]

    row = lax.broadcasted_iota(jnp.int32, (SUBLANES, CONV_LANE_TILE), 0)
    n_tiles = CONV_ROW_TILE // SUBLANES
    max_a = (hist0 + CONV_WIDTH - 1) // SUBLANES
    tiles_per_block = CONV_ROW_BLOCK // CONV_ROW_TILE

    def row_tiles(rb):
        out = []
        for i in range(tiles_per_block):
            flat = rb * CONV_ROW_BLOCK + i * CONV_ROW_TILE
            out.append((flat // tm, flat % tm, flat))
        return out

    def block_rows(ref, rb):
        flat = rb * CONV_ROW_BLOCK
        if tm >= CONV_ROW_BLOCK:
            return ref.at[flat // tm, flat % tm:flat % tm + CONV_ROW_BLOCK, :]
        return ref.at[flat // tm:(flat + CONV_ROW_BLOCK) // tm, :, :]

    def pointwise_in(rb):
        x = block_rows(x_ref, rb)[...].reshape(CONV_ROW_BLOCK, D_MODEL)
        h = _rms(x, gin_ref[...]).astype(BF16)
        ag = _dot(h, w1_ref[...]) + b1_ref[...]
        glu = ag[:, :D_MODEL] * jax.nn.sigmoid(ag[:, D_MODEL:])
        for i, (s, r0, _) in enumerate(row_tiles(rb)):
            gbuf[s, CONV_PAD + r0:CONV_PAD + r0 + CONV_ROW_TILE, :] = (
                glu[i * CONV_ROW_TILE:(i + 1) * CONV_ROW_TILE])

    def depthwise(rb):
        for l0 in range(0, D_MODEL, CONV_LANE_TILE):
            lanes = slice(l0, l0 + CONV_LANE_TILE)
            carried, carried_for = {}, None
            for s, r0, flat in row_tiles(rb):
                if carried_for != (s, r0):
                    carried = {}
                tiles = [gbuf[s, r0 + SUBLANES * j:r0 + SUBLANES * (j + 1), lanes]
                         for j in range(n_tiles + max_a)]
                out = [None] * n_tiles
                for b in range(SUBLANES):
                    taps = [a for a in range(max_a + 1)
                            if 0 <= SUBLANES * a + b - hist0 < CONV_WIDTH]
                    wts = [wdw_ref[SUBLANES * a + b - hist0:SUBLANES * a + b - hist0 + 1, lanes]
                           for a in taps]
                    z = []
                    for m in range(n_tiles + (1 if b else 0)):
                        if m == 0 and b in carried:
                            z.append(carried[b])
                            continue
                        acc = None
                        for a, wt in zip(taps, wts):
                            term = tiles[m + a] * wt
                            acc = term if acc is None else acc + term
                        z.append(acc)
                    if b:
                        carried[b] = z[n_tiles]
                    for m in range(n_tiles):
                        if b == 0:
                            part = z[m]
                        else:
                            part = pltpu.roll(jnp.where(row >= b, z[m], z[m + 1]), SUBLANES - b, 0)
                        out[m] = part if out[m] is None else out[m] + part
                carried_for = (s, r0 + CONV_ROW_TILE)
                for m in range(n_tiles):
                    lo = flat + SUBLANES * m
                    ybuf[lo:lo + SUBLANES, lanes] = out[m]
        for s, r0, flat in row_tiles(rb):
            rs = slice(flat, flat + CONV_ROW_TILE)
            y = _layernorm(ybuf[rs, :] + bdw_ref[...], lng_ref[...], lnb_ref[...])
            cbuf[rs, :] = (y * jax.nn.sigmoid(y)).astype(BF16)

    def pointwise_out(rb):
        flat = rb * CONV_ROW_BLOCK
        y = _dot(cbuf[flat:flat + CONV_ROW_BLOCK, :], w2_ref[...]) + b2_ref[...]
        x = block_rows(x_ref, rb)[...]
        block_rows(o_ref, rb)[...] = x + _rms(y, gout_ref[...]).reshape(x.shape)

    n_blocks = rows // CONV_ROW_BLOCK
    pointwise_in(0)
    for rb in range(1, n_blocks):
        depthwise(rb - 1)
        pointwise_in(rb)
        pointwise_out(rb - 1)
    depthwise(n_blocks - 1)
    pointwise_out(n_blocks - 1)

    last = gbuf[:, CONV_PAD + tm - CONV_HIST:CONV_PAD + tm, :]

    @pl.when(t == nt - 1)
    def _():
        nst_ref[...] = last

    if nt > 1:
        gbuf[:, hist0:CONV_PAD, :] = last


def _conv_layer(x, st, gin, gout, w1, b1, wdw, bdw, lng, lnb, w2, b2, *, bb, tm, ffn_cast=None):
    B, T, _ = x.shape
    nt = T // tm
    assert tm >= CONV_HIST and tm % CONV_ROW_TILE == 0
    body = functools.partial(_conv_body, bb=bb, tm=tm, nt=nt)
    return _mixer_call(
        body,
        grid=(B // bb, nt),
        out_shape=(jax.ShapeDtypeStruct(x.shape, F32),
                   jax.ShapeDtypeStruct((B, CONV_HIST, D_MODEL), F32)),
        in_specs=[
            _seq_spec(bb, tm, D_MODEL),
            _state_spec(bb, CONV_HIST, D_MODEL),
            _const_spec((1, D_MODEL)),
            _const_spec((1, D_MODEL)),
            _const_spec(w1.shape),
            _const_spec(b1.shape),
            _const_spec(wdw.shape),
            _const_spec(bdw.shape),
            _const_spec(lng.shape),
            _const_spec(lnb.shape),
            _const_spec(w2.shape),
            _const_spec(b2.shape),
        ],
        out_specs=(_seq_spec(bb, tm, D_MODEL), _state_spec(bb, CONV_HIST, D_MODEL)),
        scratch_shapes=[
            pltpu.VMEM((bb, CONV_PAD + tm, D_MODEL), F32),
            pltpu.VMEM((bb * tm, D_MODEL), F32),
            pltpu.VMEM((bb * tm, D_MODEL), BF16),
        ],
        name="conv_mixer",
        ffn_cast=ffn_cast,
    )(x, st, gin, gout, w1, b1, wdw, bdw, lng, lnb, w2, b2)


def _cmlp_body(x_ref, gin_ref, gout_ref, win_ref, bin_ref, lng_ref, lnb_ref, ws_ref, bs_ref,
               wout_ref, bout_ref, *rest, bb, tm, blk, emit_v):
    if emit_v:
        o_ref, v_ref, vbuf, ybuf = rest
    else:
        o_ref, vbuf, ybuf = rest
    rows = bb * tm
    x = x_ref[...].reshape(rows, D_MODEL)
    h = _rms(x, gin_ref[...]).astype(BF16)
    hv = _gelu(_dot(h, win_ref[:, CMLP_DIM:]) + bin_ref[:, CMLP_DIM:])
    vln = _layernorm(hv, lng_ref[...], lnb_ref[...])
    if emit_v:
        v_ref[...] = vln.reshape(bb, tm, CMLP_DIM)
    vbuf[...] = vln.astype(BF16)

    pos_r = lax.broadcasted_iota(jnp.int32, (blk, blk), 0) // CHUNK
    pos_c = lax.broadcasted_iota(jnp.int32, (blk, blk), 1) // CHUNK
    for g in range(CMLP_GROUPS):
        lanes = slice(g * CMLP_GROUP_DIM, (g + 1) * CMLP_GROUP_DIM)
        ug = _gelu(_dot(h, win_ref[:, lanes]) + bin_ref[:, lanes])
        wsm = jnp.where(pos_c <= pos_r, ws_ref[g], 0.0).astype(BF16)
        for n in range(rows // blk):
            rs = slice(n * blk, (n + 1) * blk)
            gate = _dot(wsm, vbuf[rs, lanes]) + bs_ref[g]
            ybuf[rs, lanes] = (ug[rs] * gate).astype(BF16)

    y = _dot(ybuf[...], wout_ref[...]) + bout_ref[...]
    o_ref[...] = (x + _rms(y, gout_ref[...])).reshape(bb, tm, D_MODEL)


def _cmlp_layer(x, gin, gout, win, bin_, lng, lnb, ws, bs, wout, bout, *, bb, tm, emit_v, ffn_cast=None):
    B, T, _ = x.shape
    blk = min(T, MIX_BLOCK)
    assert tm % blk == 0
    ws = ws[:, :blk, :blk]
    bs = bs[:, :blk, None]
    body = functools.partial(_cmlp_body, bb=bb, tm=tm, blk=blk, emit_v=emit_v)
    out_shape = [jax.ShapeDtypeStruct(x.shape, F32)]
    out_specs = [_seq_spec(bb, tm, D_MODEL)]
    if emit_v:
        out_shape.append(jax.ShapeDtypeStruct((B, T, CMLP_DIM), F32))
        out_specs.append(_seq_spec(bb, tm, CMLP_DIM))
    return _mixer_call(
        body,
        grid=(B // bb, T // tm),
        out_shape=tuple(out_shape),
        in_specs=[
            _seq_spec(bb, tm, D_MODEL),
            _const_spec((1, D_MODEL)),
            _const_spec((1, D_MODEL)),
            _const_spec(win.shape),
            _const_spec(bin_.shape),
            _const_spec(lng.shape),
            _const_spec(lnb.shape),
            _const_spec(ws.shape),
            _const_spec(bs.shape),
            _const_spec(wout.shape),
            _const_spec(bout.shape),
        ],
        out_specs=tuple(out_specs),
        scratch_shapes=[
            pltpu.VMEM((bb * tm, CMLP_DIM), BF16),
            pltpu.VMEM((bb * tm, CMLP_DIM), BF16),
        ],
        name="cmlp_mixer",
        ffn_cast=ffn_cast,
    )(x, gin, gout, win, bin_, lng, lnb, ws, bs, wout, bout)


def _ffn_body(x_ref, st_ref, gin_ref, gout_ref, wup_ref, wdw_ref, bdw_ref, wdn_ref,
              o_ref, nst_ref, actbuf, *upbufs, bb, tm, nt):
    n_chunks = D_FF // FFN_CHUNK
    t = pl.program_id(1)
    rows = bb * tm
    hist0 = FFN_PAD - FFN_HIST
    n_tiles = tm // SUBLANES
    tile_row = lax.broadcasted_iota(jnp.int32, (bb * n_tiles, SUBLANES, FFN_CHUNK), 1)

    def halves(c):
        return ((c * FFN_CHUNK, slice(0, FFN_CHUNK)),
                (D_FF + c * FFN_CHUNK, slice(FFN_CHUNK, 2 * FFN_CHUNK)))

    @pl.when(t == 0)
    def _():
        for c in range(n_chunks):
            upbufs[c][:, 0:hist0, :] = jnp.zeros((bb, hist0, 2 * FFN_CHUNK), F32)
            for lo, lanes in halves(c):
                upbufs[c][:, hist0:FFN_PAD, lanes] = st_ref[:, :, lo:lo + FFN_CHUNK]

    def up_proj(h, c):
        for lo, lanes in halves(c):
            up = _dot(h, wup_ref[:, lo:lo + FFN_CHUNK])
            upbufs[c][:, FFN_PAD:FFN_PAD + tm, lanes] = up.reshape(bb, tm, FFN_CHUNK)

    def delayed(cur, hist):
        cur_r = pltpu.roll(cur, 1, 1)
        hist_r = pltpu.roll(hist, 1, 1)
        cur_r4 = cur_r.reshape(bb, n_tiles, SUBLANES, FFN_CHUNK)
        prev_r = jnp.concatenate([hist_r[:, None], cur_r4[:, :n_tiles - 1]], axis=1)
        out = jnp.where(tile_row < 1, prev_r.reshape(bb * n_tiles, SUBLANES, FFN_CHUNK), cur_r)
        return out, hist_r

    def conv(c, lo, lanes):
        w = wdw_ref[:, lo:lo + FFN_CHUNK]
        cur = upbufs[c][:, FFN_PAD:FFN_PAD + tm, lanes].reshape(bb * n_tiles, SUBLANES, FFN_CHUNK)
        hist = upbufs[c][:, 0:FFN_PAD, lanes]
        acc, acc_hist = None, None
        for kk in range(FFN_HIST):
            term, term_hist = cur * w[kk:kk + 1], hist * w[kk:kk + 1]
            if acc is not None:
                term, term_hist = term + acc, term_hist + acc_hist
            acc, acc_hist = delayed(term, term_hist)
        out = cur * w[FFN_HIST:FFN_HIST + 1] + bdw_ref[:, lo:lo + FFN_CHUNK] + acc
        return out.reshape(rows, FFN_CHUNK)

    x = x_ref[...].reshape(rows, D_MODEL)
    h = _rms(x, gin_ref[...]).astype(BF16)
    for c in range(n_chunks):
        up_proj(h, c)
        (glo, glanes), (ulo, ulanes) = halves(c)
        act = _gelu(conv(c, glo, glanes)) * conv(c, ulo, ulanes)
        actbuf[:, c * FFN_CHUNK:(c + 1) * FFN_CHUNK] = act.astype(BF16)

    part_rows = rows // FFN_DOWN_ROW_PARTS
    for rp in range(FFN_DOWN_ROW_PARTS):
        rs = slice(rp * part_rows, (rp + 1) * part_rows)
        out = x[rs] + _rms(_dot(actbuf[rs, :], wdn_ref[...]), gout_ref[...])
        if bb == 1:
            o_ref[0, rs, :] = out
        else:
            nb = part_rows // tm
            o_ref[rp * nb:(rp + 1) * nb] = out.reshape(nb, tm, D_MODEL)

    lasts = [(lo, lanes, upbufs[c][:, FFN_PAD + tm - FFN_HIST:FFN_PAD + tm, lanes])
             for c in range(n_chunks) for lo, lanes in halves(c)]
    if nt > 1:
        for i, (lo, lanes, last) in enumerate(lasts):
            upbufs[i // 2][:, hist0:FFN_PAD, lanes] = last

    @pl.when(t == nt - 1)
    def _():
        for lo, lanes, last in lasts:
            nst_ref[:, :, lo:lo + FFN_CHUNK] = last


def _ffn_layer(x, st, gin, gout, wup, wdw, bdw, wdn, *, bb, tm):
    B, T, _ = x.shape
    nt = T // tm
    assert tm >= FFN_HIST and D_FF % FFN_CHUNK == 0
    body = functools.partial(_ffn_body, bb=bb, tm=tm, nt=nt)
    return pl.pallas_call(
        body,
        grid=(B // bb, nt),
        out_shape=(jax.ShapeDtypeStruct(x.shape, F32),
                   jax.ShapeDtypeStruct((B, FFN_HIST, 2 * D_FF), F32)),
        in_specs=[
            _seq_spec(bb, tm, D_MODEL),
            _state_spec(bb, FFN_HIST, 2 * D_FF),
            _const_spec((1, D_MODEL)),
            _const_spec((1, D_MODEL)),
            _const_spec(wup.shape),
            _const_spec(wdw.shape),
            _const_spec(bdw.shape),
            _const_spec(wdn.shape),
        ],
        out_specs=(_seq_spec(bb, tm, D_MODEL), _state_spec(bb, FFN_HIST, 2 * D_FF)),
        scratch_shapes=(
            [pltpu.VMEM((bb * tm, D_FF), BF16)]
            + [pltpu.VMEM((bb, FFN_PAD + tm, 2 * FFN_CHUNK), F32)] * (D_FF // FFN_CHUNK)),
        compiler_params=_params(),
        name="conv_ffn",
    )(x, st, gin, gout, wup, wdw, bdw, wdn)


def _trunk(x, caches, p, bias, ffn_bf16, *, bb, tm):
    prompt = caches is None
    cast_here = ffn_bf16 is None
    ffn_bf16 = [] if cast_here else ffn_bf16
    B = x.shape[0]
    new_k, new_v, new_conv, new_cv, new_ffn = [], [], [], [], []
    for i in range(DEPTH):
        kind, j = i % N_MIXERS, i // N_MIXERS
        g = p['norm_gain'][i]
        gains = [g[n][None, :] for n in range(4)]
        ffn_cast = (p['ffn_w_up'], p['ffn_w_down'], i) if cast_here else None
        if kind == 0:
            if prompt:
                kc = jnp.zeros((B, WINDOW, KV_DIM), F32)
                vc = kc
            else:
                kc = caches[0][j].reshape(B, WINDOW, KV_DIM)
                vc = caches[1][j].reshape(B, WINDOW, KV_DIM)
            x, k_rows, v_rows, *cast = _attn_layer(
                x, kc, vc, gains[0], gains[1], p['attn_w_qkv'], p['attn_b_qkv'][j][None, :],
                p['attn_w_o'], p['attn_b_o'][j][None, :], p['attn_sinks'][j], bias, layer=j,
                bb=bb, tm=tm, mask_start=prompt, ffn_cast=ffn_cast)
            new_k.append(k_rows.reshape(B, -1, N_KV_HEADS, HEAD_DIM))
            new_v.append(v_rows.reshape(B, -1, N_KV_HEADS, HEAD_DIM))
        elif kind == 1:
            st = jnp.zeros((B, CONV_HIST, D_MODEL), F32) if prompt else caches[2][j]
            x, s, *cast = _conv_layer(
                x, st, gains[0], gains[1], p['conv_w_pw1'][j], p['conv_b_pw1'][j][None, :],
                p['conv_w_dw'][j], p['conv_b_dw'][j][None, :], p['conv_ln_g'][j][None, :],
                p['conv_ln_b'][j][None, :], p['conv_w_pw2'][j], p['conv_b_pw2'][j][None, :],
                bb=bb, tm=tm, ffn_cast=ffn_cast)
            new_conv.append(s)
        else:
            res = _cmlp_layer(
                x, gains[0], gains[1], p['cmlp_w_in'][j], p['cmlp_b_in'][j][None, :],
                p['cmlp_ln_g'][j][None, :], p['cmlp_ln_b'][j][None, :], p['cmlp_w_s'][j],
                p['cmlp_b_s'][j], p['cmlp_w_out'][j], p['cmlp_b_out'][j][None, :],
                bb=bb, tm=tm, emit_v=not prompt, ffn_cast=ffn_cast)
            x = res[0]
            if not prompt:
                new_cv.append(res[1])
            cast = res[-2:] if cast_here else []
        st = jnp.zeros((B, FFN_HIST, 2 * D_FF), F32) if prompt else caches[3][i]
        if cast_here:
            ffn_bf16.append(tuple(cast))
        wup, wdn = ffn_bf16[i]
        x, s = _ffn_layer(x, st, gains[2], gains[3], wup, p['ffn_w_dw'][i],
                          p['ffn_b_dw'][i][None, :], wdn, bb=bb, tm=tm)
        new_ffn.append(s)
    cv = jnp.stack(new_cv) if new_cv else None
    return (x, jnp.stack(new_k), jnp.stack(new_v), jnp.stack(new_conv), cv, jnp.stack(new_ffn)), ffn_bf16


def kernel(x_prompt, x_sample, cache_attn_k, cache_attn_v, state_conv, state_ffn_conv, rel_bias_table, norm_gain, attn_w_qkv, attn_b_qkv, attn_w_o, attn_b_o, attn_sinks, conv_w_pw1, conv_b_pw1, conv_w_dw, conv_b_dw, conv_ln_g, conv_ln_b, conv_w_pw2, conv_b_pw2, cmlp_w_in, cmlp_b_in, cmlp_ln_g, cmlp_ln_b, cmlp_w_s, cmlp_b_s, cmlp_w_out, cmlp_b_out, ffn_w_up, ffn_w_dw, ffn_b_dw, ffn_w_down):
    p = {
        'norm_gain': norm_gain,
        'attn_w_qkv': attn_w_qkv.astype(BF16), 'attn_b_qkv': attn_b_qkv,
        'attn_w_o': attn_w_o.astype(BF16), 'attn_b_o': attn_b_o, 'attn_sinks': attn_sinks,
        'conv_w_pw1': conv_w_pw1.astype(BF16), 'conv_b_pw1': conv_b_pw1, 'conv_w_dw': conv_w_dw,
        'conv_b_dw': conv_b_dw, 'conv_ln_g': conv_ln_g, 'conv_ln_b': conv_ln_b,
        'conv_w_pw2': conv_w_pw2.astype(BF16), 'conv_b_pw2': conv_b_pw2,
        'cmlp_w_in': cmlp_w_in.astype(BF16), 'cmlp_b_in': cmlp_b_in, 'cmlp_ln_g': cmlp_ln_g,
        'cmlp_ln_b': cmlp_ln_b, 'cmlp_w_s': cmlp_w_s, 'cmlp_b_s': cmlp_b_s,
        'cmlp_w_out': cmlp_w_out.astype(BF16), 'cmlp_b_out': cmlp_b_out,
        'ffn_w_up': ffn_w_up, 'ffn_w_dw': ffn_w_dw, 'ffn_b_dw': ffn_b_dw, 'ffn_w_down': ffn_w_down,
    }
    bias = _rel_bias(rel_bias_table)
    (y_prompt, p_attn_k, p_attn_v, p_conv, _, p_ffn_conv), ffn_bf16 = _trunk(
        x_prompt, None, p, bias, None, bb=1, tm=512)
    (y_sample, s_attn_k, s_attn_v, s_conv, s_cmlp_v, s_ffn_conv), _ = _trunk(
        x_sample, (cache_attn_k, cache_attn_v, state_conv, state_ffn_conv), p, bias, ffn_bf16, bb=8, tm=64)
    return (y_prompt, y_sample, p_attn_k, p_attn_v, p_conv, p_ffn_conv,
            s_attn_k, s_attn_v, s_conv, s_cmlp_v, s_ffn_conv)
```

```python
import functools
import math

import jax
import jax.numpy as jnp
from jax import lax
from jax.experimental import pallas as pl
from jax.experimental.pallas import tpu as pltpu

D_MODEL = 1024
DEPTH = 4
CHUNK = 64
N_MIXERS = 3
N_HEADS = 16
N_KV_HEADS = 4
HEAD_DIM = 64
GQA_GROUP = N_HEADS // N_KV_HEADS
WINDOW = 128
KV_DIM = N_KV_HEADS * HEAD_DIM
Q_DIM = N_HEADS * HEAD_DIM
NUM_BUCKETS = 32
MAX_DISTANCE = 128
CONV_WIDTH = 31
CONV_HIST = CONV_WIDTH - 1
MIX_BLOCK = 128
CMLP_GROUPS = 4
CMLP_DIM = 2 * D_MODEL
CMLP_GROUP_DIM = CMLP_DIM // CMLP_GROUPS
D_FF = 2816
FFN_CONV_WIDTH = 3
FFN_HIST = FFN_CONV_WIDTH - 1
RMS_EPS = 1e-6
_GELU_C0 = math.sqrt(2.0 / math.pi)
_GELU_C1 = _GELU_C0 * 0.044715
LN_EPS = 1e-5

F32 = jnp.float32
BF16 = jnp.bfloat16

V7X_VMEM_BYTES = 64 * 1024 * 1024
VMEM_LIMIT_BYTES = V7X_VMEM_BYTES - 8 * 1024 * 1024
SUBLANES = 8
BF16_SUBLANES = 2 * SUBLANES
FFN_CHUNK = 256
FFN_DOWN_ROW_PARTS = 4
FFN_PAD = SUBLANES
CONV_PAD = 32
CONV_ROW_TILE = 64
CONV_LANE_TILE = 128
CONV_ROW_BLOCK = 256


def _rms(x, g):
    ms = jnp.mean(x * x, axis=-1, keepdims=True)
    return x * lax.rsqrt(ms + RMS_EPS) * g


def _layernorm(x, g, b):
    mu = jnp.mean(x, axis=-1, keepdims=True)
    xc = x - mu
    var = jnp.mean(xc * xc, axis=-1, keepdims=True)
    return xc * lax.rsqrt(var + LN_EPS) * g + b


def _gelu(x):
    hx = 0.5 * x
    return hx + hx * jnp.tanh(x * (_GELU_C0 + _GELU_C1 * (x * x)))


def _dot(a, b):
    return jnp.dot(a, b, preferred_element_type=F32)


def _const_spec(shape):
    zeros = (0,) * len(shape)
    return pl.BlockSpec(shape, lambda b, t: zeros, pipeline_mode=pl.Buffered(1))


def _layer_spec(stacked, layer):
    zeros = (0,) * (stacked.ndim - 1)
    return pl.BlockSpec((None,) + stacked.shape[1:], lambda b, t: (layer,) + zeros,
                        pipeline_mode=pl.Buffered(1))


def _seq_spec(bb, rows, width):
    return pl.BlockSpec((bb, rows, width), lambda b, t: (b, t, 0))


def _state_spec(bb, rows, width):
    return pl.BlockSpec((bb, rows, width), lambda b, t: (b, 0, 0))


def _params(flags=None):
    return pltpu.CompilerParams(
        dimension_semantics=("arbitrary", "arbitrary"),
        vmem_limit_bytes=VMEM_LIMIT_BYTES,
        flags=flags,
    )


def _mixer_call(body, *, grid, out_shape, in_specs, out_specs, scratch_shapes, name, ffn_cast=None):
    out_shape, out_specs, in_specs = tuple(out_shape), tuple(out_specs), list(in_specs)
    extra_args = ()
    if ffn_cast is not None:
        wup, wdn, layer = ffn_cast
        nt, steps = grid[1], grid[0] * grid[1]
        up_rows, dn_rows = D_MODEL // steps, D_FF // (steps // 2)
        assert up_rows * steps == D_MODEL and dn_rows * (steps // 2) == D_FF
        assert up_rows % BF16_SUBLANES == 0 and dn_rows % BF16_SUBLANES == 0
        n_in, n_out = len(in_specs), len(out_specs)
        mixer_body = body

        def body(*refs):
            cast_in = refs[n_in:n_in + 2]
            cast_out = refs[n_in + 2 + n_out:n_in + 4 + n_out]
            for src, dst in zip(cast_in, cast_out):
                dst[...] = src[...].astype(BF16)
            mixer_body(*refs[:n_in], *refs[n_in + 2:n_in + 2 + n_out], *refs[n_in + 4 + n_out:])

        in_specs += [pl.BlockSpec((None, up_rows, 2 * D_FF), lambda b, t: (layer, b * nt + t, 0)),
                     pl.BlockSpec((None, dn_rows, D_MODEL), lambda b, t: (layer, (b * nt + t) // 2, 0))]
        out_specs += (pl.BlockSpec((up_rows, 2 * D_FF), lambda b, t: (b * nt + t, 0)),
                      pl.BlockSpec((dn_rows, D_MODEL), lambda b, t: ((b * nt + t) // 2, 0)))
        out_shape += (jax.ShapeDtypeStruct((D_MODEL, 2 * D_FF), BF16),
                      jax.ShapeDtypeStruct((D_FF, D_MODEL), BF16))
        extra_args = (wup, wdn)
    call = pl.pallas_call(body, grid=grid, out_shape=out_shape, in_specs=in_specs, out_specs=out_specs,
                          scratch_shapes=scratch_shapes, compiler_params=_params(), name=name)
    return lambda *args: call(*args, *extra_args)


HEAD_PAIR = 2 * HEAD_DIM
ATTN_LOOKAHEAD = 4
ATTN_CHUNKS_PER_UNIT = 2
ATTN_ROW_SPLIT = 2
PAIR_STEP = GQA_GROUP // 2


def _pair_heads(j, v):
    return GQA_GROUP * j + v, GQA_GROUP * j + PAIR_STEP + v


def _bias_body(bucket_ref, table_ref, o_ref):
    bk = bucket_ref[...]
    low = lax.broadcasted_iota(jnp.int32, (1, HEAD_PAIR), 1) < CHUNK
    for j in range(N_KV_HEADS):
        for v in range(2):
            ha, hb = _pair_heads(j, v)
            acc = jnp.zeros(bk.shape, F32)
            for b in range(NUM_BUCKETS):
                val = jnp.where(low, table_ref[b, ha], table_ref[b, hb])
                acc = jnp.where(bk == b, val, acc)
            o_ref[j, v] = acc


def _t5_bucket(rel):
    half = NUM_BUCKETS // 2
    max_exact = half // 2
    n = jnp.abs(rel)
    log_ratio = jnp.log(jnp.maximum(n, 1).astype(F32) / max_exact) / math.log(MAX_DISTANCE / max_exact)
    large = jnp.minimum(max_exact + (log_ratio * (half - max_exact)).astype(jnp.int32), half - 1)
    return jnp.where(rel > 0, half, 0) + jnp.where(n < max_exact, n, large)


def _rel_bias(table):
    q_pos = jnp.arange(CHUNK, dtype=jnp.int32)
    k_pos = jnp.arange(WINDOW + CHUNK, dtype=jnp.int32) - WINDOW
    bucket = _t5_bucket(k_pos[:, None] - q_pos[None, :]).astype(jnp.int32)
    bucket = jnp.concatenate([bucket, bucket], axis=1)
    return pl.pallas_call(
        _bias_body,
        out_shape=jax.ShapeDtypeStruct((N_KV_HEADS, 2, WINDOW + CHUNK, HEAD_PAIR), F32),
        in_specs=[pl.BlockSpec(memory_space=pltpu.VMEM), pl.BlockSpec(memory_space=pltpu.SMEM)],
        out_specs=pl.BlockSpec(memory_space=pltpu.VMEM),
        name="rel_bias",
    )(bucket, table)


def _attn_body(x_ref, kc_ref, vc_ref, gin_ref, gout_ref, wqkv_ref, bqkv_ref, wo_ref, bo_ref,
               sinks_ref, bias_ref, o_ref, kout_ref, vout_ref, qbuf, kzbuf, vzbuf, obuf, ybuf,
               *, bb, tm, nt, mask_start):
    t = pl.program_id(1)
    rows = bb * tm
    kv_rows = min(WINDOW, tm)
    span = WINDOW + CHUNK
    low = lax.broadcasted_iota(jnp.int32, (1, HEAD_PAIR), 1) < CHUNK

    def expand(buf, sl, r0, kv):
        nb, n = kv.shape[0], kv.shape[1]
        for pair in range(N_KV_HEADS // 2):
            tile = kv[:, :, pair * HEAD_PAIR:(pair + 1) * HEAD_PAIR]
            swapped = pltpu.roll(tile.reshape(nb * n, HEAD_PAIR), HEAD_DIM, 1).reshape(nb, n, HEAD_PAIR)
            for j, (lo_src, hi_src) in ((2 * pair, (tile, swapped)), (2 * pair + 1, (swapped, tile))):
                buf[sl, j, 0, r0:r0 + n, :] = jnp.where(low, lo_src, 0.0).astype(BF16)
                buf[sl, j, 1, r0:r0 + n, :] = jnp.where(low, 0.0, hi_src).astype(BF16)

    @pl.when(t == 0)
    def _():
        expand(kzbuf, slice(0, bb), 0, kc_ref[...])
        expand(vzbuf, slice(0, bb), 0, vc_ref[...])

    if bb >= ATTN_ROW_SPLIT:
        sub_b, sub_t = bb // ATTN_ROW_SPLIT, tm
        subs = [(slice(i * sub_b, (i + 1) * sub_b), 0) for i in range(ATTN_ROW_SPLIT)]
    else:
        sub_b, sub_t = bb, tm // ATTN_ROW_SPLIT
        subs = [(slice(0, bb), i * sub_t) for i in range(ATTN_ROW_SPLIT)]
    sub_rows = sub_b * sub_t
    assert sub_t % CHUNK == 0 and (sub_b > 1 or sub_t >= kv_rows or ATTN_ROW_SPLIT == 1)
    n_piece = GQA_GROUP * HEAD_DIM

    def flat0(sub):
        sl, r0 = sub
        return sl.start * tm + r0

    def load_x(sub):
        sl, r0 = sub
        return x_ref[sl, r0:r0 + sub_t, :].reshape(sub_rows, D_MODEL)

    def in_pieces(sub):
        sl, r0 = sub
        h = _rms(load_x(sub), gin_ref[...]).astype(BF16)
        f0 = flat0(sub)

        def kv_piece(col, buf, out_ref):
            def run():
                kv = (_dot(h, wqkv_ref[:, col:col + KV_DIM]) + bqkv_ref[:, col:col + KV_DIM]).reshape(
                    sub_b, sub_t, KV_DIM)
                expand(buf, sl, WINDOW + r0, kv)
                if sub_b > 1 or r0 + sub_t == tm:
                    out_ref[sl] = kv[:, sub_t - kv_rows:, :]
            return run

        def q_piece(j):
            def run():
                col = j * n_piece
                q = _dot(h, wqkv_ref[:, col:col + n_piece]) + bqkv_ref[:, col:col + n_piece]
                qbuf[f0:f0 + sub_rows, col:col + n_piece] = (q * (HEAD_DIM ** -0.5)).astype(BF16)
            return run

        return ([kv_piece(Q_DIM, kzbuf, kout_ref), kv_piece(Q_DIM + KV_DIM, vzbuf, vout_ref)]
                + [q_piece(j) for j in range(N_KV_HEADS)])

    def out_pieces(sub):
        sl, r0 = sub
        f0 = flat0(sub)

        def piece(n):
            def run():
                col = n * n_piece
                ybuf[f0:f0 + sub_rows, col:col + n_piece] = (
                    _dot(obuf[f0:f0 + sub_rows, :], wo_ref[:, col:col + n_piece]) + bo_ref[:, col:col + n_piece])
            return run

        def finish():
            y = ybuf[f0:f0 + sub_rows, :]
            o_ref[sl, r0:r0 + sub_t, :] = (load_x(sub) + _rms(y, gout_ref[...])).reshape(sub_b, sub_t, D_MODEL)

        return [piece(n) for n in range(D_MODEL // n_piece)] + [finish]

    key_row = lax.broadcasted_iota(jnp.int32, (span, HEAD_PAIR), 0)
    first_valid = jnp.where(t == 0, WINDOW, 0)
    contract_last = (((1,), (1,)), ((), ()))
    contract_first = (((0,), (0,)), ((), ()))

    def scores(s, chunks, j):
        g0 = j * GQA_GROUP * HEAD_DIM
        r0 = chunks[0] * CHUNK
        n_keys = span + (len(chunks) - 1) * CHUNK
        q_parts = []
        for c in chunks:
            q_rows = slice(s * tm + c * CHUNK, s * tm + (c + 1) * CHUNK)
            q_parts += [qbuf[q_rows, g0:g0 + HEAD_PAIR], qbuf[q_rows, g0 + HEAD_PAIR:g0 + 2 * HEAD_PAIR]]
        keys = jnp.concatenate([kzbuf[s, j, v_idx, r0:r0 + n_keys, :] for v_idx in range(2)], axis=0)
        sc_all = lax.dot_general(keys, jnp.concatenate(q_parts, axis=0), contract_last,
                                 preferred_element_type=F32)
        out = []
        for n, c in enumerate(chunks):
            tiles = []
            for v_idx in range(2):
                k0 = v_idx * n_keys + n * CHUNK
                sc = sc_all[k0:k0 + span, n * HEAD_PAIR:(n + 1) * HEAD_PAIR] + bias_ref[j, v_idx]
                if mask_start and c * CHUNK < WINDOW:
                    sc = jnp.where(key_row + c * CHUNK >= first_valid, sc, -jnp.inf)
                tiles.append(sc)
            out.append((c, tiles))
        return out

    def attend(s, c, j, scs):
        r0 = c * CHUNK
        q_rows = slice(s * tm + r0, s * tm + r0 + CHUNK)
        g0 = j * GQA_GROUP * HEAD_DIM
        probs = []
        for v_idx, sc in enumerate(scs):
            ha, hb = _pair_heads(j, v_idx)
            sink = jnp.where(low, sinks_ref[ha], sinks_ref[hb])
            m = jnp.maximum(jnp.max(sc, axis=0, keepdims=True), sink)
            p = jnp.exp(sc - m)
            denom = jnp.sum(p, axis=0, keepdims=True) + jnp.exp(sink - m)
            probs.append((p * (1.0 / denom)).astype(BF16))
        p_kn = jnp.concatenate(probs, axis=0)
        v_kd = jnp.concatenate([vzbuf[s, j, 0, r0:r0 + span, :],
                                vzbuf[s, j, 1, r0:r0 + span, :]], axis=0)
        o = lax.dot_general(p_kn, v_kd, contract_first, preferred_element_type=F32)
        obuf[q_rows, g0:g0 + HEAD_PAIR] = o[:CHUNK].astype(BF16)
        obuf[q_rows, g0 + HEAD_PAIR:g0 + 2 * HEAD_PAIR] = o[CHUNK:].astype(BF16)

    def units_of(sub):
        sl, r0 = sub
        c0, c1 = r0 // CHUNK, (r0 + sub_t) // CHUNK
        step = ATTN_CHUNKS_PER_UNIT if (c1 - c0) % ATTN_CHUNKS_PER_UNIT == 0 else 1
        return [(s, tuple(range(c, c + step)), j) for s in range(sl.start, sl.stop)
                for c in range(c0, c1, step) for j in range(N_KV_HEADS)]

    def run_units(units, fillers):
        every = max(1, len(units) // max(1, len(fillers)))
        pending = []

        def retire():
            (s, _, j), per_chunk = pending.pop(0)
            for c, tiles in per_chunk:
                attend(s, c, j, tiles)

        for n, unit in enumerate(units):
            pending.append((unit, scores(*unit)))
            if len(pending) > ATTN_LOOKAHEAD:
                retire()
            if fillers and (n + 1) % every == 0:
                fillers.pop(0)()
        while pending:
            retire()
        for filler in fillers:
            filler()

    for piece in in_pieces(subs[0]):
        piece()
    for i, sub in enumerate(subs):
        fillers = []
        if i > 0:
            fillers += out_pieces(subs[i - 1])
        if i + 1 < len(subs):
            fillers += in_pieces(subs[i + 1])
        run_units(units_of(sub), fillers)
    for piece in out_pieces(subs[-1]):
        piece()

    if nt > 1:
        kzbuf[:, :, :, 0:WINDOW, :] = kzbuf[:, :, :, tm:tm + WINDOW, :]
        vzbuf[:, :, :, 0:WINDOW, :] = vzbuf[:, :, :, tm:tm + WINDOW, :]


def _attn_layer(x, kc, vc, gin, gout, wqkv, bqkv, wo, bo, sinks, bias, *, layer, bb, tm, mask_start,
                ffn_cast=None):
    B, T, _ = x.shape
    nt = T // tm
    kv_rows = min(WINDOW, tm)
    body = functools.partial(_attn_body, bb=bb, tm=tm, nt=nt, mask_start=mask_start)
    return _mixer_call(
        body,
        grid=(B // bb, nt),
        out_shape=(jax.ShapeDtypeStruct(x.shape, F32),
                   jax.ShapeDtypeStruct((B, kv_rows, KV_DIM), F32),
                   jax.ShapeDtypeStruct((B, kv_rows, KV_DIM), F32)),
        in_specs=[
            _seq_spec(bb, tm, D_MODEL),
            _state_spec(bb, WINDOW, KV_DIM),
            _state_spec(bb, WINDOW, KV_DIM),
            _const_spec((1, D_MODEL)),
            _const_spec((1, D_MODEL)),
            _layer_spec(wqkv, layer),
            _const_spec(bqkv.shape),
            _layer_spec(wo, layer),
            _const_spec(bo.shape),
            pl.BlockSpec(memory_space=pltpu.SMEM),
            _const_spec(bias.shape),
        ],
        out_specs=(_seq_spec(bb, tm, D_MODEL),
                   _state_spec(bb, kv_rows, KV_DIM),
                   _state_spec(bb, kv_rows, KV_DIM)),
        scratch_shapes=[
            pltpu.VMEM((bb * tm, Q_DIM), BF16),
            pltpu.VMEM((bb, N_KV_HEADS, 2, WINDOW + tm, HEAD_PAIR), BF16),
            pltpu.VMEM((bb, N_KV_HEADS, 2, WINDOW + tm, HEAD_PAIR), BF16),
            pltpu.VMEM((bb * tm, Q_DIM), BF16),
            pltpu.VMEM((bb * tm, D_MODEL), F32),
        ],
        name="attn_mixer",
        ffn_cast=ffn_cast,
    )(x, kc, vc, gin, gout, wqkv, bqkv, wo, bo, sinks, bias)


def _conv_body(x_ref, st_ref, gin_ref, gout_ref, w1_ref, b1_ref, wdw_ref, bdw_ref, lng_ref, lnb_ref,
               w2_ref, b2_ref, o_ref, nst_ref, gbuf, ybuf, cbuf, *, bb, tm, nt):
    t = pl.program_id(1)
    rows = bb * tm
    hist0 = CONV_PAD - CONV_HIST

    @pl.when(t == 0)
    def _():
        gbuf[:, hist0:CONV_PAD, :] = st_ref[...]

    row = lax.broadcasted_iota(jnp.int32, (SUBLANES, CONV_LANE_TILE), 0)
    n_tiles = CONV_ROW_TILE // SUBLANES
    max_a = (hist0 + CONV_WIDTH - 1) // SUBLANES
    tiles_per_block = CONV_ROW_BLOCK // CONV_ROW_TILE

    def row_tiles(rb):
        out = []
        for i in range(tiles_per_block):
            flat = rb * CONV_ROW_BLOCK + i * CONV_ROW_TILE
            out.append((flat // tm, flat % tm, flat))
        return out

    def block_rows(ref, rb):
        flat = rb * CONV_ROW_BLOCK
        if tm >= CONV_ROW_BLOCK:
            return ref.at[flat // tm, flat % tm:flat % tm + CONV_ROW_BLOCK, :]
        return ref.at[flat // tm:(flat + CONV_ROW_BLOCK) // tm, :, :]

    def pointwise_in(rb):
        x = block_rows(x_ref, rb)[...].reshape(CONV_ROW_BLOCK, D_MODEL)
        h = _rms(x, gin_ref[...]).astype(BF16)
        ag = _dot(h, w1_ref[...]) + b1_ref[...]
        glu = ag[:, :D_MODEL] * jax.nn.sigmoid(ag[:, D_MODEL:])
        for i, (s, r0, _) in enumerate(row_tiles(rb)):
            gbuf[s, CONV_PAD + r0:CONV_PAD + r0 + CONV_ROW_TILE, :] = (
                glu[i * CONV_ROW_TILE:(i + 1) * CONV_ROW_TILE])

    def depthwise(rb):
        for l0 in range(0, D_MODEL, CONV_LANE_TILE):
            lanes = slice(l0, l0 + CONV_LANE_TILE)
            carried, carried_for = {}, None
            for s, r0, flat in row_tiles(rb):
                if carried_for != (s, r0):
                    carried = {}
                tiles = [gbuf[s, r0 + SUBLANES * j:r0 + SUBLANES * (j + 1), lanes]
                         for j in range(n_tiles + max_a)]
                out = [None] * n_tiles
                for b in range(SUBLANES):
                    taps = [a for a in range(max_a + 1)
                            if 0 <= SUBLANES * a + b - hist0 < CONV_WIDTH]
                    wts = [wdw_ref[SUBLANES * a + b - hist0:SUBLANES * a + b - hist0 + 1, lanes]
                           for a in taps]
                    z = []
                    for m in range(n_tiles + (1 if b else 0)):
                        if m == 0 and b in carried:
                            z.append(carried[b])
                            continue
                        acc = None
                        for a, wt in zip(taps, wts):
                            term = tiles[m + a] * wt
                            acc = term if acc is None else acc + term
                        z.append(acc)
                    if b:
                        carried[b] = z[n_tiles]
                    for m in range(n_tiles):
                        if b == 0:
                            part = z[m]
                        else:
                            part = pltpu.roll(jnp.where(row >= b, z[m], z[m + 1]), SUBLANES - b, 0)
                        out[m] = part if out[m] is None else out[m] + part
                carried_for = (s, r0 + CONV_ROW_TILE)
                for m in range(n_tiles):
                    lo = flat + SUBLANES * m
                    ybuf[lo:lo + SUBLANES, lanes] = out[m]
        for s, r0, flat in row_tiles(rb):
            rs = slice(flat, flat + CONV_ROW_TILE)
            y = _layernorm(ybuf[rs, :] + bdw_ref[...], lng_ref[...], lnb_ref[...])
            cbuf[rs, :] = (y * jax.nn.sigmoid(y)).astype(BF16)

    def pointwise_out(rb):
        flat = rb * CONV_ROW_BLOCK
        y = _dot(cbuf[flat:flat + CONV_ROW_BLOCK, :], w2_ref[...]) + b2_ref[...]
        x = block_rows(x_ref, rb)[...]
        block_rows(o_ref, rb)[...] = x + _rms(y, gout_ref[...]).reshape(x.shape)

    n_blocks = rows // CONV_ROW_BLOCK
    pointwise_in(0)
    for rb in range(1, n_blocks):
        depthwise(rb - 1)
        pointwise_in(rb)
        pointwise_out(rb - 1)
    depthwise(n_blocks - 1)
    pointwise_out(n_blocks - 1)

    last = gbuf[:, CONV_PAD + tm - CONV_HIST:CONV_PAD + tm, :]

    @pl.when(t == nt - 1)
    def _():
        nst_ref[...] = last

    if nt > 1:
        gbuf[:, hist0:CONV_PAD, :] = last


def _conv_layer(x, st, gin, gout, w1, b1, wdw, bdw, lng, lnb, w2, b2, *, bb, tm, ffn_cast=None):
    B, T, _ = x.shape
    nt = T // tm
    assert tm >= CONV_HIST and tm % CONV_ROW_TILE == 0
    body = functools.partial(_conv_body, bb=bb, tm=tm, nt=nt)
    return _mixer_call(
        body,
        grid=(B // bb, nt),
        out_shape=(jax.ShapeDtypeStruct(x.shape, F32),
                   jax.ShapeDtypeStruct((B, CONV_HIST, D_MODEL), F32)),
        in_specs=[
            _seq_spec(bb, tm, D_MODEL),
            _state_spec(bb, CONV_HIST, D_MODEL),
            _const_spec((1, D_MODEL)),
            _const_spec((1, D_MODEL)),
            _const_spec(w1.shape),
            _const_spec(b1.shape),
            _const_spec(wdw.shape),
            _const_spec(bdw.shape),
            _const_spec(lng.shape),
            _const_spec(lnb.shape),
            _const_spec(w2.shape),
            _const_spec(b2.shape),
        ],
        out_specs=(_seq_spec(bb, tm, D_MODEL), _state_spec(bb, CONV_HIST, D_MODEL)),
        scratch_shapes=[
            pltpu.VMEM((bb, CONV_PAD + tm, D_MODEL), F32),
            pltpu.VMEM((bb * tm, D_MODEL), F32),
            pltpu.VMEM((bb * tm, D_MODEL), BF16),
        ],
        name="conv_mixer",
        ffn_cast=ffn_cast,
    )(x, st, gin, gout, w1, b1, wdw, bdw, lng, lnb, w2, b2)


def _cmlp_body(x_ref, gin_ref, gout_ref, win_ref, bin_ref, lng_ref, lnb_ref, ws_ref, bs_ref,
               wout_ref, bout_ref, *rest, bb, tm, blk, emit_v):
    if emit_v:
        o_ref, v_ref, vbuf, ybuf = rest
    else:
        o_ref, vbuf, ybuf = rest
    rows = bb * tm
    x = x_ref[...].reshape(rows, D_MODEL)
    h = _rms(x, gin_ref[...]).astype(BF16)
    hv = _gelu(_dot(h, win_ref[:, CMLP_DIM:]) + bin_ref[:, CMLP_DIM:])
    vln = _layernorm(hv, lng_ref[...], lnb_ref[...])
    if emit_v:
        v_ref[...] = vln.reshape(bb, tm, CMLP_DIM)
    vbuf[...] = vln.astype(BF16)

    pos_r = lax.broadcasted_iota(jnp.int32, (blk, blk), 0) // CHUNK
    pos_c = lax.broadcasted_iota(jnp.int32, (blk, blk), 1) // CHUNK
    for g in range(CMLP_GROUPS):
        lanes = slice(g * CMLP_GROUP_DIM, (g + 1) * CMLP_GROUP_DIM)
        ug = _gelu(_dot(h, win_ref[:, lanes]) + bin_ref[:, lanes])
        wsm = jnp.where(pos_c <= pos_r, ws_ref[g], 0.0).astype(BF16)
        for n in range(rows // blk):
            rs = slice(n * blk, (n + 1) * blk)
            gate = _dot(wsm, vbuf[rs, lanes]) + bs_ref[g]
            ybuf[rs, lanes] = (ug[rs] * gate).astype(BF16)

    y = _dot(ybuf[...], wout_ref[...]) + bout_ref[...]
    o_ref[...] = (x + _rms(y, gout_ref[...])).reshape(bb, tm, D_MODEL)


def _cmlp_layer(x, gin, gout, win, bin_, lng, lnb, ws, bs, wout, bout, *, bb, tm, emit_v, ffn_cast=None):
    B, T, _ = x.shape
    blk = min(T, MIX_BLOCK)
    assert tm % blk == 0
    ws = ws[:, :blk, :blk]
    bs = bs[:, :blk, None]
    body = functools.partial(_cmlp_body, bb=bb, tm=tm, blk=blk, emit_v=emit_v)
    out_shape = [jax.ShapeDtypeStruct(x.shape, F32)]
    out_specs = [_seq_spec(bb, tm, D_MODEL)]
    if emit_v:
        out_shape.append(jax.ShapeDtypeStruct((B, T, CMLP_DIM), F32))
        out_specs.append(_seq_spec(bb, tm, CMLP_DIM))
    return _mixer_call(
        body,
        grid=(B // bb, T // tm),
        out_shape=tuple(out_shape),
        in_specs=[
            _seq_spec(bb, tm, D_MODEL),
            _const_spec((1, D_MODEL)),
            _const_spec((1, D_MODEL)),
            _const_spec(win.shape),
            _const_spec(bin_.shape),
            _const_spec(lng.shape),
            _const_spec(lnb.shape),
            _const_spec(ws.shape),
            _const_spec(bs.shape),
            _const_spec(wout.shape),
            _const_spec(bout.shape),
        ],
        out_specs=tuple(out_specs),
        scratch_shapes=[
            pltpu.VMEM((bb * tm, CMLP_DIM), BF16),
            pltpu.VMEM((bb * tm, CMLP_DIM), BF16),
        ],
        name="cmlp_mixer",
        ffn_cast=ffn_cast,
    )(x, gin, gout, win, bin_, lng, lnb, ws, bs, wout, bout)


def _ffn_body(x_ref, st_ref, gin_ref, gout_ref, wup_ref, wdw_ref, bdw_ref, wdn_ref,
              o_ref, nst_ref, actbuf, *upbufs, bb, tm, nt):
    n_chunks = D_FF // FFN_CHUNK
    t = pl.program_id(1)
    rows = bb * tm
    hist0 = FFN_PAD - FFN_HIST
    n_tiles = tm // SUBLANES
    tile_row = lax.broadcasted_iota(jnp.int32, (bb * n_tiles, SUBLANES, FFN_CHUNK), 1)

    def halves(c):
        return ((c * FFN_CHUNK, slice(0, FFN_CHUNK)),
                (D_FF + c * FFN_CHUNK, slice(FFN_CHUNK, 2 * FFN_CHUNK)))

    @pl.when(t == 0)
    def _():
        for c in range(n_chunks):
            upbufs[c][:, 0:hist0, :] = jnp.zeros((bb, hist0, 2 * FFN_CHUNK), F32)
            for lo, lanes in halves(c):
                upbufs[c][:, hist0:FFN_PAD, lanes] = st_ref[:, :, lo:lo + FFN_CHUNK]

    def up_proj(h, c):
        for lo, lanes in halves(c):
            up = _dot(h, wup_ref[:, lo:lo + FFN_CHUNK])
            upbufs[c][:, FFN_PAD:FFN_PAD + tm, lanes] = up.reshape(bb, tm, FFN_CHUNK)

    def delayed(cur, hist):
        cur_r = pltpu.roll(cur, 1, 1)
        hist_r = pltpu.roll(hist, 1, 1)
        cur_r4 = cur_r.reshape(bb, n_tiles, SUBLANES, FFN_CHUNK)
        prev_r = jnp.concatenate([hist_r[:, None], cur_r4[:, :n_tiles - 1]], axis=1)
        out = jnp.where(tile_row < 1, prev_r.reshape(bb * n_tiles, SUBLANES, FFN_CHUNK), cur_r)
        return out, hist_r

    def conv(c, lo, lanes):
        w = wdw_ref[:, lo:lo + FFN_CHUNK]
        cur = upbufs[c][:, FFN_PAD:FFN_PAD + tm, lanes].reshape(bb * n_tiles, SUBLANES, FFN_CHUNK)
        hist = upbufs[c][:, 0:FFN_PAD, lanes]
        acc, acc_hist = None, None
        for kk in range(FFN_HIST):
            term, term_hist = cur * w[kk:kk + 1], hist * w[kk:kk + 1]
            if acc is not None:
                term, term_hist = term + acc, term_hist + acc_hist
            acc, acc_hist = delayed(term, term_hist)
        out = cur * w[FFN_HIST:FFN_HIST + 1] + bdw_ref[:, lo:lo + FFN_CHUNK] + acc
        return out.reshape(rows, FFN_CHUNK)

    x = x_ref[...].reshape(rows, D_MODEL)
    h = _rms(x, gin_ref[...]).astype(BF16)
    for c in range(n_chunks):
        up_proj(h, c)
        (glo, glanes), (ulo, ulanes) = halves(c)
        act = _gelu(conv(c, glo, glanes)) * conv(c, ulo, ulanes)
        actbuf[:, c * FFN_CHUNK:(c + 1) * FFN_CHUNK] = act.astype(BF16)

    part_rows = rows // FFN_DOWN_ROW_PARTS
    for rp in range(FFN_DOWN_ROW_PARTS):
        rs = slice(rp * part_rows, (rp + 1) * part_rows)
        out = x[rs] + _rms(_dot(actbuf[rs, :], wdn_ref[...]), gout_ref[...])
        if bb == 1:
            o_ref[0, rs, :] = out
        else:
            nb = part_rows // tm
            o_ref[rp * nb:(rp + 1) * nb] = out.reshape(nb, tm, D_MODEL)

    lasts = [(lo, lanes, upbufs[c][:, FFN_PAD + tm - FFN_HIST:FFN_PAD + tm, lanes])
             for c in range(n_chunks) for lo, lanes in halves(c)]
    if nt > 1:
        for i, (lo, lanes, last) in enumerate(lasts):
            upbufs[i // 2][:, hist0:FFN_PAD, lanes] = last

    @pl.when(t == nt - 1)
    def _():
        for lo, lanes, last in lasts:
            nst_ref[:, :, lo:lo + FFN_CHUNK] = last


def _ffn_layer(x, st, gin, gout, wup, wdw, bdw, wdn, *, bb, tm):
    B, T, _ = x.shape
    nt = T // tm
    assert tm >= FFN_HIST and D_FF % FFN_CHUNK == 0
    body = functools.partial(_ffn_body, bb=bb, tm=tm, nt=nt)
    return pl.pallas_call(
        body,
        grid=(B // bb, nt),
        out_shape=(jax.ShapeDtypeStruct(x.shape, F32),
                   jax.ShapeDtypeStruct((B, FFN_HIST, 2 * D_FF), F32)),
        in_specs=[
            _seq_spec(bb, tm, D_MODEL),
            _state_spec(bb, FFN_HIST, 2 * D_FF),
            _const_spec((1, D_MODEL)),
            _const_spec((1, D_MODEL)),
            _const_spec(wup.shape),
            _const_spec(wdw.shape),
            _const_spec(bdw.shape),
            _const_spec(wdn.shape),
        ],
        out_specs=(_seq_spec(bb, tm, D_MODEL), _state_spec(bb, FFN_HIST, 2 * D_FF)),
        scratch_shapes=(
            [pltpu.VMEM((bb * tm, D_FF), BF16)]
            + [pltpu.VMEM((bb, FFN_PAD + tm, 2 * FFN_CHUNK), F32)] * (D_FF // FFN_CHUNK)),
        compiler_params=_params(),
        name="conv_ffn",
    )(x, st, gin, gout, wup, wdw, bdw, wdn)


def _trunk(x, caches, p, bias, ffn_bf16, *, bb, tm):
    prompt = caches is None
    cast_here = ffn_bf16 is None
    ffn_bf16 = [] if cast_here else ffn_bf16
    B = x.shape[0]
    new_k, new_v, new_conv, new_cv, new_ffn = [], [], [], [], []
    for i in range(DEPTH):
        kind, j = i % N_MIXERS, i // N_MIXERS
        g = p['norm_gain'][i]
        gains = [g[n][None, :] for n in range(4)]
        ffn_cast = (p['ffn_w_up'], p['ffn_w_down'], i) if cast_here else None
        if kind == 0:
            if prompt:
                kc = jnp.zeros((B, WINDOW, KV_DIM), F32)
                vc = kc
            else:
                kc = caches[0][j].reshape(B, WINDOW, KV_DIM)
                vc = caches[1][j].reshape(B, WINDOW, KV_DIM)
            x, k_rows, v_rows, *cast = _attn_layer(
                x, kc, vc, gains[0], gains[1], p['attn_w_qkv'], p['attn_b_qkv'][j][None, :],
                p['attn_w_o'], p['attn_b_o'][j][None, :], p['attn_sinks'][j], bias, layer=j,
                bb=bb, tm=tm, mask_start=prompt, ffn_cast=ffn_cast)
            new_k.append(k_rows.reshape(B, -1, N_KV_HEADS, HEAD_DIM))
            new_v.append(v_rows.reshape(B, -1, N_KV_HEADS, HEAD_DIM))
        elif kind == 1:
            st = jnp.zeros((B, CONV_HIST, D_MODEL), F32) if prompt else caches[2][j]
            x, s, *cast = _conv_layer(
                x, st, gains[0], gains[1], p['conv_w_pw1'][j], p['conv_b_pw1'][j][None, :],
                p['conv_w_dw'][j], p['conv_b_dw'][j][None, :], p['conv_ln_g'][j][None, :],
                p['conv_ln_b'][j][None, :], p['conv_w_pw2'][j], p['conv_b_pw2'][j][None, :],
                bb=bb, tm=tm, ffn_cast=ffn_cast)
            new_conv.append(s)
        else:
            res = _cmlp_layer(
                x, gains[0], gains[1], p['cmlp_w_in'][j], p['cmlp_b_in'][j][None, :],
                p['cmlp_ln_g'][j][None, :], p['cmlp_ln_b'][j][None, :], p['cmlp_w_s'][j],
                p['cmlp_b_s'][j], p['cmlp_w_out'][j], p['cmlp_b_out'][j][None, :],
                bb=bb, tm=tm, emit_v=not prompt, ffn_cast=ffn_cast)
            x = res[0]
            if not prompt:
                new_cv.append(res[1])
            cast = res[-2:] if cast_here else []
        st = jnp.zeros((B, FFN_HIST, 2 * D_FF), F32) if prompt else caches[3][i]
        if cast_here:
            ffn_bf16.append(tuple(cast))
        wup, wdn = ffn_bf16[i]
        x, s = _ffn_layer(x, st, gains[2], gains[3], wup, p['ffn_w_dw'][i],
                          p['ffn_b_dw'][i][None, :], wdn, bb=bb, tm=tm)
        new_ffn.append(s)
    cv = jnp.stack(new_cv) if new_cv else None
    return (x, jnp.stack(new_k), jnp.stack(new_v), jnp.stack(new_conv), cv, jnp.stack(new_ffn)), ffn_bf16


def kernel(x_prompt, x_sample, cache_attn_k, cache_attn_v, state_conv, state_ffn_conv, rel_bias_table, norm_gain, attn_w_qkv, attn_b_qkv, attn_w_o, attn_b_o, attn_sinks, conv_w_pw1, conv_b_pw1, conv_w_dw, conv_b_dw, conv_ln_g, conv_ln_b, conv_w_pw2, conv_b_pw2, cmlp_w_in, cmlp_b_in, cmlp_ln_g, cmlp_ln_b, cmlp_w_s, cmlp_b_s, cmlp_w_out, cmlp_b_out, ffn_w_up, ffn_w_dw, ffn_b_dw, ffn_w_down):
    p = {
        'norm_gain': norm_gain,
        'attn_w_qkv': attn_w_qkv.astype(BF16), 'attn_b_qkv': attn_b_qkv,
        'attn_w_o': attn_w_o.astype(BF16), 'attn_b_o': attn_b_o, 'attn_sinks': attn_sinks,
        'conv_w_pw1': conv_w_pw1.astype(BF16), 'conv_b_pw1': conv_b_pw1, 'conv_w_dw': conv_w_dw,
        'conv_b_dw': conv_b_dw, 'conv_ln_g': conv_ln_g, 'conv_ln_b': conv_ln_b,
        'conv_w_pw2': conv_w_pw2.astype(BF16), 'conv_b_pw2': conv_b_pw2,
        'cmlp_w_in': cmlp_w_in.astype(BF16), 'cmlp_b_in': cmlp_b_in, 'cmlp_ln_g': cmlp_ln_g,
        'cmlp_ln_b': cmlp_ln_b, 'cmlp_w_s': cmlp_w_s, 'cmlp_b_s': cmlp_b_s,
        'cmlp_w_out': cmlp_w_out.astype(BF16), 'cmlp_b_out': cmlp_b_out,
        'ffn_w_up': ffn_w_up, 'ffn_w_dw': ffn_w_dw, 'ffn_b_dw': ffn_b_dw, 'ffn_w_down': ffn_w_down,
    }
    bias = _rel_bias(rel_bias_table)
    (y_prompt, p_attn_k, p_attn_v, p_conv, _, p_ffn_conv), ffn_bf16 = _trunk(
        x_prompt, None, p, bias, None, bb=1, tm=512)
    (y_sample, s_attn_k, s_attn_v, s_conv, s_cmlp_v, s_ffn_conv), _ = _trunk(
        x_sample, (cache_attn_k, cache_attn_v, state_conv, state_ffn_conv), p, bias, ffn_bf16, bb=8, tm=64)
    return (y_prompt, y_sample, p_attn_k, p_attn_v, p_conv, p_ffn_conv,
            s_attn_k, s_attn_v, s_conv, s_cmlp_v, s_ffn_conv)
```
